```python
import math
import jax, jax.numpy as jnp
from jax import lax
import numpy as np

D_MODEL = 4096
BATCH = 1
SEQ = 8192
DEPTH = 1
DEC_BATCH = 16
DEC_SEQ = 32
PAST_LEN = 4096

CHUNK = 64
NORM_EPS = 1e-6

SSD_INNER = D_MODEL
SSD_HEADDIM = 64
SSD_HEADS = SSD_INNER // SSD_HEADDIM
SSD_GROUPS = 8
SSD_STATE = 128
SSD_CONV = 4
SSD_CONV_DIM = SSD_INNER + 2 * SSD_GROUPS * SSD_STATE

HG_HEADS = 32
HG_K = 128
HG_V = D_MODEL // HG_HEADS
HG_WIDTH = HG_HEADS * HG_V

N_BRANCH = 2
IN_SIZES = (SSD_INNER, SSD_CONV_DIM, SSD_HEADS,
            HG_HEADS * HG_K, HG_HEADS * HG_K, HG_WIDTH, HG_WIDTH,
            N_BRANCH * D_MODEL)
IN_DIM = sum(IN_SIZES)

PEER_HEADS = 8
PEER_KEYS = 128
PEER_EXPERTS = PEER_KEYS * PEER_KEYS
PEER_TOPK = 16
PEER_QDIM = 256
PEER_BLOCK = 64

kernel_name = "hybrid_ssd_hgrn2_peer_stream_step"


def rmsnorm(x, w):
    xf = x.astype(jnp.float32)
    xf = xf * lax.rsqrt(jnp.mean(xf * xf, axis=-1, keepdims=True) + NORM_EPS)
    return (xf * w.astype(jnp.float32)).astype(x.dtype)


def to_chunks(a, q):
    return jnp.moveaxis(a.reshape(a.shape[0], a.shape[1] // q, q, *a.shape[2:]), 1, 0)


def from_chunks(a):
    a = jnp.moveaxis(a, 0, 1)
    return a.reshape(a.shape[0], a.shape[1] * a.shape[2], *a.shape[3:])


def causal_conv(u, buf, w, b):
    seqlen = u.shape[1]
    up = jnp.concatenate([buf.astype(u.dtype), u], axis=1)
    out = b
    for j in range(SSD_CONV):
        out = out + up[:, j:j + seqlen] * w[j]
    return out, up[:, up.shape[1] - (SSD_CONV - 1):]


def ssd_scan(x, dt, a_neg, bm, cm, s0):
    seqlen = x.shape[1]
    q = min(CHUNK, seqlen)
    mask = jnp.tril(jnp.ones((q, q), dtype=bool))[None, :, :, None, None]

    def step(s, inp):
        xc, dtc, bc, cc = inp
        cum = jnp.cumsum(dtc * a_neg, axis=1)
        seg = cum[:, :, None] - cum[:, None, :]
        decay = jnp.exp(jnp.where(mask, seg, -jnp.inf))
        cb = jnp.einsum('btgn,bsgn->btsg', cc, bc)
        mix = cb[..., None] * decay
        y = jnp.einsum('btsgr,bsgrp->btgrp', mix, xc * dtc[..., None])
        y = y + jnp.einsum('btgn,bgrpn->btgrp', cc, s) * jnp.exp(cum)[..., None]
        tail = jnp.exp(cum[:, -1:] - cum) * dtc
        s = s * jnp.exp(cum[:, -1])[..., None, None] + jnp.einsum('bsgn,bsgr,bsgrp->bgrpn', bc, tail, xc)
        return s, y

    s, ys = lax.scan(step, s0, (to_chunks(x, q), to_chunks(dt, q), to_chunks(bm, q), to_chunks(cm, q)))
    return from_chunks(ys), s


def hgrn2_scan(q, k, v, log_f, s0):
    seqlen = q.shape[1]
    blk = min(CHUNK, seqlen)
    mask = jnp.tril(jnp.ones((blk, blk), dtype=bool))[None, :, :, None, None]

    def step(s, inp):
        qc, kc, vc, lfc = inp
        cum = jnp.cumsum(lfc, axis=1)
        seg = cum[:, :, None] - cum[:, None, :]
        decay = jnp.exp(jnp.where(mask, seg, -jnp.inf))
        att = jnp.einsum('bthk,btshk,bshk->btsh', qc, decay, kc)
        o = jnp.einsum('btsh,bshv->bthv', att, vc)
        o = o + jnp.einsum('bthk,bhkv->bthv', qc * jnp.exp(cum), s)
        s = s * jnp.exp(cum[:, -1])[..., None] + jnp.einsum('bshk,bshv->bhkv', kc * jnp.exp(cum[:, -1:] - cum), vc)
        return s, o

    s, os_ = lax.scan(step, s0, (to_chunks(q, blk), to_chunks(k, blk), to_chunks(v, blk), to_chunks(log_f, blk)))
    return from_chunks(os_), s


def peer_ffn(xn, w_pq, sub_keys, u_experts, v_experts):
    bt, seqlen, d = xn.shape
    n_tok = bt * seqlen
    n_blk = -(-n_tok // PEER_BLOCK)
    flat = jnp.pad(xn.reshape(n_tok, d), ((0, n_blk * PEER_BLOCK - n_tok), (0, 0)))
    sk = sub_keys.astype(jnp.float32)

    def block(xb):
        nb = xb.shape[0]
        q = (xb @ w_pq).astype(jnp.float32).reshape(nb, PEER_HEADS, 2, PEER_QDIM // 2)
        s = jnp.einsum('bhcd,ckd->bhck', q, sk)
        sv, si = lax.top_k(s, PEER_TOPK)
        cand = (sv[:, :, 0, :, None] + sv[:, :, 1, None, :]).reshape(nb, PEER_HEADS, PEER_TOPK * PEER_TOPK)
        cidx = (si[:, :, 0, :, None] * PEER_KEYS + si[:, :, 1, None, :]).reshape(nb, PEER_HEADS, PEER_TOPK * PEER_TOPK)
        top, pos = lax.top_k(cand, PEER_TOPK)
        eidx = jnp.take_along_axis(cidx, pos, axis=-1)
        gate = jax.nn.softmax(top, axis=-1)
        u = u_experts[eidx]
        act = jax.nn.gelu(jnp.einsum('bhkd,bd->bhk', u, xb).astype(jnp.float32))
        wgt = (gate * act).astype(xb.dtype)
        return jnp.einsum('bhk,bhkd->bd', wgt, v_experts[eidx])

    out = lax.map(block, flat.reshape(n_blk, PEER_BLOCK, d))
    return out.reshape(n_blk * PEER_BLOCK, d)[:n_tok].reshape(bt, seqlen, d)


def encoder_layer(x, s_ssd, conv_buf, s_hg, lb, norm1_w, w_in, conv_w, conv_b, dt_bias, a_log, d_skip,
                  ssd_norm_w, w_ssd_out, hg_norm_w, w_hg_out, w_o, norm2_w, w_pq, sub_keys,
                  u_experts, v_experts):
    bt, seqlen, _ = x.shape
    dtype = x.dtype
    f32 = jnp.float32
    G, R, P, N = SSD_GROUPS, SSD_HEADS // SSD_GROUPS, SSD_HEADDIM, SSD_STATE

    xn = rmsnorm(x, norm1_w)
    proj = xn @ w_in
    offs, acc = [], 0
    for size in IN_SIZES[:-1]:
        acc += size
        offs.append(acc)
    z, xbc, dt_raw, hq, hf, hi, hgate, gates = jnp.split(proj, offs, axis=-1)

    xbc, new_conv = causal_conv(xbc, conv_buf, conv_w, conv_b)
    xbc = jax.nn.silu(xbc).astype(f32)
    xs = xbc[..., :SSD_INNER].reshape(bt, seqlen, G, R, P)
    bm = xbc[..., SSD_INNER:SSD_INNER + G * N].reshape(bt, seqlen, G, N)
    cm = xbc[..., SSD_INNER + G * N:].reshape(bt, seqlen, G, N)
    dt = jax.nn.softplus(dt_raw.astype(f32) + dt_bias.astype(f32)).reshape(bt, seqlen, G, R)
    a_neg = -jnp.exp(a_log.astype(f32)).reshape(G, R)
    y, s_ssd_new = ssd_scan(xs, dt, a_neg, bm, cm, s_ssd.astype(f32).reshape(bt, G, R, P, N))
    y = y + d_skip.astype(f32).reshape(G, R, 1) * xs
    y = y.reshape(bt, seqlen, SSD_INNER) * jax.nn.silu(z.astype(f32))
    y = rmsnorm(y.reshape(bt, seqlen, G, SSD_INNER // G), ssd_norm_w.reshape(G, SSD_INNER // G))
    y_ssd = y.reshape(bt, seqlen, SSD_INNER).astype(dtype) @ w_ssd_out

    q = jax.nn.silu(hq.astype(f32)).reshape(bt, seqlen, HG_HEADS, HG_K) * (HG_K ** -0.5)
    f = lb + (1.0 - lb) * jax.nn.sigmoid(hf.astype(f32).reshape(bt, seqlen, HG_HEADS, HG_K))
    v = hi.astype(f32).reshape(bt, seqlen, HG_HEADS, HG_V)
    o, s_hg_new = hgrn2_scan(q, 1.0 - f, v, jnp.log(f), s_hg.astype(f32))
    o = rmsnorm(o, hg_norm_w) * jax.nn.silu(hgate.astype(f32).reshape(bt, seqlen, HG_HEADS, HG_V))
    y_hg = o.reshape(bt, seqlen, HG_WIDTH).astype(dtype) @ w_hg_out

    g_ssd, g_hg = jnp.split(jax.nn.sigmoid(gates), N_BRANCH, axis=-1)
    h = x + (g_ssd * y_ssd + g_hg * y_hg) @ w_o

    h = h + peer_ffn(rmsnorm(h, norm2_w), w_pq, sub_keys, u_experts, v_experts)
    return (h, s_ssd_new.reshape(bt, SSD_HEADS, P, N).astype(dtype), new_conv.astype(dtype),
            s_hg_new.astype(dtype))


def setup_inputs(seed: int = 0) -> dict:
    key = jax.random.key(seed)
    ks = jax.random.split(key, 26)
    nrm = jax.random.normal
    f32 = jnp.float32
    dt0 = jnp.exp(jax.random.uniform(ks[8], (DEPTH, SSD_HEADS), f32) * (math.log(0.1) - math.log(0.001)) + math.log(0.001))
    return {
        "x_prompt": nrm(ks[0], (BATCH, SEQ, D_MODEL), f32),
        "x_sample": nrm(ks[1], (DEC_BATCH, DEC_SEQ, D_MODEL), f32),
        "state_ssd": 0.1 * nrm(ks[2], (DEPTH, DEC_BATCH, SSD_HEADS, SSD_HEADDIM, SSD_STATE), f32),
        "cache_ssd_conv": nrm(ks[3], (DEPTH, DEC_BATCH, SSD_CONV - 1, SSD_CONV_DIM), f32),
        "state_hgrn": 0.5 * nrm(ks[4], (DEPTH, DEC_BATCH, HG_HEADS, HG_K, HG_V), f32),
        "lb_table": 0.1 * nrm(ks[5], (DEPTH + 1, HG_HEADS * HG_K), f32),
        "norm1_w": 1.0 + 0.02 * nrm(ks[6], (DEPTH, D_MODEL), f32),
        "w_in": nrm(ks[7], (DEPTH, D_MODEL, IN_DIM), f32) * D_MODEL ** -0.5,
        "conv_w": nrm(ks[9], (DEPTH, SSD_CONV, SSD_CONV_DIM), f32) * SSD_CONV ** -0.5,
        "conv_b": 0.02 * nrm(ks[10], (DEPTH, SSD_CONV_DIM), f32),
        "dt_bias": dt0 + jnp.log(-jnp.expm1(-dt0)),
        "a_log": jnp.log(jax.random.uniform(ks[11], (DEPTH, SSD_HEADS), f32, 1.0, 16.0)),
        "d_skip": 1.0 + 0.02 * nrm(ks[12], (DEPTH, SSD_HEADS), f32),
        "ssd_norm_w": 1.0 + 0.02 * nrm(ks[13], (DEPTH, SSD_INNER), f32),
        "w_ssd_out": nrm(ks[14], (DEPTH, SSD_INNER, D_MODEL), f32) * SSD_INNER ** -0.5,
        "hg_norm_w": 1.0 + 0.02 * nrm(ks[15], (DEPTH, HG_V), f32),
        "w_hg_out": nrm(ks[16], (DEPTH, HG_WIDTH, D_MODEL), f32) * HG_WIDTH ** -0.5,
        "w_o": nrm(ks[17], (DEPTH, D_MODEL, D_MODEL), f32) * D_MODEL ** -0.5,
        "norm2_w": 1.0 + 0.02 * nrm(ks[18], (DEPTH, D_MODEL), f32),
        "w_pq": nrm(ks[19], (DEPTH, D_MODEL, PEER_HEADS * PEER_QDIM), f32) * D_MODEL ** -0.5,
        "sub_keys": nrm(ks[20], (DEPTH, 2, PEER_KEYS, PEER_QDIM // 2), f32) * (PEER_QDIM // 2) ** -0.5,
        "u_experts": nrm(ks[21], (DEPTH, PEER_EXPERTS, D_MODEL), f32) * D_MODEL ** -0.5,
        "v_experts": 0.5 * nrm(ks[22], (DEPTH, PEER_EXPERTS, D_MODEL), f32),
        "final_norm_w": 1.0 + 0.02 * nrm(ks[23], (D_MODEL,), f32),
    }


def reference(x_prompt, x_sample, state_ssd, cache_ssd_conv, state_hgrn, lb_table, norm1_w, w_in,
              conv_w, conv_b, dt_bias, a_log, d_skip, ssd_norm_w, w_ssd_out, hg_norm_w, w_hg_out, w_o,
              norm2_w, w_pq, sub_keys, u_experts, v_experts, final_norm_w):
    lbs = jnp.cumsum(jax.nn.softmax(lb_table.astype(jnp.float32), axis=0), axis=0)

    def run(x, s_ssd_all, conv_all, s_hg_all):
        h = x
        ssd_out, conv_out, hg_out = [], [], []
        for l in range(DEPTH):
            h, s1, c1, s2 = encoder_layer(
                h, s_ssd_all[l], conv_all[l], s_hg_all[l], lbs[l].reshape(HG_HEADS, HG_K),
                norm1_w[l], w_in[l], conv_w[l], conv_b[l], dt_bias[l], a_log[l], d_skip[l],
                ssd_norm_w[l], w_ssd_out[l], hg_norm_w[l], w_hg_out[l], w_o[l], norm2_w[l],
                w_pq[l], sub_keys[l], u_experts[l], v_experts[l])
            ssd_out.append(s1)
            conv_out.append(c1)
            hg_out.append(s2)
        return rmsnorm(h, final_norm_w), jnp.stack(ssd_out), jnp.stack(conv_out), jnp.stack(hg_out)

    dtype = x_prompt.dtype
    zero_ssd = jnp.zeros((DEPTH, BATCH, SSD_HEADS, SSD_HEADDIM, SSD_STATE), dtype)
    zero_conv = jnp.zeros((DEPTH, BATCH, SSD_CONV - 1, SSD_CONV_DIM), dtype)
    zero_hg = jnp.zeros((DEPTH, BATCH, HG_HEADS, HG_K, HG_V), dtype)
    y_prompt, ssd_p, conv_p, hg_p = run(x_prompt, zero_ssd, zero_conv, zero_hg)
    y_sample, ssd_s, conv_s, hg_s = run(x_sample, state_ssd, cache_ssd_conv, state_hgrn)
    return (y_prompt, y_sample, ssd_p, conv_p, hg_p, ssd_s, conv_s, hg_s)
```

```python
import functools

import jax
import jax.numpy as jnp
from jax import lax
from jax.experimental import pallas as pl
from jax.experimental.pallas import tpu as pltpu

F32 = jnp.float32
BF16 = jnp.bfloat16
HIGHEST = lax.Precision.HIGHEST

NORM_EPS = 1e-6
LANES = 128
VMEM_LIMIT_BYTES = 56 * 1024 * 1024

D_MODEL = 4096
SSD_GROUPS = 8
SSD_HEADS_PER_GROUP = 8
SSD_HEADDIM = 64
SSD_STATE = 128
SSD_HEADS = SSD_GROUPS * SSD_HEADS_PER_GROUP
SSD_INNER = SSD_HEADS * SSD_HEADDIM
SSD_GROUP_WIDTH = SSD_HEADS_PER_GROUP * SSD_HEADDIM
SSD_CONV = 4
SSD_CONV_DIM = SSD_INNER + 2 * SSD_GROUPS * SSD_STATE
HG_HEADS = 32
HG_K = 128
HG_V = 128
CHUNK = 64
PEER_HEADS = 8
PEER_KEYS = 128
PEER_TOPK = 16
PEER_QDIM = 256

COL_Z = 0
COL_XBC = SSD_INNER
COL_HQ = COL_XBC + SSD_CONV_DIM
COL_HF = COL_HQ + HG_HEADS * HG_K
COL_HI = COL_HF + HG_HEADS * HG_K
COL_HG = COL_HI + HG_HEADS * HG_V
COL_GA = COL_HG + HG_HEADS * HG_V
COL_GB = COL_GA + D_MODEL
PROJ_WIDTH = COL_GB + D_MODEL
DT_COL = SSD_INNER + SSD_CONV_DIM


def _largest_divisor(n, candidates):
    for c in candidates:
        if n % c == 0:
            return c
    raise ValueError(f"no block size among {candidates} divides {n}")


def _params(*semantics):
    return pltpu.CompilerParams(dimension_semantics=semantics,
                                vmem_limit_bytes=VMEM_LIMIT_BYTES)


def _sigmoid(x):
    return 1.0 / (1.0 + jnp.exp(-x))


def _silu(x):
    return x * _sigmoid(x)


def _rmsnorm_kernel(x_ref, w_ref, o_ref):
    x = x_ref[...]
    ms = jnp.mean(x * x, axis=-1, keepdims=True)
    o_ref[...] = (x * lax.rsqrt(ms + NORM_EPS) * w_ref[...]).astype(o_ref.dtype)


def _rmsnorm(x, w, out_dtype):
    t, d = x.shape
    bm = _largest_divisor(t, (256, 128, 64, 32, 16, 8))
    return pl.pallas_call(
        _rmsnorm_kernel,
        grid=(t // bm,),
        in_specs=[pl.BlockSpec((bm, d), lambda i: (i, 0)),
                  pl.BlockSpec((1, d), lambda i: (0, 0))],
        out_specs=pl.BlockSpec((bm, d), lambda i: (i, 0)),
        out_shape=jax.ShapeDtypeStruct((t, d), out_dtype),
        compiler_params=_params("parallel"),
        name="rmsnorm",
    )(x, w.reshape(1, d))


def _add_rmsnorm_kernel(a_ref, b_ref, w_ref, o_ref):
    x = a_ref[...] + b_ref[...]
    ms = jnp.mean(x * x, axis=-1, keepdims=True)
    o_ref[...] = x * lax.rsqrt(ms + NORM_EPS) * w_ref[...]


def _add_rmsnorm(a, b, w):
    t, d = a.shape
    bm = _largest_divisor(t, (256, 128, 64, 32, 16, 8))
    return pl.pallas_call(
        _add_rmsnorm_kernel,
        grid=(t // bm,),
        in_specs=[pl.BlockSpec((bm, d), lambda i: (i, 0)),
                  pl.BlockSpec((bm, d), lambda i: (i, 0)),
                  pl.BlockSpec((1, d), lambda i: (0, 0))],
        out_specs=pl.BlockSpec((bm, d), lambda i: (i, 0)),
        out_shape=jax.ShapeDtypeStruct((t, d), F32),
        compiler_params=_params("parallel"),
        name="add_rmsnorm",
    )(a, b, w.reshape(1, d))


def _matmul_kernel(x_ref, w_ref, o_ref):
    o_ref[...] = jnp.dot(x_ref[...], w_ref[...], preferred_element_type=F32).astype(o_ref.dtype)


def _matmul(x, w, out_dtype, bn):
    m, k = x.shape
    n = w.shape[1]
    bm = _largest_divisor(m, (512, 256, 128, 64, 32, 16, 8))
    return pl.pallas_call(
        _matmul_kernel,
        grid=(n // bn, m // bm),
        in_specs=[pl.BlockSpec((bm, k), lambda j, i: (i, 0)),
                  pl.BlockSpec((k, bn), lambda j, i: (0, j))],
        out_specs=pl.BlockSpec((bm, bn), lambda j, i: (i, j)),
        out_shape=jax.ShapeDtypeStruct((m, n), out_dtype),
        compiler_params=_params("parallel", "parallel"),
        name="matmul",
    )(x, w)


def _softplus_matmul_kernel(x_ref, w_ref, b_ref, o_ref):
    a = jnp.dot(x_ref[...], w_ref[...], preferred_element_type=F32) + b_ref[...]
    o_ref[...] = jnp.maximum(a, 0.0) + jnp.log1p(jnp.exp(-jnp.abs(a)))


def _softplus_matmul(x, w, b):
    m, k = x.shape
    n = w.shape[1]
    bm = _largest_divisor(m, (512, 256, 128, 64, 32, 16, 8))
    return pl.pallas_call(
        _softplus_matmul_kernel,
        grid=(m // bm,),
        in_specs=[pl.BlockSpec((bm, k), lambda i: (i, 0)),
                  pl.BlockSpec((k, n), lambda i: (0, 0)),
                  pl.BlockSpec((1, n), lambda i: (0, 0))],
        out_specs=pl.BlockSpec((bm, n), lambda i: (i, 0)),
        out_shape=jax.ShapeDtypeStruct((m, n), F32),
        compiler_params=_params("parallel"),
        name="dt_softplus",
    )(x, w, b)


def _merge_kernel(ya_ref, yb_ref, wa_ref, wb_ref, ga_ref, gb_ref, o_ref):
    pa = jnp.dot(ya_ref[...], wa_ref[...], preferred_element_type=F32)
    pb = jnp.dot(yb_ref[...], wb_ref[...], preferred_element_type=F32)
    o_ref[...] = (_sigmoid(ga_ref[...]) * pa + _sigmoid(gb_ref[...]) * pb).astype(o_ref.dtype)


def _gated_merge(ya, yb, wa, wb, proj):
    m, k = ya.shape
    n = wa.shape[1]
    bm = _largest_divisor(m, (512, 256, 128, 64, 32, 16, 8))
    bn = 512
    return pl.pallas_call(
        _merge_kernel,
        grid=(n // bn, m // bm),
        in_specs=[pl.BlockSpec((bm, k), lambda j, i: (i, 0)),
                  pl.BlockSpec((bm, k), lambda j, i: (i, 0)),
                  pl.BlockSpec((k, bn), lambda j, i: (0, j)),
                  pl.BlockSpec((k, bn), lambda j, i: (0, j)),
                  pl.BlockSpec((bm, bn), lambda j, i: (i, COL_GA // bn + j)),
                  pl.BlockSpec((bm, bn), lambda j, i: (i, COL_GB // bn + j))],
        out_specs=pl.BlockSpec((bm, bn), lambda j, i: (i, j)),
        out_shape=jax.ShapeDtypeStruct((m, n), BF16),
        compiler_params=_params("parallel", "parallel"),
        name="gated_merge",
    )(ya, yb, wa, wb, proj, proj)


def _residual_matmul_kernel(x_ref, w_ref, r_ref, o_ref):
    o_ref[...] = r_ref[...] + jnp.dot(x_ref[...], w_ref[...], preferred_element_type=F32)


def _residual_matmul(x, w, r):
    m, k = x.shape
    n = w.shape[1]
    bm = _largest_divisor(m, (512, 256, 128, 64, 32, 16, 8))
    bn = 1024
    return pl.pallas_call(
        _residual_matmul_kernel,
        grid=(n // bn, m // bm),
        in_specs=[pl.BlockSpec((bm, k), lambda j, i: (i, 0)),
                  pl.BlockSpec((k, bn), lambda j, i: (0, j)),
                  pl.BlockSpec((bm, bn), lambda j, i: (i, j))],
        out_specs=pl.BlockSpec((bm, bn), lambda j, i: (i, j)),
        out_shape=jax.ShapeDtypeStruct((m, n), F32),
        compiler_params=_params("parallel", "parallel"),
        name="residual_matmul",
    )(x, w, r)


CONV_PAD = 8


def _conv_kernel(u_ref, buf_ref, w_ref, b_ref, o_ref, pad_ref, *, tb):
    t = pl.program_id(2)
    lo = CONV_PAD - (SSD_CONV - 1)

    @pl.when(t == 0)
    def _():
        pad_ref[lo:CONV_PAD, :] = buf_ref[0]

    @pl.when(t > 0)
    def _():
        pad_ref[lo:CONV_PAD, :] = pad_ref[tb + lo:tb + CONV_PAD, :]

    pad_ref[CONV_PAD:CONV_PAD + tb, :] = u_ref[...]
    acc = b_ref[...]
    for j in range(SSD_CONV):
        acc = acc + pad_ref[lo + j:lo + j + tb, :] * w_ref[j:j + 1, :]
    o_ref[...] = _silu(acc)


def _conv_silu(proj, row_off, bt, seqlen, buf, conv_w, conv_b):
    tb = _largest_divisor(seqlen, (512, 256, 128, 64, 32))
    cb = 512
    nt = seqlen // tb
    roff = row_off // tb
    coff = COL_XBC // cb
    return pl.pallas_call(
        functools.partial(_conv_kernel, tb=tb),
        grid=(SSD_CONV_DIM // cb, bt, nt),
        in_specs=[pl.BlockSpec((tb, cb), lambda c, b, t: (roff + b * nt + t, coff + c)),
                  pl.BlockSpec((1, SSD_CONV - 1, cb), lambda c, b, t: (b, 0, c)),
                  pl.BlockSpec((SSD_CONV, cb), lambda c, b, t: (0, c)),
                  pl.BlockSpec((1, cb), lambda c, b, t: (0, c))],
        out_specs=pl.BlockSpec((tb, cb), lambda c, b, t: (b * nt + t, c)),
        out_shape=jax.ShapeDtypeStruct((bt * seqlen, SSD_CONV_DIM), F32),
        scratch_shapes=[pltpu.VMEM((CONV_PAD + tb, cb), F32)],
        compiler_params=_params("parallel", "parallel", "arbitrary"),
        name="conv_silu",
    )(proj, buf, conv_w, conv_b.reshape(1, SSD_CONV_DIM))


def _ssd_kernel(x_ref, b_ref, c_ref, z_ref, dtx_ref, dtt_ref, alx_ref, alt_ref, dsk_ref, nw_ref,
                s0_ref, y_ref, sout_ref, st_ref, *, q, nchunk):
    t = pl.program_id(2)
    p = SSD_HEADDIM

    @pl.when(t == 0)
    def _():
        st_ref[...] = s0_ref[0, 0]

    row = lax.broadcasted_iota(jnp.int32, (q, q), 0)
    col = lax.broadcasted_iota(jnp.int32, (q, q), 1)
    lower = row >= col
    tril = lower.astype(F32)
    triu = (row <= col).astype(F32)
    a_x = -jnp.exp(alx_ref[...])
    a_t = -jnp.exp(alt_ref[0])
    d_skip = dsk_ref[...]
    norm_w = nw_ref[...]

    def chunk(ci, carry):
        r0 = pl.multiple_of(ci * q, q)
        x = x_ref[pl.ds(r0, q), :]
        bm = b_ref[pl.ds(r0, q), :]
        cm = c_ref[pl.ds(r0, q), :]
        dtx = dtx_ref[pl.ds(r0, q), :]
        cum = jnp.dot(tril, dtx * a_x, precision=HIGHEST, preferred_element_type=F32)
        cum_t = jnp.dot(dtt_ref[0, ci] * a_t, triu, precision=HIGHEST,
                        preferred_element_type=F32)
        cb = lax.dot_general(cm, bm, (((1,), (1,)), ((), ())), preferred_element_type=F32)
        st = st_ref[...]
        y = jnp.dot(cm, st, preferred_element_type=F32) * jnp.exp(cum)
        xdt = x * dtx
        parts = []
        for r in range(SSD_HEADS_PER_GROUP):
            seg = cum[:, r * p:r * p + q] - cum_t[r:r + 1, :]
            mix = cb * jnp.where(lower, jnp.exp(seg), 0.0)
            parts.append(jnp.dot(mix, xdt[:, r * p:(r + 1) * p], preferred_element_type=F32))
        y = y + jnp.concatenate(parts, axis=1) + d_skip * x
        y = y * _silu(z_ref[pl.ds(r0, q), :])
        ms = jnp.mean(y * y, axis=-1, keepdims=True)
        y_ref[pl.ds(r0, q), :] = (y * lax.rsqrt(ms + NORM_EPS) * norm_w).astype(y_ref.dtype)

        cum_last = cum[q - 1:q, :]
        xt = x * (jnp.exp(cum_last - cum) * dtx)
        st_ref[...] = st * jnp.exp(cum_last) + lax.dot_general(
            bm, xt, (((0,), (0,)), ((), ())), preferred_element_type=F32)
        return carry

    lax.fori_loop(0, nchunk, chunk, 0)

    @pl.when(t == pl.num_programs(2) - 1)
    def _():
        sout_ref[0, 0] = st_ref[...]


def _ssd_scan(xconv, proj, row_off, bt, seqlen, dt, a_log, d_skip, ssd_norm_w, state0):
    g, r, p, n, gw = SSD_GROUPS, SSD_HEADS_PER_GROUP, SSD_HEADDIM, SSD_STATE, SSD_GROUP_WIDTH
    q = min(CHUNK, seqlen)
    tb = _largest_divisor(seqlen, (512, 256, 128, 64, 32))
    nchunk = tb // q
    nt = seqlen // tb
    rows = bt * seqlen
    roff = row_off // tb
    dtx = jnp.repeat(dt, p, axis=1)
    dtt = dt.reshape(rows // q, q, g, r).transpose(2, 0, 3, 1)
    alx = jnp.repeat(a_log, p).reshape(1, SSD_INNER)
    alt = a_log.reshape(g, r, 1)
    dsk = jnp.repeat(d_skip, p).reshape(1, SSD_INNER)
    s0 = state0.reshape(bt, g, r, p, n).transpose(0, 1, 4, 2, 3).reshape(bt, g, n, gw)
    bcol = SSD_INNER // n
    ccol = bcol + g
    y, sout = pl.pallas_call(
        functools.partial(_ssd_kernel, q=q, nchunk=nchunk),
        grid=(bt, g, nt),
        in_specs=[pl.BlockSpec((tb, gw), lambda b, gi, t: (b * nt + t, gi)),
                  pl.BlockSpec((tb, n), lambda b, gi, t: (b * nt + t, bcol + gi)),
                  pl.BlockSpec((tb, n), lambda b, gi, t: (b * nt + t, ccol + gi)),
                  pl.BlockSpec((tb, gw), lambda b, gi, t: (roff + b * nt + t, COL_Z // gw + gi)),
                  pl.BlockSpec((tb, gw), lambda b, gi, t: (b * nt + t, gi)),
                  pl.BlockSpec((1, nchunk, r, q), lambda b, gi, t: (gi, b * nt + t, 0, 0)),
                  pl.BlockSpec((1, gw), lambda b, gi, t: (0, gi)),
                  pl.BlockSpec((1, r, 1), lambda b, gi, t: (gi, 0, 0)),
                  pl.BlockSpec((1, gw), lambda b, gi, t: (0, gi)),
                  pl.BlockSpec((1, gw), lambda b, gi, t: (0, gi)),
                  pl.BlockSpec((1, 1, n, gw), lambda b, gi, t: (b, gi, 0, 0))],
        out_specs=[pl.BlockSpec((tb, gw), lambda b, gi, t: (b * nt + t, gi)),
                   pl.BlockSpec((1, 1, n, gw), lambda b, gi, t: (b, gi, 0, 0))],
        out_shape=[jax.ShapeDtypeStruct((rows, SSD_INNER), BF16),
                   jax.ShapeDtypeStruct((bt, g, n, gw), F32)],
        scratch_shapes=[pltpu.VMEM((n, gw), F32)],
        compiler_params=_params("parallel", "parallel", "arbitrary"),
        name="ssd_scan",
    )(xconv, xconv, xconv, proj, dtx, dtt, alx, alt, dsk, ssd_norm_w.reshape(1, SSD_INNER), s0)
    state = sout.reshape(bt, g, n, r, p).transpose(0, 1, 3, 4, 2).reshape(bt, SSD_HEADS, p, n)
    return y, state


def _hgrn_kernel(q_ref, f_ref, i_ref, g_ref, lbt_ref, nw_ref, s0_ref, o_ref, sout_ref, st_ref,
                 *, q, nchunk):
    t = pl.program_id(2)

    @pl.when(t == 0)
    def _():
        st_ref[...] = s0_ref[0, 0]

    row = lax.broadcasted_iota(jnp.int32, (q, q), 0)
    col = lax.broadcasted_iota(jnp.int32, (q, q), 1)
    lower = row >= col
    tril = lower.astype(F32)
    tab = lbt_ref[...]
    e = jnp.exp(tab - jnp.max(tab, axis=0, keepdims=True))
    lb = e[0:1, :] / jnp.sum(e, axis=0, keepdims=True)
    norm_w = nw_ref[...]
    mid = q // 2

    def chunk(ci, carry):
        r0 = pl.multiple_of(ci * q, q)
        qq = _silu(q_ref[pl.ds(r0, q), :]) * (HG_K ** -0.5)
        f = lb + (1.0 - lb) * _sigmoid(f_ref[pl.ds(r0, q), :])
        kk = 1.0 - f
        v = i_ref[pl.ds(r0, q), :]
        cum = jnp.dot(tril, jnp.log(f), precision=HIGHEST, preferred_element_type=F32)
        ref = cum[mid - 1:mid, :]
        att = lax.dot_general(qq * jnp.exp(cum - ref), kk * jnp.exp(ref - cum),
                              (((1,), (1,)), ((), ())), preferred_element_type=F32)
        att = jnp.where(lower, att, 0.0)
        st = st_ref[...]
        o = jnp.dot(att, v, preferred_element_type=F32)
        o = o + lax.dot_general(qq * jnp.exp(cum), st, (((1,), (1,)), ((), ())),
                                preferred_element_type=F32)
        ms = jnp.mean(o * o, axis=-1, keepdims=True)
        o = o * lax.rsqrt(ms + NORM_EPS) * norm_w * _silu(g_ref[pl.ds(r0, q), :])
        o_ref[pl.ds(r0, q), :] = o.astype(o_ref.dtype)

        cum_last = cum[q - 1:q, :]
        kt = kk * jnp.exp(cum_last - cum)
        st_ref[...] = st * jnp.exp(cum_last) + lax.dot_general(
            v, kt, (((0,), (0,)), ((), ())), preferred_element_type=F32)
        return carry

    lax.fori_loop(0, nchunk, chunk, 0)

    @pl.when(t == pl.num_programs(2) - 1)
    def _():
        sout_ref[0, 0] = st_ref[...]


def _hgrn_scan(proj, row_off, bt, seqlen, lb_table, hg_norm_w, state0):
    q = min(CHUNK, seqlen)
    tb = _largest_divisor(seqlen, (512, 256, 128, 64, 32))
    nchunk = tb // q
    nt = seqlen // tb
    rows = bt * seqlen
    roff = row_off // tb
    s0 = state0.transpose(0, 1, 3, 2)
    nrow = lb_table.shape[0]

    def col_spec(col0):
        return pl.BlockSpec((tb, LANES), lambda b, h, t: (roff + b * nt + t, col0 // LANES + h))

    o, sout = pl.pallas_call(
        functools.partial(_hgrn_kernel, q=q, nchunk=nchunk),
        grid=(bt, HG_HEADS, nt),
        in_specs=[col_spec(COL_HQ), col_spec(COL_HF), col_spec(COL_HI), col_spec(COL_HG),
                  pl.BlockSpec((nrow, HG_K), lambda b, h, t: (0, h)),
                  pl.BlockSpec((1, HG_V), lambda b, h, t: (0, 0)),
                  pl.BlockSpec((1, 1, HG_V, HG_K), lambda b, h, t: (b, h, 0, 0))],
        out_specs=[pl.BlockSpec((tb, HG_V), lambda b, h, t: (b * nt + t, h)),
                   pl.BlockSpec((1, 1, HG_V, HG_K), lambda b, h, t: (b, h, 0, 0))],
        out_shape=[jax.ShapeDtypeStruct((rows, HG_HEADS * HG_V), BF16),
                   jax.ShapeDtypeStruct((bt, HG_HEADS, HG_V, HG_K), F32)],
        scratch_shapes=[pltpu.VMEM((HG_V, HG_K), F32)],
        compiler_params=_params("parallel", "parallel", "arbitrary"),
        name="hgrn_scan",
    )(proj, proj, proj, proj, lb_table, hg_norm_w.reshape(1, HG_V), s0)
    return o, sout.transpose(0, 1, 3, 2)


def _top_values(s, count):
    vals = []
    work = s
    for _ in range(count):
        m = jnp.max(work, axis=0, keepdims=True)
        vals.append(m)
        work = jnp.where(work == m, -jnp.inf, work)
    return jnp.concatenate(vals, axis=0)


def _peer_query_kernel(w_ref, x_ref, sk_ref, s0_ref, e0_ref, s1_ref, e1_ref, tau_ref):
    half = PEER_QDIM // 2
    qt = jnp.dot(w_ref[...], x_ref[...], preferred_element_type=F32)
    s0 = jnp.dot(sk_ref[0], qt[:half], precision=HIGHEST, preferred_element_type=F32)
    s1 = jnp.dot(sk_ref[1], qt[half:], precision=HIGHEST, preferred_element_type=F32)
    sv0 = _top_values(s0, PEER_TOPK)
    sv1 = _top_values(s1, PEER_TOPK)
    cand = [sv0[0:1] + sv1]
    for a in range(1, 8):
        cand.append(sv0[a:a + 1] + sv1[0:8])
    cand.append(sv0[8:16] + sv1[0:1])
    top = _top_values(jnp.concatenate(cand, axis=0), PEER_TOPK)
    z = jnp.sum(jnp.exp(top - top[0:1]), axis=0, keepdims=True)
    s0_ref[0] = s0
    s1_ref[0] = s1
    e0_ref[0] = jnp.exp(s0 - sv0[0:1]) / z
    e1_ref[0] = jnp.exp(s1 - sv1[0:1])
    tau_ref[0] = top[PEER_TOPK - 1:PEER_TOPK]


def _peer_query(w_pq_t, xn_t, sub_keys):
    d, t = xn_t.shape
    tb = _largest_divisor(t, (256, 128))
    keyed = jax.ShapeDtypeStruct((PEER_HEADS, PEER_KEYS, t), F32)
    key_spec = pl.BlockSpec((1, PEER_KEYS, tb), lambda i, h: (h, 0, i))
    return pl.pallas_call(
        _peer_query_kernel,
        grid=(t // tb, PEER_HEADS),
        in_specs=[pl.BlockSpec((PEER_QDIM, d), lambda i, h: (h, 0)),
                  pl.BlockSpec((d, tb), lambda i, h: (0, i)),
                  pl.BlockSpec((2, PEER_KEYS, PEER_QDIM // 2), lambda i, h: (0, 0, 0))],
        out_specs=[key_spec, key_spec, key_spec, key_spec,
                   pl.BlockSpec((1, 1, tb), lambda i, h: (h, 0, i))],
        out_shape=[keyed, keyed, keyed, keyed,
                   jax.ShapeDtypeStruct((PEER_HEADS, 1, t), F32)],
        compiler_params=_params("parallel", "arbitrary"),
        name="peer_query",
    )(w_pq_t, xn_t, sub_keys)


def _gelu_tanh(x):
    return 0.5 * x * (1.0 + jnp.tanh(0.7978845608028654 * (x + 0.044715 * (x * x * x))))


def _peer_expert_kernel(x_ref, u_ref, v_ref, s0_ref, e0_ref, s1_ref, e1_ref, tau_ref, o_ref, w_ref,
                        *, nb, tb):
    e = pl.program_id(1)

    @pl.when(e == 0)
    def _():
        o_ref[...] = jnp.zeros_like(o_ref)

    act = _gelu_tanh(jnp.dot(u_ref[...], x_ref[...], preferred_element_type=F32))

    for i in range(nb):
        for tl in range(tb // LANES):
            lanes = pl.ds(tl * LANES, LANES)
            acc = jnp.zeros((PEER_KEYS, LANES), F32)
            for h in range(PEER_HEADS):
                s0 = s0_ref[h, 0, i:i + 1, lanes]
                c0 = e0_ref[h, 0, i:i + 1, lanes]
                sel = (s1_ref[h, :, lanes] + s0) >= tau_ref[h, :, lanes]
                acc = acc + jnp.where(sel, e1_ref[h, :, lanes] * c0, 0.0)
            w_ref[i * PEER_KEYS:(i + 1) * PEER_KEYS, lanes] = acc
    g = (w_ref[...] * act).astype(BF16)
    o_ref[...] += lax.dot_general(g, v_ref[...], (((0,), (0,)), ((), ())),
                                  preferred_element_type=F32)


def _peer_experts(xn_t, u, v, s0, e0, s1, e1, tau):
    d, t = xn_t.shape
    n_exp = u.shape[0]
    tb = _largest_divisor(t, (512, 256, 128))
    nb = 4
    eb = nb * PEER_KEYS
    once = pl.Buffered(1)
    keyed = pl.BlockSpec((PEER_HEADS, PEER_KEYS, tb), lambda i, e: (0, 0, i), pipeline_mode=once)
    rows = pl.BlockSpec((PEER_HEADS, 1, nb, tb), lambda i, e: (0, e, 0, i))
    s0 = s0.reshape(PEER_HEADS, PEER_KEYS // nb, nb, t)
    e0 = e0.reshape(PEER_HEADS, PEER_KEYS // nb, nb, t)
    return pl.pallas_call(
        functools.partial(_peer_expert_kernel, nb=nb, tb=tb),
        grid=(t // tb, n_exp // eb),
        in_specs=[pl.BlockSpec((d, tb), lambda i, e: (0, i), pipeline_mode=once),
                  pl.BlockSpec((eb, d), lambda i, e: (e, 0)),
                  pl.BlockSpec((eb, d), lambda i, e: (e, 0)),
                  rows, rows, keyed, keyed,
                  pl.BlockSpec((PEER_HEADS, 1, tb), lambda i, e: (0, 0, i), pipeline_mode=once)],
        out_specs=pl.BlockSpec((tb, d), lambda i, e: (i, 0)),
        out_shape=jax.ShapeDtypeStruct((t, d), F32),
        scratch_shapes=[pltpu.VMEM((eb, tb), F32)],
        compiler_params=_params("parallel", "arbitrary"),
        name="peer_experts",
    )(xn_t, u, v, s0, e0, s1, e1, tau)


def _branches(proj, dt, row_off, bt, seqlen, conv_buf, s_ssd, s_hg, conv_w, conv_b, a_log, d_skip,
              ssd_norm_w, lb_table, hg_norm_w):
    rows = bt * seqlen
    xconv = _conv_silu(proj, row_off, bt, seqlen, conv_buf, conv_w, conv_b)
    y_ssd, s_ssd_new = _ssd_scan(xconv, proj, row_off, bt, seqlen, dt[row_off:row_off + rows],
                                 a_log, d_skip, ssd_norm_w, s_ssd)
    o_hg, s_hg_new = _hgrn_scan(proj, row_off, bt, seqlen, lb_table, hg_norm_w, s_hg)
    keep = SSD_CONV - 1
    if seqlen >= keep:
        u = proj[row_off:row_off + rows, COL_XBC:COL_XBC + SSD_CONV_DIM]
        new_conv = u.reshape(bt, seqlen, SSD_CONV_DIM)[:, seqlen - keep:]
    else:
        raise NotImplementedError("sequences shorter than the conv cache")
    return y_ssd, o_hg, s_ssd_new, new_conv, s_hg_new


def kernel(x_prompt, x_sample, state_ssd, cache_ssd_conv, state_hgrn, lb_table, norm1_w, w_in, conv_w, conv_b, dt_bias, a_log, d_skip, ssd_norm_w, w_ssd_out, hg_norm_w, w_hg_out, w_o, norm2_w, w_pq, sub_keys, u_experts, v_experts, final_norm_w):
    assert w_in.shape[0] == 1 and lb_table.shape[0] == 2, "single-layer stack"
    bp, lp, d = x_prompt.shape
    bs, ls, _ = x_sample.shape
    tp, ts = bp * lp, bs * ls
    x_all = jnp.concatenate([x_prompt.reshape(tp, d), x_sample.reshape(ts, d)], axis=0)

    w_in0 = w_in[0]
    w_main = jnp.concatenate([w_in0[:, :DT_COL], w_in0[:, DT_COL + SSD_HEADS:]], axis=1).astype(BF16)
    w_dt = jnp.pad(w_in0[:, DT_COL:DT_COL + SSD_HEADS], ((0, 0), (0, LANES - SSD_HEADS))).astype(BF16)
    b_dt = jnp.pad(dt_bias[0], (0, LANES - SSD_HEADS)).reshape(1, LANES)

    xn = _rmsnorm(x_all, norm1_w[0], BF16)
    proj = _matmul(xn, w_main, F32, bn=1024)
    dt = _softplus_matmul(xn, w_dt, b_dt)[:, :SSD_HEADS]

    zeros = functools.partial(jnp.zeros, dtype=F32)
    common = (conv_w[0], conv_b[0], a_log[0], d_skip[0], ssd_norm_w[0], lb_table, hg_norm_w[0])
    ya_p, yb_p, ssd_p, conv_p, hg_p = _branches(
        proj, dt, 0, bp, lp,
        zeros((bp, SSD_CONV - 1, SSD_CONV_DIM)), zeros((bp, SSD_HEADS, SSD_HEADDIM, SSD_STATE)),
        zeros((bp, HG_HEADS, HG_K, HG_V)), *common)
    ya_s, yb_s, ssd_s, conv_s, hg_s = _branches(
        proj, dt, tp, bs, ls, cache_ssd_conv[0], state_ssd[0], state_hgrn[0], *common)

    ya = jnp.concatenate([ya_p, ya_s], axis=0)
    yb = jnp.concatenate([yb_p, yb_s], axis=0)
    merged = _gated_merge(ya, yb, w_ssd_out[0].astype(BF16), w_hg_out[0].astype(BF16), proj)
    h = _residual_matmul(merged, w_o[0].astype(BF16), x_all)

    hn_t = _rmsnorm(h, norm2_w[0], BF16).T
    s0, e0, s1, e1, tau = _peer_query(w_pq[0].T.astype(BF16), hn_t, sub_keys[0])
    peer = _peer_experts(hn_t, u_experts[0].astype(BF16), v_experts[0].astype(BF16),
                         s0, e0, s1, e1, tau)
    y = _add_rmsnorm(h, peer, final_norm_w)

    y_prompt = y[:tp].reshape(bp, lp, d)
    y_sample = y[tp:].reshape(bs, ls, d)
    return (y_prompt, y_sample, ssd_p[None], conv_p[None], hg_p[None],
            ssd_s[None], conv_s[None], hg_s[None])
```

```python
import functools

import jax
import jax.numpy as jnp
from jax import lax
from jax.experimental import pallas as pl
from jax.experimental.pallas import tpu as pltpu

F32 = jnp.float32
BF16 = jnp.bfloat16
HIGHEST = lax.Precision.HIGHEST

NORM_EPS = 1e-6
LANES = 128
VMEM_LIMIT_BYTES = 56 * 1024 * 1024

D_MODEL = 4096
SSD_GROUPS = 8
SSD_HEADS_PER_GROUP = 8
SSD_HEADDIM = 64
SSD_STATE = 128
SSD_HEADS = SSD_GROUPS * SSD_HEADS_PER_GROUP
SSD_INNER = SSD_HEADS * SSD_HEADDIM
SSD_GROUP_WIDTH = SSD_HEADS_PER_GROUP * SSD_HEADDIM
SSD_CONV = 4
SSD_CONV_DIM = SSD_INNER + 2 * SSD_GROUPS * SSD_STATE
HG_HEADS = 32
HG_K = 128
HG_V = 128
CHUNK = 64
PEER_HEADS = 8
PEER_KEYS = 128
PEER_TOPK = 16
PEER_QDIM = 256
HGRN_HEADS_PER_STEP = 8

COL_Z = 0
COL_XBC = SSD_INNER
COL_HQ = COL_XBC + SSD_CONV_DIM
COL_HF = COL_HQ + HG_HEADS * HG_K
COL_HI = COL_HF + HG_HEADS * HG_K
COL_HG = COL_HI + HG_HEADS * HG_V
COL_GA = COL_HG + HG_HEADS * HG_V
COL_GB = COL_GA + D_MODEL
PROJ_WIDTH = COL_GB + D_MODEL
DT_COL = SSD_INNER + SSD_CONV_DIM


def _largest_divisor(n, candidates):
    for c in candidates:
        if n % c == 0:
            return c
    raise ValueError(f"no block size among {candidates} divides {n}")


def _params(*semantics):
    return pltpu.CompilerParams(dimension_semantics=semantics,
                                vmem_limit_bytes=VMEM_LIMIT_BYTES)


def _sigmoid(x):
    return 1.0 / (1.0 + jnp.exp(-x))


def _silu(x):
    return x * _sigmoid(x)


def _stacked_specs(block, n_first, row_pos, col_of=None):
    def first(*idx):
        return (jnp.minimum(idx[row_pos], n_first - 1), 0 if col_of is None else col_of(*idx))

    def second(*idx):
        return (jnp.maximum(idx[row_pos] - n_first, 0), 0 if col_of is None else col_of(*idx))

    return pl.BlockSpec(block, first), pl.BlockSpec(block, second)


def _stacked_block(i, n_first, first_ref, second_ref):
    return jnp.where(i < n_first, first_ref[...], second_ref[...])


def _row_block(n_first_rows, n_second_rows, candidates):
    for c in candidates:
        if n_first_rows % c == 0 and n_second_rows % c == 0:
            return c
    raise ValueError(f"no block size among {candidates} divides {n_first_rows} and {n_second_rows}")


def _rmsnorm_stacked_kernel(xa_ref, xb_ref, w_ref, o_ref, *, n_first):
    x = _stacked_block(pl.program_id(0), n_first, xa_ref, xb_ref)
    ms = jnp.mean(x * x, axis=-1, keepdims=True)
    o_ref[...] = (x * lax.rsqrt(ms + NORM_EPS) * w_ref[...]).astype(o_ref.dtype)


def _rmsnorm_stacked(xa, xb, w, out_dtype):
    (ta, d), tb_rows = xa.shape, xb.shape[0]
    bm = _row_block(ta, tb_rows, (256, 128, 64, 32, 16, 8))
    n_first = ta // bm
    spec_a, spec_b = _stacked_specs((bm, d), n_first, 0)
    return pl.pallas_call(
        functools.partial(_rmsnorm_stacked_kernel, n_first=n_first),
        grid=((ta + tb_rows) // bm,),
        in_specs=[spec_a, spec_b, pl.BlockSpec((1, d), lambda i: (0, 0))],
        out_specs=pl.BlockSpec((bm, d), lambda i: (i, 0)),
        out_shape=jax.ShapeDtypeStruct((ta + tb_rows, d), out_dtype),
        compiler_params=_params("parallel"),
        name="rmsnorm_in",
    )(xa, xb, w.reshape(1, d))


def _rmsnorm_kernel(x_ref, w_ref, o_ref):
    x = x_ref[...]
    ms = jnp.mean(x * x, axis=-1, keepdims=True)
    o_ref[...] = (x * lax.rsqrt(ms + NORM_EPS) * w_ref[...]).astype(o_ref.dtype)


def _rmsnorm(x, w, out_dtype):
    t, d = x.shape
    bm = _largest_divisor(t, (256, 128, 64, 32, 16, 8))
    return pl.pallas_call(
        _rmsnorm_kernel,
        grid=(t // bm,),
        in_specs=[pl.BlockSpec((bm, d), lambda i: (i, 0)),
                  pl.BlockSpec((1, d), lambda i: (0, 0))],
        out_specs=pl.BlockSpec((bm, d), lambda i: (i, 0)),
        out_shape=jax.ShapeDtypeStruct((t, d), out_dtype),
        compiler_params=_params("parallel"),
        name="rmsnorm",
    )(x, w.reshape(1, d))


def _add_rmsnorm_kernel(a_ref, b_ref, w_ref, o_ref):
    x = a_ref[...] + b_ref[...]
    ms = jnp.mean(x * x, axis=-1, keepdims=True)
    o_ref[...] = x * lax.rsqrt(ms + NORM_EPS) * w_ref[...]


def _add_rmsnorm(a, b, w, row_off, rows):
    d = a.shape[1]
    bm = _row_block(rows, row_off, (256, 128, 64, 32, 16, 8))
    off = row_off // bm
    return pl.pallas_call(
        _add_rmsnorm_kernel,
        grid=(rows // bm,),
        in_specs=[pl.BlockSpec((bm, d), lambda i: (off + i, 0)),
                  pl.BlockSpec((bm, d), lambda i: (off + i, 0)),
                  pl.BlockSpec((1, d), lambda i: (0, 0))],
        out_specs=pl.BlockSpec((bm, d), lambda i: (i, 0)),
        out_shape=jax.ShapeDtypeStruct((rows, d), F32),
        compiler_params=_params("parallel"),
        name="add_rmsnorm",
    )(a, b, w.reshape(1, d))


def _in_proj_kernel(x_ref, wa_ref, wb_ref, o_ref, wbuf_ref, *, n_plain, shift, chunk):
    j = pl.program_id(0)
    i = pl.program_id(1)
    nchunks = wa_ref.shape[0] // chunk

    @pl.when(jnp.logical_and(i == 0, j < n_plain))
    def _():
        def body(c, carry):
            rows = pl.ds(pl.multiple_of(c * chunk, chunk), chunk)
            wbuf_ref[rows, :] = wa_ref[rows, :].astype(BF16)
            return carry
        lax.fori_loop(0, nchunks, body, 0)

    @pl.when(jnp.logical_and(i == 0, j >= n_plain))
    def _():
        def body(c, carry):
            rows = pl.ds(pl.multiple_of(c * chunk, chunk), chunk)
            a = wa_ref[rows, :]
            b = wb_ref[rows, :]
            wbuf_ref[rows, :] = jnp.concatenate([a[:, shift:], b[:, :shift]], axis=1).astype(BF16)
            return carry
        lax.fori_loop(0, nchunks, body, 0)

    o_ref[...] = jnp.dot(x_ref[...], wbuf_ref[...], preferred_element_type=F32)


def _in_proj(x, w):
    m, k = x.shape
    bm = _largest_divisor(m, (512, 256, 128, 64, 32, 16, 8))
    bn = 1024
    assert DT_COL % bn == 0 and PROJ_WIDTH % bn == 0
    once = pl.Buffered(1)
    return pl.pallas_call(
        functools.partial(_in_proj_kernel, n_plain=DT_COL // bn, shift=SSD_HEADS, chunk=128),
        grid=(PROJ_WIDTH // bn, m // bm),
        in_specs=[pl.BlockSpec((bm, k), lambda j, i: (i, 0)),
                  pl.BlockSpec((k, bn), lambda j, i: (0, j), pipeline_mode=once),
                  pl.BlockSpec((k, LANES), lambda j, i: (0, (bn // LANES) * (j + 1)),
                               pipeline_mode=once)],
        out_specs=pl.BlockSpec((bm, bn), lambda j, i: (i, j)),
        out_shape=jax.ShapeDtypeStruct((m, PROJ_WIDTH), F32),
        scratch_shapes=[pltpu.VMEM((k, bn), BF16)],
        compiler_params=_params("arbitrary", "arbitrary"),
        name="in_proj",
    )(x, w, w)


def _softplus_matmul_kernel(x_ref, w_ref, b_ref, o_ref):
    a = jnp.dot(x_ref[...], w_ref[...], preferred_element_type=F32) + b_ref[...]
    o_ref[...] = jnp.maximum(a, 0.0) + jnp.log1p(jnp.exp(-jnp.abs(a)))


def _softplus_matmul(x, w, b):
    m, k = x.shape
    n = w.shape[1]
    bm = _largest_divisor(m, (512, 256, 128, 64, 32, 16, 8))
    return pl.pallas_call(
        _softplus_matmul_kernel,
        grid=(m // bm,),
        in_specs=[pl.BlockSpec((bm, k), lambda i: (i, 0)),
                  pl.BlockSpec((k, n), lambda i: (0, 0)),
                  pl.BlockSpec((1, n), lambda i: (0, 0))],
        out_specs=pl.BlockSpec((bm, n), lambda i: (i, 0)),
        out_shape=jax.ShapeDtypeStruct((m, n), F32),
        compiler_params=_params("parallel"),
        name="dt_softplus",
    )(x, w, b)


def _merge_kernel(ya1_ref, ya2_ref, yb1_ref, yb2_ref, wa_ref, wb_ref, ga_ref, gb_ref, o_ref, *, n_first):
    i = pl.program_id(1)
    ya = _stacked_block(i, n_first, ya1_ref, ya2_ref)
    yb = _stacked_block(i, n_first, yb1_ref, yb2_ref)
    pa = jnp.dot(ya, wa_ref[...], preferred_element_type=F32)
    pb = jnp.dot(yb, wb_ref[...], preferred_element_type=F32)
    o_ref[...] = (_sigmoid(ga_ref[...]) * pa + _sigmoid(gb_ref[...]) * pb).astype(o_ref.dtype)


def _gated_merge(ya_parts, yb_parts, wa, wb, proj):
    (m1, k), m2 = ya_parts[0].shape, ya_parts[1].shape[0]
    n = wa.shape[1]
    bm = _row_block(m1, m2, (512, 256, 128, 64, 32, 16, 8))
    bn = 512
    n_first = m1 // bm
    y_first, y_second = _stacked_specs((bm, k), n_first, 1)
    return pl.pallas_call(
        functools.partial(_merge_kernel, n_first=n_first),
        grid=(n // bn, (m1 + m2) // bm),
        in_specs=[y_first, y_second, y_first, y_second,
                  pl.BlockSpec((k, bn), lambda j, i: (0, j)),
                  pl.BlockSpec((k, bn), lambda j, i: (0, j)),
                  pl.BlockSpec((bm, bn), lambda j, i: (i, COL_GA // bn + j)),
                  pl.BlockSpec((bm, bn), lambda j, i: (i, COL_GB // bn + j))],
        out_specs=pl.BlockSpec((bm, bn), lambda j, i: (i, j)),
        out_shape=jax.ShapeDtypeStruct((m1 + m2, n), BF16),
        compiler_params=_params("parallel", "parallel"),
        name="gated_merge",
    )(ya_parts[0], ya_parts[1], yb_parts[0], yb_parts[1], wa, wb, proj, proj)


def _residual_matmul_kernel(x_ref, w_ref, r1_ref, r2_ref, o_ref, *, n_first):
    r = _stacked_block(pl.program_id(1), n_first, r1_ref, r2_ref)
    o_ref[...] = r + jnp.dot(x_ref[...], w_ref[...], preferred_element_type=F32)


def _residual_matmul(x, w, r_parts):
    m, k = x.shape
    n = w.shape[1]
    m1, m2 = r_parts[0].shape[0], r_parts[1].shape[0]
    bm = _row_block(m1, m2, (512, 256, 128, 64, 32, 16, 8))
    bn = 1024
    n_first = m1 // bm
    r_first, r_second = _stacked_specs((bm, bn), n_first, 1, col_of=lambda j, i: j)
    return pl.pallas_call(
        functools.partial(_residual_matmul_kernel, n_first=n_first),
        grid=(n // bn, m // bm),
        in_specs=[pl.BlockSpec((bm, k), lambda j, i: (i, 0)),
                  pl.BlockSpec((k, bn), lambda j, i: (0, j)),
                  r_first, r_second],
        out_specs=pl.BlockSpec((bm, bn), lambda j, i: (i, j)),
        out_shape=jax.ShapeDtypeStruct((m, n), F32),
        compiler_params=_params("parallel", "parallel"),
        name="residual_matmul",
    )(x, w, r_parts[0], r_parts[1])


CONV_PAD = 8


def _conv_kernel(u_ref, buf_ref, w_ref, b_ref, o_ref, cache_ref, pad_ref, *, tb):
    t = pl.program_id(2)
    lo = CONV_PAD - (SSD_CONV - 1)

    @pl.when(t == 0)
    def _():
        pad_ref[lo:CONV_PAD, :] = buf_ref[0]

    @pl.when(t > 0)
    def _():
        pad_ref[lo:CONV_PAD, :] = pad_ref[tb + lo:tb + CONV_PAD, :]

    pad_ref[CONV_PAD:CONV_PAD + tb, :] = u_ref[...]
    acc = b_ref[...]
    for j in range(SSD_CONV):
        acc = acc + pad_ref[lo + j:lo + j + tb, :] * w_ref[j:j + 1, :]
    o_ref[...] = _silu(acc)

    @pl.when(t == pl.num_programs(2) - 1)
    def _():
        cache_ref[0] = pad_ref[tb + lo:tb + CONV_PAD, :]


def _conv_silu(proj, row_off, bt, seqlen, buf, conv_w, conv_b):
    tb = _largest_divisor(seqlen, (512, 256, 128, 64, 32))
    cb = 512
    nt = seqlen // tb
    roff = row_off // tb
    coff = COL_XBC // cb
    return pl.pallas_call(
        functools.partial(_conv_kernel, tb=tb),
        grid=(SSD_CONV_DIM // cb, bt, nt),
        in_specs=[pl.BlockSpec((tb, cb), lambda c, b, t: (roff + b * nt + t, coff + c)),
                  pl.BlockSpec((1, SSD_CONV - 1, cb), lambda c, b, t: (b, 0, c)),
                  pl.BlockSpec((SSD_CONV, cb), lambda c, b, t: (0, c)),
                  pl.BlockSpec((1, cb), lambda c, b, t: (0, c))],
        out_specs=[pl.BlockSpec((tb, cb), lambda c, b, t: (b * nt + t, c)),
                   pl.BlockSpec((1, SSD_CONV - 1, cb), lambda c, b, t: (b, 0, c))],
        out_shape=[jax.ShapeDtypeStruct((bt * seqlen, SSD_CONV_DIM), F32),
                   jax.ShapeDtypeStruct((bt, SSD_CONV - 1, SSD_CONV_DIM), F32)],
        scratch_shapes=[pltpu.VMEM((CONV_PAD + tb, cb), F32)],
        compiler_params=_params("parallel", "parallel", "arbitrary"),
        name="conv_silu",
    )(proj, buf, conv_w, conv_b.reshape(1, SSD_CONV_DIM))


def _ssd_kernel(x_ref, b_ref, c_ref, z_ref, dtx_ref, dtt_ref, alx_ref, alt_ref, dsk_ref, nw_ref,
                s0_ref, y_ref, sout_ref, st_ref, *, q, nchunk):
    t = pl.program_id(2)
    p = SSD_HEADDIM

    @pl.when(t == 0)
    def _():
        st_ref[...] = s0_ref[0, 0]

    row = lax.broadcasted_iota(jnp.int32, (q, q), 0)
    col = lax.broadcasted_iota(jnp.int32, (q, q), 1)
    lower = row >= col
    tril = lower.astype(F32)
    triu = (row <= col).astype(F32)
    a_x = -jnp.exp(alx_ref[...])
    a_t = -jnp.exp(alt_ref[0])
    d_skip = dsk_ref[...]
    norm_w = nw_ref[...]

    def chunk(ci, carry):
        r0 = pl.multiple_of(ci * q, q)
        x = x_ref[pl.ds(r0, q), :]
        bm = b_ref[pl.ds(r0, q), :]
        cm = c_ref[pl.ds(r0, q), :]
        dtx = dtx_ref[pl.ds(r0, q), :]
        cum = jnp.dot(tril, dtx * a_x, precision=HIGHEST, preferred_element_type=F32)
        cum_t = jnp.dot(dtt_ref[0, ci] * a_t, triu, precision=HIGHEST,
                        preferred_element_type=F32)
        cb = lax.dot_general(cm, bm, (((1,), (1,)), ((), ())), preferred_element_type=F32)
        st = st_ref[...]
        y = jnp.dot(cm, st, preferred_element_type=F32) * jnp.exp(cum)
        xdt = x * dtx
        parts = []
        for r in range(SSD_HEADS_PER_GROUP):
            seg = cum[:, r * p:r * p + q] - cum_t[r:r + 1, :]
            mix = cb * jnp.where(lower, jnp.exp(seg), 0.0)
            parts.append(jnp.dot(mix, xdt[:, r * p:(r + 1) * p], preferred_element_type=F32))
        y = y + jnp.concatenate(parts, axis=1) + d_skip * x
        y = y * _silu(z_ref[pl.ds(r0, q), :])
        ms = jnp.mean(y * y, axis=-1, keepdims=True)
        y_ref[pl.ds(r0, q), :] = (y * lax.rsqrt(ms + NORM_EPS) * norm_w).astype(y_ref.dtype)

        cum_last = cum[q - 1:q, :]
        xt = x * (jnp.exp(cum_last - cum) * dtx)
        st_ref[...] = st * jnp.exp(cum_last) + lax.dot_general(
            bm, xt, (((0,), (0,)), ((), ())), preferred_element_type=F32)
        return carry

    lax.fori_loop(0, nchunk, chunk, 0)

    @pl.when(t == pl.num_programs(2) - 1)
    def _():
        sout_ref[0, 0] = st_ref[...]


def _ssd_scan(xconv, proj, row_off, bt, seqlen, dt, a_log, d_skip, ssd_norm_w, state0):
    g, r, p, n, gw = SSD_GROUPS, SSD_HEADS_PER_GROUP, SSD_HEADDIM, SSD_STATE, SSD_GROUP_WIDTH
    q = min(CHUNK, seqlen)
    tb = _largest_divisor(seqlen, (512, 256, 128, 64, 32))
    nchunk = tb // q
    nt = seqlen // tb
    rows = bt * seqlen
    roff = row_off // tb
    dtx = jnp.repeat(dt, p, axis=1)
    dtt = dt.reshape(rows // q, q, g, r).transpose(2, 0, 3, 1)
    alx = jnp.repeat(a_log, p).reshape(1, SSD_INNER)
    alt = a_log.reshape(g, r, 1)
    dsk = jnp.repeat(d_skip, p).reshape(1, SSD_INNER)
    s0 = state0.reshape(bt, g, r, p, n).transpose(0, 1, 4, 2, 3).reshape(bt, g, n, gw)
    bcol = SSD_INNER // n
    ccol = bcol + g
    y, sout = pl.pallas_call(
        functools.partial(_ssd_kernel, q=q, nchunk=nchunk),
        grid=(bt, g, nt),
        in_specs=[pl.BlockSpec((tb, gw), lambda b, gi, t: (b * nt + t, gi)),
                  pl.BlockSpec((tb, n), lambda b, gi, t: (b * nt + t, bcol + gi)),
                  pl.BlockSpec((tb, n), lambda b, gi, t: (b * nt + t, ccol + gi)),
                  pl.BlockSpec((tb, gw), lambda b, gi, t: (roff + b * nt + t, COL_Z // gw + gi)),
                  pl.BlockSpec((tb, gw), lambda b, gi, t: (b * nt + t, gi)),
                  pl.BlockSpec((1, nchunk, r, q), lambda b, gi, t: (gi, b * nt + t, 0, 0)),
                  pl.BlockSpec((1, gw), lambda b, gi, t: (0, gi)),
                  pl.BlockSpec((1, r, 1), lambda b, gi, t: (gi, 0, 0)),
                  pl.BlockSpec((1, gw), lambda b, gi, t: (0, gi)),
                  pl.BlockSpec((1, gw), lambda b, gi, t: (0, gi)),
                  pl.BlockSpec((1, 1, n, gw), lambda b, gi, t: (b, gi, 0, 0))],
        out_specs=[pl.BlockSpec((tb, gw), lambda b, gi, t: (b * nt + t, gi)),
                   pl.BlockSpec((1, 1, n, gw), lambda b, gi, t: (b, gi, 0, 0))],
        out_shape=[jax.ShapeDtypeStruct((rows, SSD_INNER), BF16),
                   jax.ShapeDtypeStruct((bt, g, n, gw), F32)],
        scratch_shapes=[pltpu.VMEM((n, gw), F32)],
        compiler_params=_params("parallel", "parallel", "arbitrary"),
        name="ssd_scan",
    )(xconv, xconv, xconv, proj, dtx, dtt, alx, alt, dsk, ssd_norm_w.reshape(1, SSD_INNER), s0)
    state = sout.reshape(bt, g, n, r, p).transpose(0, 1, 3, 4, 2).reshape(bt, SSD_HEADS, p, n)
    return y, state


def _cumsum_rows(tril_bf16, x):
    hi = x.astype(BF16)
    rest = x - hi.astype(F32)
    mid = rest.astype(BF16)
    lo = (rest - mid.astype(F32)).astype(BF16)
    return (jnp.dot(tril_bf16, hi, preferred_element_type=F32)
            + jnp.dot(tril_bf16, mid, preferred_element_type=F32)
            + jnp.dot(tril_bf16, lo, preferred_element_type=F32))


def _hgrn_kernel(q_ref, f_ref, i_ref, g_ref, lbt_ref, nw_ref, s0_ref, o_ref, sout_ref, st_ref,
                 *, q, nchunk, nh):
    t = pl.program_id(2)

    @pl.when(t == 0)
    def _():
        st_ref[...] = s0_ref[0]

    row = lax.broadcasted_iota(jnp.int32, (q, q), 0)
    col = lax.broadcasted_iota(jnp.int32, (q, q), 1)
    lower = row >= col
    tril = lower.astype(BF16)
    tab = lbt_ref[...]
    e = jnp.exp(tab - jnp.max(tab, axis=0, keepdims=True))
    lb = e[0:1, :] / jnp.sum(e, axis=0, keepdims=True)
    norm_w = nw_ref[...]
    mid = q // 2
    nt_dims = (((1,), (1,)), ((), ()))
    tn_dims = (((0,), (0,)), ((), ()))

    def chunk(ci, carry):
        rows = pl.ds(pl.multiple_of(ci * q, q), q)
        qq = _silu(q_ref[rows, :]) * (HG_K ** -0.5)
        f = lb + (1.0 - lb) * _sigmoid(f_ref[rows, :])
        kk = 1.0 - f
        v = i_ref[rows, :].astype(BF16)
        gate = _silu(g_ref[rows, :])
        cum = _cumsum_rows(tril, jnp.log(f))
        ref = cum[mid - 1:mid, :]
        cum_last = cum[q - 1:q, :]
        qd = (qq * jnp.exp(cum - ref)).astype(BF16)
        kd = (kk * jnp.exp(ref - cum)).astype(BF16)
        qe = (qq * jnp.exp(cum)).astype(BF16)
        kt = (kk * jnp.exp(cum_last - cum)).astype(BF16)
        dec = jnp.exp(cum_last)
        for h in range(nh):
            sl = slice(h * HG_K, (h + 1) * HG_K)
            att = lax.dot_general(qd[:, sl], kd[:, sl], nt_dims, preferred_element_type=F32)
            att = jnp.where(lower, att, 0.0).astype(BF16)
            st = st_ref[h]
            o = jnp.dot(att, v[:, sl], preferred_element_type=F32)
            o = o + lax.dot_general(qe[:, sl], st.astype(BF16), nt_dims,
                                    preferred_element_type=F32)
            ms = jnp.mean(o * o, axis=-1, keepdims=True)
            o_ref[rows, sl] = (o * lax.rsqrt(ms + NORM_EPS) * norm_w * gate[:, sl]).astype(o_ref.dtype)
            st_ref[h] = st * dec[:, sl] + lax.dot_general(v[:, sl], kt[:, sl], tn_dims,
                                                          preferred_element_type=F32)
        return carry

    lax.fori_loop(0, nchunk, chunk, 0)

    @pl.when(t == pl.num_programs(2) - 1)
    def _():
        sout_ref[0] = st_ref[...]


def _hgrn_scan(proj, row_off, bt, seqlen, lb_table, hg_norm_w, state0):
    q = min(CHUNK, seqlen)
    tb = _largest_divisor(seqlen, (512, 256, 128, 64, 32))
    nchunk = tb // q
    nt = seqlen // tb
    rows = bt * seqlen
    roff = row_off // tb
    nh = HGRN_HEADS_PER_STEP
    hw = nh * HG_K
    s0 = state0.transpose(0, 1, 3, 2)
    nrow = lb_table.shape[0]

    def col_spec(col0):
        return pl.BlockSpec((tb, hw), lambda b, h, t: (roff + b * nt + t, col0 // hw + h))

    o, sout = pl.pallas_call(
        functools.partial(_hgrn_kernel, q=q, nchunk=nchunk, nh=nh),
        grid=(bt, HG_HEADS // nh, nt),
        in_specs=[col_spec(COL_HQ), col_spec(COL_HF), col_spec(COL_HI), col_spec(COL_HG),
                  pl.BlockSpec((nrow, hw), lambda b, h, t: (0, h)),
                  pl.BlockSpec((1, HG_V), lambda b, h, t: (0, 0)),
                  pl.BlockSpec((1, nh, HG_V, HG_K), lambda b, h, t: (b, h, 0, 0))],
        out_specs=[pl.BlockSpec((tb, hw), lambda b, h, t: (b * nt + t, h)),
                   pl.BlockSpec((1, nh, HG_V, HG_K), lambda b, h, t: (b, h, 0, 0))],
        out_shape=[jax.ShapeDtypeStruct((rows, HG_HEADS * HG_V), BF16),
                   jax.ShapeDtypeStruct((bt, HG_HEADS, HG_V, HG_K), F32)],
        scratch_shapes=[pltpu.VMEM((nh, HG_V, HG_K), F32)],
        compiler_params=_params("parallel", "parallel", "arbitrary"),
        name="hgrn_scan",
    )(proj, proj, proj, proj, lb_table, hg_norm_w.reshape(1, HG_V), s0)
    return o, sout.transpose(0, 1, 3, 2)


def _top_values(s, count):
    vals = []
    work = s
    for _ in range(count):
        m = jnp.max(work, axis=0, keepdims=True)
        vals.append(m)
        work = jnp.where(work == m, -jnp.inf, work)
    return jnp.concatenate(vals, axis=0)


def _peer_query_kernel(w_ref, x_ref, sk_ref, thr_ref, e0_ref, s1_ref, e1_ref):
    half = PEER_QDIM // 2
    qt = jnp.dot(w_ref[...], x_ref[...], preferred_element_type=F32)
    s0 = jnp.dot(sk_ref[0], qt[:half], precision=HIGHEST, preferred_element_type=F32)
    s1 = jnp.dot(sk_ref[1], qt[half:], precision=HIGHEST, preferred_element_type=F32)
    sv0 = _top_values(s0, PEER_TOPK)
    sv1 = _top_values(s1, PEER_TOPK)
    cand = [sv0[0:1] + sv1]
    for a in range(1, 8):
        cand.append(sv0[a:a + 1] + sv1[0:8])
    cand.append(sv0[8:16] + sv1[0:1])
    top = _top_values(jnp.concatenate(cand, axis=0), PEER_TOPK)
    z = jnp.sum(jnp.exp(top - top[0:1]), axis=0, keepdims=True)
    tau = top[PEER_TOPK - 1:PEER_TOPK]
    thr = jnp.full(s0.shape, jnp.inf, F32)
    for b in range(PEER_TOPK):
        thr = jnp.where(s0 + sv1[b:b + 1] >= tau, sv1[b:b + 1], thr)
    thr_ref[0] = thr
    s1_ref[0] = s1
    e0_ref[0] = jnp.exp(s0 - sv0[0:1]) / z
    e1_ref[0] = jnp.exp(s1 - sv1[0:1])


def _peer_query(w_pq_t, xn_t, sub_keys):
    d, t = xn_t.shape
    tb = _largest_divisor(t, (256, 128))
    keyed = jax.ShapeDtypeStruct((PEER_HEADS, PEER_KEYS, t), F32)
    key_spec = pl.BlockSpec((1, PEER_KEYS, tb), lambda i, h: (h, 0, i))
    return pl.pallas_call(
        _peer_query_kernel,
        grid=(t // tb, PEER_HEADS),
        in_specs=[pl.BlockSpec((PEER_QDIM, d), lambda i, h: (h, 0)),
                  pl.BlockSpec((d, tb), lambda i, h: (0, i)),
                  pl.BlockSpec((2, PEER_KEYS, PEER_QDIM // 2), lambda i, h: (0, 0, 0))],
        out_specs=[key_spec, key_spec, key_spec, key_spec],
        out_shape=[keyed, keyed, keyed, keyed],
        compiler_params=_params("parallel", "arbitrary"),
        name="peer_query",
    )(w_pq_t, xn_t, sub_keys)


def _gelu_tanh(x):
    return 0.5 * x * (1.0 + jnp.tanh(0.7978845608028654 * (x + 0.044715 * (x * x * x))))


def _peer_expert_kernel(x_ref, u_ref, v_ref, thr_ref, e0_ref, s1_ref, e1_ref, o_ref, w_ref,
                        *, nb, tb):
    e = pl.program_id(1)

    @pl.when(e == 0)
    def _():
        o_ref[...] = jnp.zeros_like(o_ref)

    act = _gelu_tanh(jnp.dot(u_ref[...], x_ref[...], preferred_element_type=F32))
    for i in range(nb):
        for tl in range(tb // LANES):
            lanes = pl.ds(tl * LANES, LANES)
            acc = jnp.zeros((PEER_KEYS, LANES), F32)
            for h in range(PEER_HEADS):
                thr = thr_ref[h, 0, i:i + 1, lanes]
                c0 = e0_ref[h, 0, i:i + 1, lanes]
                acc = acc + jnp.where(s1_ref[h, :, lanes] >= thr, e1_ref[h, :, lanes] * c0, 0.0)
            w_ref[i * PEER_KEYS:(i + 1) * PEER_KEYS, lanes] = acc
    g = (w_ref[...] * act).astype(BF16)
    o_ref[...] += lax.dot_general(g, v_ref[...], (((0,), (0,)), ((), ())),
                                  preferred_element_type=F32)


def _peer_experts(xn_t, u, v, thr, e0, s1, e1):
    d, t = xn_t.shape
    n_exp = u.shape[0]
    tb = _largest_divisor(t, (512, 256, 128))
    nb = 4
    eb = nb * PEER_KEYS
    once = pl.Buffered(1)
    keyed = pl.BlockSpec((PEER_HEADS, PEER_KEYS, tb), lambda i, e: (0, 0, i), pipeline_mode=once)
    rows = pl.BlockSpec((PEER_HEADS, 1, nb, tb), lambda i, e: (0, e, 0, i))
    thr = thr.reshape(PEER_HEADS, PEER_KEYS // nb, nb, t)
    e0 = e0.reshape(PEER_HEADS, PEER_KEYS // nb, nb, t)
    return pl.pallas_call(
        functools.partial(_peer_expert_kernel, nb=nb, tb=tb),
        grid=(t // tb, n_exp // eb),
        in_specs=[pl.BlockSpec((d, tb), lambda i, e: (0, i), pipeline_mode=once),
                  pl.BlockSpec((eb, d), lambda i, e: (e, 0)),
                  pl.BlockSpec((eb, d), lambda i, e: (e, 0)),
                  rows, rows, keyed, keyed],
        out_specs=pl.BlockSpec((tb, d), lambda i, e: (i, 0)),
        out_shape=jax.ShapeDtypeStruct((t, d), F32),
        scratch_shapes=[pltpu.VMEM((eb, tb), F32)],
        compiler_params=_params("parallel", "arbitrary"),
        name="peer_experts",
    )(xn_t, u, v, thr, e0, s1, e1)


def _branches(proj, dt, row_off, bt, seqlen, conv_buf, s_ssd, s_hg, conv_w, conv_b, a_log, d_skip,
              ssd_norm_w, lb_table, hg_norm_w):
    rows = bt * seqlen
    xconv, new_conv = _conv_silu(proj, row_off, bt, seqlen, conv_buf, conv_w, conv_b)
    y_ssd, s_ssd_new = _ssd_scan(xconv, proj, row_off, bt, seqlen, dt[row_off:row_off + rows],
                                 a_log, d_skip, ssd_norm_w, s_ssd)
    o_hg, s_hg_new = _hgrn_scan(proj, row_off, bt, seqlen, lb_table, hg_norm_w, s_hg)
    return y_ssd, o_hg, s_ssd_new, new_conv, s_hg_new


def kernel(x_prompt, x_sample, state_ssd, cache_ssd_conv, state_hgrn, lb_table, norm1_w, w_in, conv_w, conv_b, dt_bias, a_log, d_skip, ssd_norm_w, w_ssd_out, hg_norm_w, w_hg_out, w_o, norm2_w, w_pq, sub_keys, u_experts, v_experts, final_norm_w):
    assert w_in.shape[0] == 1 and lb_table.shape[0] == 2, "single-layer stack"
    bp, lp, d = x_prompt.shape
    bs, ls, _ = x_sample.shape
    tp, ts = bp * lp, bs * ls
    x_parts = (x_prompt.reshape(tp, d), x_sample.reshape(ts, d))

    w_in0 = w_in[0]
    w_dt = jnp.pad(w_in0[:, DT_COL:DT_COL + SSD_HEADS], ((0, 0), (0, LANES - SSD_HEADS))).astype(BF16)
    b_dt = jnp.pad(dt_bias[0], (0, LANES - SSD_HEADS)).reshape(1, LANES)

    xn = _rmsnorm_stacked(*x_parts, norm1_w[0], BF16)
    proj = _in_proj(xn, w_in0)
    dt = _softplus_matmul(xn, w_dt, b_dt)[:, :SSD_HEADS]

    zeros = functools.partial(jnp.zeros, dtype=F32)
    common = (conv_w[0], conv_b[0], a_log[0], d_skip[0], ssd_norm_w[0], lb_table, hg_norm_w[0])
    ya_p, yb_p, ssd_p, conv_p, hg_p = _branches(
        proj, dt, 0, bp, lp,
        zeros((bp, SSD_CONV - 1, SSD_CONV_DIM)), zeros((bp, SSD_HEADS, SSD_HEADDIM, SSD_STATE)),
        zeros((bp, HG_HEADS, HG_K, HG_V)), *common)
    ya_s, yb_s, ssd_s, conv_s, hg_s = _branches(
        proj, dt, tp, bs, ls, cache_ssd_conv[0], state_ssd[0], state_hgrn[0], *common)

    merged = _gated_merge((ya_p, ya_s), (yb_p, yb_s), w_ssd_out[0].astype(BF16),
                          w_hg_out[0].astype(BF16), proj)
    h = _residual_matmul(merged, w_o[0].astype(BF16), x_parts)

    hn_t = _rmsnorm(h, norm2_w[0], BF16).T
    thr, e0, s1, e1 = _peer_query(w_pq[0].T.astype(BF16), hn_t, sub_keys[0])
    peer = _peer_experts(hn_t, u_experts[0].astype(BF16), v_experts[0].astype(BF16),
                         thr, e0, s1, e1)
    y_prompt = _add_rmsnorm(h, peer, final_norm_w, 0, tp).reshape(bp, lp, d)
    y_sample = _add_rmsnorm(h, peer, final_norm_w, tp, ts).reshape(bs, ls, d)
    return (y_prompt, y_sample, ssd_p[None], conv_p[None], hg_p[None],
            ssd_s[None], conv_s[None], hg_s[None])
```

```python
import functools

import jax
import jax.numpy as jnp
from jax import lax
from jax.experimental import pallas as pl
from jax.experimental.pallas import tpu as pltpu

F32 = jnp.float32
BF16 = jnp.bfloat16
HIGHEST = lax.Precision.HIGHEST

NORM_EPS = 1e-6
LANES = 128
VMEM_LIMIT_BYTES = 56 * 1024 * 1024

D_MODEL = 4096
SSD_GROUPS = 8
SSD_HEADS_PER_GROUP = 8
SSD_HEADDIM = 64
SSD_STATE = 128
SSD_HEADS = SSD_GROUPS * SSD_HEADS_PER_GROUP
SSD_INNER = SSD_HEADS * SSD_HEADDIM
SSD_GROUP_WIDTH = SSD_HEADS_PER_GROUP * SSD_HEADDIM
SSD_CONV = 4
SSD_CONV_DIM = SSD_INNER + 2 * SSD_GROUPS * SSD_STATE
HG_HEADS = 32
HG_K = 128
HG_V = 128
CHUNK = 64
PEER_HEADS = 8
PEER_KEYS = 128
PEER_TOPK = 16
PEER_QDIM = 256
SSD_GROUPS_PER_STEP = 2
HGRN_HEADS_PER_STEP = 8

COL_Z = 0
COL_XBC = SSD_INNER
COL_HQ = COL_XBC + SSD_CONV_DIM
COL_HF = COL_HQ + HG_HEADS * HG_K
COL_HI = COL_HF + HG_HEADS * HG_K
COL_HG = COL_HI + HG_HEADS * HG_V
COL_GA = COL_HG + HG_HEADS * HG_V
COL_GB = COL_GA + D_MODEL
PROJ_WIDTH = COL_GB + D_MODEL
DT_COL = SSD_INNER + SSD_CONV_DIM


def _largest_divisor(n, candidates):
    for c in candidates:
        if n % c == 0:
            return c
    raise ValueError(f"no block size among {candidates} divides {n}")


def _params(*semantics):
    return pltpu.CompilerParams(dimension_semantics=semantics,
                                vmem_limit_bytes=VMEM_LIMIT_BYTES)


def _sigmoid(x):
    return 1.0 / (1.0 + jnp.exp(-x))


def _silu(x):
    return x * _sigmoid(x)


def _stacked_specs(block, n_first, row_pos, col_of=None):
    def first(*idx):
        return (jnp.minimum(idx[row_pos], n_first - 1), 0 if col_of is None else col_of(*idx))

    def second(*idx):
        return (jnp.maximum(idx[row_pos] - n_first, 0), 0 if col_of is None else col_of(*idx))

    return pl.BlockSpec(block, first), pl.BlockSpec(block, second)


def _stacked_block(i, n_first, first_ref, second_ref):
    return jnp.where(i < n_first, first_ref[...], second_ref[...])


def _row_block(n_first_rows, n_second_rows, candidates):
    for c in candidates:
        if n_first_rows % c == 0 and n_second_rows % c == 0:
            return c
    raise ValueError(f"no block size among {candidates} divides {n_first_rows} and {n_second_rows}")


def _rmsnorm_stacked_kernel(xa_ref, xb_ref, w_ref, o_ref, *, n_first):
    x = _stacked_block(pl.program_id(0), n_first, xa_ref, xb_ref)
    ms = jnp.mean(x * x, axis=-1, keepdims=True)
    o_ref[...] = (x * lax.rsqrt(ms + NORM_EPS) * w_ref[...]).astype(o_ref.dtype)


def _rmsnorm_stacked(xa, xb, w, out_dtype):
    (ta, d), tb_rows = xa.shape, xb.shape[0]
    bm = _row_block(ta, tb_rows, (256, 128, 64, 32, 16, 8))
    n_first = ta // bm
    spec_a, spec_b = _stacked_specs((bm, d), n_first, 0)
    return pl.pallas_call(
        functools.partial(_rmsnorm_stacked_kernel, n_first=n_first),
        grid=((ta + tb_rows) // bm,),
        in_specs=[spec_a, spec_b, pl.BlockSpec((1, d), lambda i: (0, 0))],
        out_specs=pl.BlockSpec((bm, d), lambda i: (i, 0)),
        out_shape=jax.ShapeDtypeStruct((ta + tb_rows, d), out_dtype),
        compiler_params=_params("parallel"),
        name="rmsnorm_in",
    )(xa, xb, w.reshape(1, d))


def _rmsnorm_t_kernel(x_ref, w_ref, o_ref):
    x = x_ref[...]
    ms = jnp.mean(x * x, axis=-1, keepdims=True)
    o_ref[...] = jnp.transpose(x * lax.rsqrt(ms + NORM_EPS) * w_ref[...]).astype(o_ref.dtype)


def _rmsnorm_t(x, w, out_dtype):
    t, d = x.shape
    bm = _largest_divisor(t, (256, 128))
    return pl.pallas_call(
        _rmsnorm_t_kernel,
        grid=(t // bm,),
        in_specs=[pl.BlockSpec((bm, d), lambda i: (i, 0)),
                  pl.BlockSpec((1, d), lambda i: (0, 0))],
        out_specs=pl.BlockSpec((d, bm), lambda i: (0, i)),
        out_shape=jax.ShapeDtypeStruct((d, t), out_dtype),
        compiler_params=_params("parallel"),
        name="rmsnorm_t",
    )(x, w.reshape(1, d))


def _add_rmsnorm_kernel(a_ref, b_ref, w_ref, o_ref):
    x = a_ref[...] + b_ref[...]
    ms = jnp.mean(x * x, axis=-1, keepdims=True)
    o_ref[...] = x * lax.rsqrt(ms + NORM_EPS) * w_ref[...]


def _add_rmsnorm(a, b, w, row_off, rows):
    d = a.shape[1]
    bm = _row_block(rows, row_off, (256, 128, 64, 32, 16, 8))
    off = row_off // bm
    return pl.pallas_call(
        _add_rmsnorm_kernel,
        grid=(rows // bm,),
        in_specs=[pl.BlockSpec((bm, d), lambda i: (off + i, 0)),
                  pl.BlockSpec((bm, d), lambda i: (off + i, 0)),
                  pl.BlockSpec((1, d), lambda i: (0, 0))],
        out_specs=pl.BlockSpec((bm, d), lambda i: (i, 0)),
        out_shape=jax.ShapeDtypeStruct((rows, d), F32),
        compiler_params=_params("parallel"),
        name="add_rmsnorm",
    )(a, b, w.reshape(1, d))


def _in_proj_kernel(x_ref, wa_ref, wb_ref, o_ref, wbuf_ref, *, n_plain, shift, chunk):
    j = pl.program_id(0)
    i = pl.program_id(1)
    bn = wa_ref.shape[0]

    @pl.when(jnp.logical_and(i == 0, j < n_plain))
    def _():
        def body(c, carry):
            rows = pl.ds(pl.multiple_of(c * chunk, chunk), chunk)
            wbuf_ref[rows, :] = wa_ref[rows, :].astype(BF16)
            return carry
        lax.fori_loop(0, bn // chunk, body, 0)

    @pl.when(jnp.logical_and(i == 0, j >= n_plain))
    def _():
        def body(c, carry):
            dst = pl.ds(pl.multiple_of(c * chunk, chunk), chunk)
            src = pl.ds(pl.multiple_of(c * chunk + shift, shift), chunk)
            wbuf_ref[dst, :] = wa_ref[src, :].astype(BF16)
            return carry
        lax.fori_loop(0, (bn - shift) // chunk, body, 0)
        wbuf_ref[bn - shift:bn, :] = wb_ref[...].astype(BF16)

    o_ref[...] = lax.dot_general(x_ref[...], wbuf_ref[...], (((1,), (1,)), ((), ())),
                                 preferred_element_type=F32)


def _in_proj(x, w_t):
    m, k = x.shape
    bm = _largest_divisor(m, (512, 256, 128, 64, 32, 16, 8))
    bn = 1024
    shift = SSD_HEADS
    assert DT_COL % bn == 0 and PROJ_WIDTH % bn == 0 and bn % shift == 0
    once = pl.Buffered(1)
    return pl.pallas_call(
        functools.partial(_in_proj_kernel, n_plain=DT_COL // bn, shift=shift, chunk=shift),
        grid=(PROJ_WIDTH // bn, m // bm),
        in_specs=[pl.BlockSpec((bm, k), lambda j, i: (i, 0)),
                  pl.BlockSpec((bn, k), lambda j, i: (j, 0), pipeline_mode=once),
                  pl.BlockSpec((shift, k), lambda j, i: ((bn // shift) * (j + 1), 0),
                               pipeline_mode=once)],
        out_specs=pl.BlockSpec((bm, bn), lambda j, i: (i, j)),
        out_shape=jax.ShapeDtypeStruct((m, PROJ_WIDTH), F32),
        scratch_shapes=[pltpu.VMEM((bn, k), BF16)],
        compiler_params=_params("arbitrary", "arbitrary"),
        name="in_proj",
    )(x, w_t, w_t)


def _split3_dot(x, rhs_bf16):
    hi = x.astype(BF16)
    rest = x - hi.astype(F32)
    mid = rest.astype(BF16)
    lo = (rest - mid.astype(F32)).astype(BF16)
    return (jnp.dot(hi, rhs_bf16, preferred_element_type=F32)
            + jnp.dot(mid, rhs_bf16, preferred_element_type=F32)
            + jnp.dot(lo, rhs_bf16, preferred_element_type=F32))


def _dt_kernel(x_ref, w_ref, b_ref, ex_ref, dt_ref, dtx_ref):
    a = lax.dot_general(x_ref[...], w_ref[...], (((1,), (1,)), ((), ())),
                        preferred_element_type=F32) + b_ref[...]
    dt = jnp.maximum(a, 0.0) + jnp.log1p(jnp.exp(-jnp.abs(a)))
    dt_ref[...] = dt
    dtx_ref[...] = _split3_dot(dt, ex_ref[...])


def _dt_proj(x, w_t, b):
    m, k = x.shape
    n = w_t.shape[0]
    bm = _largest_divisor(m, (512, 256, 128, 64, 32, 16, 8))
    expand = (jnp.arange(SSD_INNER)[None, :] // SSD_HEADDIM == jnp.arange(n)[:, None]).astype(BF16)
    return pl.pallas_call(
        _dt_kernel,
        grid=(m // bm,),
        in_specs=[pl.BlockSpec((bm, k), lambda i: (i, 0)),
                  pl.BlockSpec((n, k), lambda i: (0, 0)),
                  pl.BlockSpec((1, n), lambda i: (0, 0)),
                  pl.BlockSpec((n, SSD_INNER), lambda i: (0, 0))],
        out_specs=[pl.BlockSpec((bm, n), lambda i: (i, 0)),
                   pl.BlockSpec((bm, SSD_INNER), lambda i: (i, 0))],
        out_shape=[jax.ShapeDtypeStruct((m, n), F32),
                   jax.ShapeDtypeStruct((m, SSD_INNER), F32)],
        compiler_params=_params("parallel"),
        name="dt_proj",
    )(x, w_t, b, expand)


def _merge_kernel(ya1_ref, ya2_ref, yb1_ref, yb2_ref, wa_ref, wb_ref, ga_ref, gb_ref, o_ref, *, n_first):
    i = pl.program_id(1)
    ya = _stacked_block(i, n_first, ya1_ref, ya2_ref)
    yb = _stacked_block(i, n_first, yb1_ref, yb2_ref)
    pa = jnp.dot(ya, wa_ref[...], preferred_element_type=F32)
    pb = jnp.dot(yb, wb_ref[...], preferred_element_type=F32)
    o_ref[...] = (_sigmoid(ga_ref[...]) * pa + _sigmoid(gb_ref[...]) * pb).astype(o_ref.dtype)


def _gated_merge(ya_parts, yb_parts, wa, wb, proj):
    (m1, k), m2 = ya_parts[0].shape, ya_parts[1].shape[0]
    n = wa.shape[1]
    bm = _row_block(m1, m2, (512, 256, 128, 64, 32, 16, 8))
    bn = 512
    n_first = m1 // bm
    y_first, y_second = _stacked_specs((bm, k), n_first, 1)
    return pl.pallas_call(
        functools.partial(_merge_kernel, n_first=n_first),
        grid=(n // bn, (m1 + m2) // bm),
        in_specs=[y_first, y_second, y_first, y_second,
                  pl.BlockSpec((k, bn), lambda j, i: (0, j)),
                  pl.BlockSpec((k, bn), lambda j, i: (0, j)),
                  pl.BlockSpec((bm, bn), lambda j, i: (i, COL_GA // bn + j)),
                  pl.BlockSpec((bm, bn), lambda j, i: (i, COL_GB // bn + j))],
        out_specs=pl.BlockSpec((bm, bn), lambda j, i: (i, j)),
        out_shape=jax.ShapeDtypeStruct((m1 + m2, n), BF16),
        compiler_params=_params("parallel", "parallel"),
        name="gated_merge",
    )(ya_parts[0], ya_parts[1], yb_parts[0], yb_parts[1], wa, wb, proj, proj)


def _residual_matmul_kernel(x_ref, w_ref, r1_ref, r2_ref, o_ref, *, n_first):
    r = _stacked_block(pl.program_id(1), n_first, r1_ref, r2_ref)
    o_ref[...] = r + jnp.dot(x_ref[...], w_ref[...], preferred_element_type=F32)


def _residual_matmul(x, w, r_parts):
    m, k = x.shape
    n = w.shape[1]
    m1, m2 = r_parts[0].shape[0], r_parts[1].shape[0]
    bm = _row_block(m1, m2, (512, 256, 128, 64, 32, 16, 8))
    bn = 1024
    n_first = m1 // bm
    r_first, r_second = _stacked_specs((bm, bn), n_first, 1, col_of=lambda j, i: j)
    return pl.pallas_call(
        functools.partial(_residual_matmul_kernel, n_first=n_first),
        grid=(n // bn, m // bm),
        in_specs=[pl.BlockSpec((bm, k), lambda j, i: (i, 0)),
                  pl.BlockSpec((k, bn), lambda j, i: (0, j)),
                  r_first, r_second],
        out_specs=pl.BlockSpec((bm, bn), lambda j, i: (i, j)),
        out_shape=jax.ShapeDtypeStruct((m, n), F32),
        compiler_params=_params("parallel", "parallel"),
        name="residual_matmul",
    )(x, w, r_parts[0], r_parts[1])


CONV_PAD = 8


def _conv_kernel(u_ref, buf_ref, w_ref, b_ref, o_ref, cache_ref, pad_ref, *, tb):
    t = pl.program_id(2)
    lo = CONV_PAD - (SSD_CONV - 1)

    @pl.when(t == 0)
    def _():
        pad_ref[lo:CONV_PAD, :] = buf_ref[0]

    @pl.when(t > 0)
    def _():
        pad_ref[lo:CONV_PAD, :] = pad_ref[tb + lo:tb + CONV_PAD, :]

    pad_ref[CONV_PAD:CONV_PAD + tb, :] = u_ref[...]
    acc = b_ref[...]
    for j in range(SSD_CONV):
        acc = acc + pad_ref[lo + j:lo + j + tb, :] * w_ref[j:j + 1, :]
    o_ref[...] = _silu(acc)

    @pl.when(t == pl.num_programs(2) - 1)
    def _():
        cache_ref[0] = pad_ref[tb + lo:tb + CONV_PAD, :]


def _conv_silu(proj, row_off, bt, seqlen, buf, conv_w, conv_b):
    tb = _largest_divisor(seqlen, (512, 256, 128, 64, 32))
    cb = 512
    nt = seqlen // tb
    roff = row_off // tb
    coff = COL_XBC // cb
    return pl.pallas_call(
        functools.partial(_conv_kernel, tb=tb),
        grid=(SSD_CONV_DIM // cb, bt, nt),
        in_specs=[pl.BlockSpec((tb, cb), lambda c, b, t: (roff + b * nt + t, coff + c)),
                  pl.BlockSpec((1, SSD_CONV - 1, cb), lambda c, b, t: (b, 0, c)),
                  pl.BlockSpec((SSD_CONV, cb), lambda c, b, t: (0, c)),
                  pl.BlockSpec((1, cb), lambda c, b, t: (0, c))],
        out_specs=[pl.BlockSpec((tb, cb), lambda c, b, t: (b * nt + t, c)),
                   pl.BlockSpec((1, SSD_CONV - 1, cb), lambda c, b, t: (b, 0, c))],
        out_shape=[jax.ShapeDtypeStruct((bt * seqlen, SSD_CONV_DIM), F32),
                   jax.ShapeDtypeStruct((bt, SSD_CONV - 1, SSD_CONV_DIM), F32)],
        scratch_shapes=[pltpu.VMEM((CONV_PAD + tb, cb), F32)],
        compiler_params=_params("parallel", "parallel", "arbitrary"),
        name="conv_silu",
    )(proj, buf, conv_w, conv_b.reshape(1, SSD_CONV_DIM))


def _ssd_kernel(x_ref, b_ref, c_ref, z_ref, dtx_ref, dtt_ref, alx_ref, alt_ref, dsk_ref, nw_ref,
                s0_ref, y_ref, sout_ref, st_ref, *, q, nchunk, ng):
    t = pl.program_id(2)
    p, r, gw, n = SSD_HEADDIM, SSD_HEADS_PER_GROUP, SSD_GROUP_WIDTH, SSD_STATE

    @pl.when(t == 0)
    def _():
        st_ref[...] = s0_ref[0]

    row = lax.broadcasted_iota(jnp.int32, (q, q), 0)
    col = lax.broadcasted_iota(jnp.int32, (q, q), 1)
    lower = row >= col
    tril = lower.astype(BF16)
    triu = (row <= col).astype(BF16)
    a_x = -jnp.exp(alx_ref[...])
    a_t = -jnp.exp(alt_ref[...]).reshape(ng * r, 1)
    d_skip = dsk_ref[...]
    norm_w = nw_ref[...]
    nt_dims = (((1,), (1,)), ((), ()))
    tn_dims = (((0,), (0,)), ((), ()))

    def chunk(ci, carry):
        rows = pl.ds(pl.multiple_of(ci * q, q), q)
        x = x_ref[rows, :]
        dtx = dtx_ref[rows, :]
        cum = _cumsum_rows(tril, dtx * a_x)
        cum_t = _split3_dot(dtt_ref[:, ci].reshape(ng * r, q) * a_t, triu)
        cum_last = cum[q - 1:q, :]
        decay_in = jnp.exp(cum)
        xdt = (x * dtx).astype(BF16)
        xt = (x * (jnp.exp(cum_last - cum) * dtx)).astype(BF16)
        decay_st = jnp.exp(cum_last)
        gate = _silu(z_ref[rows, :])
        skip = d_skip * x
        for gi in range(ng):
            cols = slice(gi * gw, (gi + 1) * gw)
            bm = b_ref[rows, gi * n:(gi + 1) * n].astype(BF16)
            cm = c_ref[rows, gi * n:(gi + 1) * n].astype(BF16)
            cb = lax.dot_general(cm, bm, nt_dims, preferred_element_type=F32)
            st = st_ref[gi]
            y = jnp.dot(cm, st.astype(BF16), preferred_element_type=F32) * decay_in[:, cols]
            parts = []
            for h in range(r):
                c0 = gi * gw + h * p
                seg = cum[:, c0:c0 + q] - cum_t[gi * r + h:gi * r + h + 1, :]
                mix = (cb * jnp.where(lower, jnp.exp(seg), 0.0)).astype(BF16)
                parts.append(jnp.dot(mix, xdt[:, c0:c0 + p], preferred_element_type=F32))
            y = (y + jnp.concatenate(parts, axis=1) + skip[:, cols]) * gate[:, cols]
            ms = jnp.mean(y * y, axis=-1, keepdims=True)
            y_ref[rows, cols] = (y * lax.rsqrt(ms + NORM_EPS) * norm_w[:, cols]).astype(y_ref.dtype)
            st_ref[gi] = st * decay_st[:, cols] + lax.dot_general(
                bm, xt[:, cols], tn_dims, preferred_element_type=F32)
        return carry

    lax.fori_loop(0, nchunk, chunk, 0)

    @pl.when(t == pl.num_programs(2) - 1)
    def _():
        sout_ref[0] = st_ref[...]


def _ssd_scan(xconv, proj, dtx, dt, row_off, bt, seqlen, a_log, d_skip, ssd_norm_w, state0):
    g, r, p, n, gw = SSD_GROUPS, SSD_HEADS_PER_GROUP, SSD_HEADDIM, SSD_STATE, SSD_GROUP_WIDTH
    ng = SSD_GROUPS_PER_STEP
    q = min(CHUNK, seqlen)
    tb = _largest_divisor(seqlen, (512, 256, 128, 64, 32))
    nchunk = tb // q
    nt = seqlen // tb
    rows = bt * seqlen
    roff = row_off // tb
    dtt = dt[row_off:row_off + rows, :SSD_HEADS].reshape(rows // q, q, g, r).transpose(2, 0, 3, 1)
    alx = jnp.repeat(a_log, p).reshape(1, SSD_INNER)
    alt = a_log.reshape(g, r, 1)
    dsk = jnp.repeat(d_skip, p).reshape(1, SSD_INNER)
    s0 = state0.reshape(bt, g, r, p, n).transpose(0, 1, 4, 2, 3).reshape(bt, g, n, gw)
    bcol = SSD_INNER // (ng * n)
    ccol = bcol + g // ng
    wide = ng * gw
    y, sout = pl.pallas_call(
        functools.partial(_ssd_kernel, q=q, nchunk=nchunk, ng=ng),
        grid=(bt, g // ng, nt),
        in_specs=[pl.BlockSpec((tb, wide), lambda b, gi, t: (b * nt + t, gi)),
                  pl.BlockSpec((tb, ng * n), lambda b, gi, t: (b * nt + t, bcol + gi)),
                  pl.BlockSpec((tb, ng * n), lambda b, gi, t: (b * nt + t, ccol + gi)),
                  pl.BlockSpec((tb, wide), lambda b, gi, t: (roff + b * nt + t, COL_Z // wide + gi)),
                  pl.BlockSpec((tb, wide), lambda b, gi, t: (roff + b * nt + t, gi)),
                  pl.BlockSpec((ng, nchunk, r, q), lambda b, gi, t: (gi, b * nt + t, 0, 0)),
                  pl.BlockSpec((1, wide), lambda b, gi, t: (0, gi)),
                  pl.BlockSpec((ng, r, 1), lambda b, gi, t: (gi, 0, 0)),
                  pl.BlockSpec((1, wide), lambda b, gi, t: (0, gi)),
                  pl.BlockSpec((1, wide), lambda b, gi, t: (0, gi)),
                  pl.BlockSpec((1, ng, n, gw), lambda b, gi, t: (b, gi, 0, 0))],
        out_specs=[pl.BlockSpec((tb, wide), lambda b, gi, t: (b * nt + t, gi)),
                   pl.BlockSpec((1, ng, n, gw), lambda b, gi, t: (b, gi, 0, 0))],
        out_shape=[jax.ShapeDtypeStruct((rows, SSD_INNER), BF16),
                   jax.ShapeDtypeStruct((bt, g, n, gw), F32)],
        scratch_shapes=[pltpu.VMEM((ng, n, gw), F32)],
        compiler_params=_params("parallel", "parallel", "arbitrary"),
        name="ssd_scan",
    )(xconv, xconv, xconv, proj, dtx, dtt, alx, alt, dsk, ssd_norm_w.reshape(1, SSD_INNER), s0)
    state = sout.reshape(bt, g, n, r, p).transpose(0, 1, 3, 4, 2).reshape(bt, SSD_HEADS, p, n)
    return y, state


def _cumsum_rows(tril_bf16, x):
    hi = x.astype(BF16)
    rest = x - hi.astype(F32)
    mid = rest.astype(BF16)
    lo = (rest - mid.astype(F32)).astype(BF16)
    return (jnp.dot(tril_bf16, hi, preferred_element_type=F32)
            + jnp.dot(tril_bf16, mid, preferred_element_type=F32)
            + jnp.dot(tril_bf16, lo, preferred_element_type=F32))


def _hgrn_kernel(q_ref, f_ref, i_ref, g_ref, lbt_ref, nw_ref, s0_ref, o_ref, sout_ref, st_ref,
                 *, q, nchunk, nh):
    t = pl.program_id(2)

    @pl.when(t == 0)
    def _():
        st_ref[...] = s0_ref[0]

    row = lax.broadcasted_iota(jnp.int32, (q, q), 0)
    col = lax.broadcasted_iota(jnp.int32, (q, q), 1)
    lower = row >= col
    tril = lower.astype(BF16)
    tab = lbt_ref[...]
    e = jnp.exp(tab - jnp.max(tab, axis=0, keepdims=True))
    lb = e[0:1, :] / jnp.sum(e, axis=0, keepdims=True)
    norm_w = nw_ref[...]
    mid = q // 2
    nt_dims = (((1,), (1,)), ((), ()))
    tn_dims = (((0,), (0,)), ((), ()))

    def chunk(ci, carry):
        rows = pl.ds(pl.multiple_of(ci * q, q), q)
        qq = _silu(q_ref[rows, :]) * (HG_K ** -0.5)
        f = lb + (1.0 - lb) * _sigmoid(f_ref[rows, :])
        kk = 1.0 - f
        v = i_ref[rows, :].astype(BF16)
        gate = _silu(g_ref[rows, :])
        cum = _cumsum_rows(tril, jnp.log(f))
        ref = cum[mid - 1:mid, :]
        cum_last = cum[q - 1:q, :]
        qd = (qq * jnp.exp(cum - ref)).astype(BF16)
        kd = (kk * jnp.exp(ref - cum)).astype(BF16)
        qe = (qq * jnp.exp(cum)).astype(BF16)
        kt = (kk * jnp.exp(cum_last - cum)).astype(BF16)
        dec = jnp.exp(cum_last)
        for h in range(nh):
            sl = slice(h * HG_K, (h + 1) * HG_K)
            att = lax.dot_general(qd[:, sl], kd[:, sl], nt_dims, preferred_element_type=F32)
            att = jnp.where(lower, att, 0.0).astype(BF16)
            st = st_ref[h]
            o = jnp.dot(att, v[:, sl], preferred_element_type=F32)
            o = o + lax.dot_general(qe[:, sl], st.astype(BF16), nt_dims,
                                    preferred_element_type=F32)
            ms = jnp.mean(o * o, axis=-1, keepdims=True)
            o_ref[rows, sl] = (o * lax.rsqrt(ms + NORM_EPS) * norm_w * gate[:, sl]).astype(o_ref.dtype)
            st_ref[h] = st * dec[:, sl] + lax.dot_general(v[:, sl], kt[:, sl], tn_dims,
                                                          preferred_element_type=F32)
        return carry

    lax.fori_loop(0, nchunk, chunk, 0)

    @pl.when(t == pl.num_programs(2) - 1)
    def _():
        sout_ref[0] = st_ref[...]


def _hgrn_scan(proj, row_off, bt, seqlen, lb_table, hg_norm_w, state0):
    q = min(CHUNK, seqlen)
    tb = _largest_divisor(seqlen, (512, 256, 128, 64, 32))
    nchunk = tb // q
    nt = seqlen // tb
    rows = bt * seqlen
    roff = row_off // tb
    nh = HGRN_HEADS_PER_STEP
    hw = nh * HG_K
    s0 = state0.transpose(0, 1, 3, 2)
    nrow = lb_table.shape[0]

    def col_spec(col0):
        return pl.BlockSpec((tb, hw), lambda b, h, t: (roff + b * nt + t, col0 // hw + h))

    o, sout = pl.pallas_call(
        functools.partial(_hgrn_kernel, q=q, nchunk=nchunk, nh=nh),
        grid=(bt, HG_HEADS // nh, nt),
        in_specs=[col_spec(COL_HQ), col_spec(COL_HF), col_spec(COL_HI), col_spec(COL_HG),
                  pl.BlockSpec((nrow, hw), lambda b, h, t: (0, h)),
                  pl.BlockSpec((1, HG_V), lambda b, h, t: (0, 0)),
                  pl.BlockSpec((1, nh, HG_V, HG_K), lambda b, h, t: (b, h, 0, 0))],
        out_specs=[pl.BlockSpec((tb, hw), lambda b, h, t: (b * nt + t, h)),
                   pl.BlockSpec((1, nh, HG_V, HG_K), lambda b, h, t: (b, h, 0, 0))],
        out_shape=[jax.ShapeDtypeStruct((rows, HG_HEADS * HG_V), BF16),
                   jax.ShapeDtypeStruct((bt, HG_HEADS, HG_V, HG_K), F32)],
        scratch_shapes=[pltpu.VMEM((nh, HG_V, HG_K), F32)],
        compiler_params=_params("parallel", "parallel", "arbitrary"),
        name="hgrn_scan",
    )(proj, proj, proj, proj, lb_table, hg_norm_w.reshape(1, HG_V), s0)
    return o, sout.transpose(0, 1, 3, 2)


def _top_values(s, count):
    vals = []
    work = s
    for _ in range(count):
        m = jnp.max(work, axis=0, keepdims=True)
        vals.append(m)
        work = jnp.where(work == m, -jnp.inf, work)
    return jnp.concatenate(vals, axis=0)


def _peer_query_kernel(w_ref, x_ref, sk_ref, thr_ref, e0_ref, s1_ref, e1_ref):
    half = PEER_QDIM // 2
    qt = jnp.dot(w_ref[...], x_ref[...], preferred_element_type=F32)
    s0 = jnp.dot(sk_ref[0], qt[:half], precision=HIGHEST, preferred_element_type=F32)
    s1 = jnp.dot(sk_ref[1], qt[half:], precision=HIGHEST, preferred_element_type=F32)
    sv0 = _top_values(s0, PEER_TOPK)
    sv1 = _top_values(s1, PEER_TOPK)
    cand = [sv0[0:1] + sv1]
    for a in range(1, 8):
        cand.append(sv0[a:a + 1] + sv1[0:8])
    cand.append(sv0[8:16] + sv1[0:1])
    top = _top_values(jnp.concatenate(cand, axis=0), PEER_TOPK)
    z = jnp.sum(jnp.exp(top - top[0:1]), axis=0, keepdims=True)
    tau = top[PEER_TOPK - 1:PEER_TOPK]
    thr = jnp.full(s0.shape, jnp.inf, F32)
    for b in range(PEER_TOPK):
        thr = jnp.where(s0 + sv1[b:b + 1] >= tau, sv1[b:b + 1], thr)
    thr_ref[0] = thr
    s1_ref[0] = s1
    e0_ref[0] = jnp.exp(s0 - sv0[0:1]) / z
    e1_ref[0] = jnp.exp(s1 - sv1[0:1])


def _peer_query(w_pq_t, xn_t, sub_keys):
    d, t = xn_t.shape
    tb = _largest_divisor(t, (256, 128))
    keyed = jax.ShapeDtypeStruct((PEER_HEADS, PEER_KEYS, t), F32)
    key_spec = pl.BlockSpec((1, PEER_KEYS, tb), lambda i, h: (h, 0, i))
    return pl.pallas_call(
        _peer_query_kernel,
        grid=(t // tb, PEER_HEADS),
        in_specs=[pl.BlockSpec((PEER_QDIM, d), lambda i, h: (h, 0)),
                  pl.BlockSpec((d, tb), lambda i, h: (0, i)),
                  pl.BlockSpec((2, PEER_KEYS, PEER_QDIM // 2), lambda i, h: (0, 0, 0))],
        out_specs=[key_spec, key_spec, key_spec, key_spec],
        out_shape=[keyed, keyed, keyed, keyed],
        compiler_params=_params("parallel", "arbitrary"),
        name="peer_query",
    )(w_pq_t, xn_t, sub_keys)


def _gelu_tanh(x):
    return 0.5 * x * (1.0 + jnp.tanh(0.7978845608028654 * (x + 0.044715 * (x * x * x))))


def _peer_expert_kernel(x_ref, u_ref, v_ref, thr_ref, e0_ref, s1_ref, e1_ref, o_ref, w_ref, *, nb, tb):
    e = pl.program_id(1)

    @pl.when(e == 0)
    def _():
        o_ref[...] = jnp.zeros_like(o_ref)

    act = _gelu_tanh(jnp.dot(u_ref[...], x_ref[...], preferred_element_type=F32))
    for i in range(nb):
        for tl in range(tb // LANES):
            lanes = pl.ds(tl * LANES, LANES)
            acc = jnp.zeros((PEER_KEYS, LANES), F32)
            for h in range(PEER_HEADS):
                thr = thr_ref[h, 0, i:i + 1, lanes]
                c0 = e0_ref[h, 0, i:i + 1, lanes]
                acc = acc + jnp.where(s1_ref[h, :, lanes] >= thr, e1_ref[h, :, lanes] * c0, 0.0)
            w_ref[i * PEER_KEYS:(i + 1) * PEER_KEYS, lanes] = acc
    g = (w_ref[...] * act).astype(BF16)
    o_ref[...] += lax.dot_general(g, v_ref[...], (((0,), (0,)), ((), ())),
                                  preferred_element_type=F32)


def _peer_experts(xn_t, u, v, thr, e0, s1, e1):
    d, t = xn_t.shape
    n_exp = u.shape[0]
    tb = _largest_divisor(t, (512, 256, 128))
    nb = 4
    eb = nb * PEER_KEYS
    once = pl.Buffered(1)
    keyed = pl.BlockSpec((PEER_HEADS, PEER_KEYS, tb), lambda i, e: (0, 0, i), pipeline_mode=once)
    rows = pl.BlockSpec((PEER_HEADS, 1, nb, tb), lambda i, e: (0, e, 0, i))
    thr = thr.reshape(PEER_HEADS, PEER_KEYS // nb, nb, t)
    e0 = e0.reshape(PEER_HEADS, PEER_KEYS // nb, nb, t)
    return pl.pallas_call(
        functools.partial(_peer_expert_kernel, nb=nb, tb=tb),
        grid=(t // tb, n_exp // eb),
        in_specs=[pl.BlockSpec((d, tb), lambda i, e: (0, i), pipeline_mode=once),
                  pl.BlockSpec((eb, d), lambda i, e: (e, 0)),
                  pl.BlockSpec((eb, d), lambda i, e: (e, 0)),
                  rows, rows, keyed, keyed],
        out_specs=pl.BlockSpec((tb, d), lambda i, e: (i, 0)),
        out_shape=jax.ShapeDtypeStruct((t, d), F32),
        scratch_shapes=[pltpu.VMEM((eb, tb), F32)],
        compiler_params=_params("parallel", "arbitrary"),
        name="peer_experts",
    )(xn_t, u, v, thr, e0, s1, e1)


def _branches(proj, dtx, dt, row_off, bt, seqlen, conv_buf, s_ssd, s_hg, conv_w, conv_b, a_log, d_skip,
              ssd_norm_w, lb_table, hg_norm_w):
    xconv, new_conv = _conv_silu(proj, row_off, bt, seqlen, conv_buf, conv_w, conv_b)
    y_ssd, s_ssd_new = _ssd_scan(xconv, proj, dtx, dt, row_off, bt, seqlen,
                                 a_log, d_skip, ssd_norm_w, s_ssd)
    o_hg, s_hg_new = _hgrn_scan(proj, row_off, bt, seqlen, lb_table, hg_norm_w, s_hg)
    return y_ssd, o_hg, s_ssd_new, new_conv, s_hg_new


def kernel(x_prompt, x_sample, state_ssd, cache_ssd_conv, state_hgrn, lb_table, norm1_w, w_in, conv_w, conv_b, dt_bias, a_log, d_skip, ssd_norm_w, w_ssd_out, hg_norm_w, w_hg_out, w_o, norm2_w, w_pq, sub_keys, u_experts, v_experts, final_norm_w):
    assert w_in.shape[0] == 1 and lb_table.shape[0] == 2, "single-layer stack"
    bp, lp, d = x_prompt.shape
    bs, ls, _ = x_sample.shape
    tp, ts = bp * lp, bs * ls
    x_parts = (x_prompt.reshape(tp, d), x_sample.reshape(ts, d))

    w_in_t = w_in[0].T
    w_dt = jnp.pad(w_in_t[DT_COL:DT_COL + SSD_HEADS], ((0, LANES - SSD_HEADS), (0, 0))).astype(BF16)
    b_dt = jnp.pad(dt_bias[0], (0, LANES - SSD_HEADS)).reshape(1, LANES)

    xn = _rmsnorm_stacked(*x_parts, norm1_w[0], BF16)
    proj = _in_proj(xn, w_in_t)
    dt, dtx = _dt_proj(xn, w_dt, b_dt)

    zeros = functools.partial(jnp.zeros, dtype=F32)
    common = (conv_w[0], conv_b[0], a_log[0], d_skip[0], ssd_norm_w[0], lb_table, hg_norm_w[0])
    ya_p, yb_p, ssd_p, conv_p, hg_p = _branches(
        proj, dtx, dt, 0, bp, lp,
        zeros((bp, SSD_CONV - 1, SSD_CONV_DIM)), zeros((bp, SSD_HEADS, SSD_HEADDIM, SSD_STATE)),
        zeros((bp, HG_HEADS, HG_K, HG_V)), *common)
    ya_s, yb_s, ssd_s, conv_s, hg_s = _branches(
        proj, dtx, dt, tp, bs, ls, cache_ssd_conv[0], state_ssd[0], state_hgrn[0], *common)

    merged = _gated_merge((ya_p, ya_s), (yb_p, yb_s), w_ssd_out[0].astype(BF16),
                          w_hg_out[0].astype(BF16), proj)
    h = _residual_matmul(merged, w_o[0].astype(BF16), x_parts)

    hn_t = _rmsnorm_t(h, norm2_w[0], BF16)
    thr, e0, s1, e1 = _peer_query(w_pq[0].T.astype(BF16), hn_t, sub_keys[0])
    peer = _peer_experts(hn_t, u_experts[0].astype(BF16), v_experts[0].astype(BF16),
                         thr, e0, s1, e1)
    y_prompt = _add_rmsnorm(h, peer, final_norm_w, 0, tp).reshape(bp, lp, d)
    y_sample = _add_rmsnorm(h, peer, final_norm_w, tp, ts).reshape(bs, ls, d)
    return (y_prompt, y_sample, ssd_p[None], conv_p[None], hg_p[None],
            ssd_s[None], conv_s[None], hg_s[None])
```

```python
import functools

import jax
import jax.numpy as jnp
from jax import lax
from jax.experimental import pallas as pl
from jax.experimental.pallas import tpu as pltpu

F32 = jnp.float32
BF16 = jnp.bfloat16
HIGHEST = lax.Precision.HIGHEST

NORM_EPS = 1e-6
LANES = 128
VMEM_LIMIT_BYTES = 56 * 1024 * 1024
IN_PROJ_VMEM_LIMIT_BYTES = 60 * 1024 * 1024

D_MODEL = 4096
SSD_GROUPS = 8
SSD_HEADS_PER_GROUP = 8
SSD_HEADDIM = 64
SSD_STATE = 128
SSD_HEADS = SSD_GROUPS * SSD_HEADS_PER_GROUP
SSD_INNER = SSD_HEADS * SSD_HEADDIM
SSD_GROUP_WIDTH = SSD_HEADS_PER_GROUP * SSD_HEADDIM
SSD_CONV = 4
SSD_CONV_DIM = SSD_INNER + 2 * SSD_GROUPS * SSD_STATE
HG_HEADS = 32
HG_K = 128
HG_V = 128
CHUNK = 64
PEER_HEADS = 8
PEER_KEYS = 128
PEER_TOPK = 16
PEER_QDIM = 256
PEER_QUERY_HEADS_PER_STEP = 4
SSD_GROUPS_PER_STEP = 2
HGRN_HEADS_PER_STEP = 8

COL_Z = 0
COL_XBC = SSD_INNER
COL_HQ = COL_XBC + SSD_CONV_DIM
COL_HF = COL_HQ + HG_HEADS * HG_K
COL_HI = COL_HF + HG_HEADS * HG_K
COL_HG = COL_HI + HG_HEADS * HG_V
COL_GA = COL_HG + HG_HEADS * HG_V
COL_GB = COL_GA + D_MODEL
PROJ_WIDTH = COL_GB + D_MODEL
DT_COL = SSD_INNER + SSD_CONV_DIM


def _largest_divisor(n, candidates):
    for c in candidates:
        if n % c == 0:
            return c
    raise ValueError(f"no block size among {candidates} divides {n}")


def _params(*semantics):
    return pltpu.CompilerParams(dimension_semantics=semantics,
                                vmem_limit_bytes=VMEM_LIMIT_BYTES)


def _sigmoid(x):
    return 1.0 / (1.0 + jnp.exp(-x))


def _silu(x):
    return x * _sigmoid(x)


def _stacked_specs(block, n_first, row_pos, col_of=None):
    def first(*idx):
        return (jnp.minimum(idx[row_pos], n_first - 1), 0 if col_of is None else col_of(*idx))

    def second(*idx):
        return (jnp.maximum(idx[row_pos] - n_first, 0), 0 if col_of is None else col_of(*idx))

    return pl.BlockSpec(block, first), pl.BlockSpec(block, second)


def _stacked_block(i, n_first, first_ref, second_ref):
    return jnp.where(i < n_first, first_ref[...], second_ref[...])


def _row_block(n_first_rows, n_second_rows, candidates):
    for c in candidates:
        if n_first_rows % c == 0 and n_second_rows % c == 0:
            return c
    raise ValueError(f"no block size among {candidates} divides {n_first_rows} and {n_second_rows}")


def _rmsnorm_stacked_kernel(xa_ref, xb_ref, w_ref, o_ref, *, n_first):
    x = _stacked_block(pl.program_id(0), n_first, xa_ref, xb_ref)
    ms = jnp.mean(x * x, axis=-1, keepdims=True)
    o_ref[...] = (x * lax.rsqrt(ms + NORM_EPS) * w_ref[...]).astype(o_ref.dtype)


def _rmsnorm_stacked(xa, xb, w, out_dtype):
    (ta, d), tb_rows = xa.shape, xb.shape[0]
    bm = _row_block(ta, tb_rows, (256, 128, 64, 32, 16, 8))
    n_first = ta // bm
    spec_a, spec_b = _stacked_specs((bm, d), n_first, 0)
    return pl.pallas_call(
        functools.partial(_rmsnorm_stacked_kernel, n_first=n_first),
        grid=((ta + tb_rows) // bm,),
        in_specs=[spec_a, spec_b, pl.BlockSpec((1, d), lambda i: (0, 0))],
        out_specs=pl.BlockSpec((bm, d), lambda i: (i, 0)),
        out_shape=jax.ShapeDtypeStruct((ta + tb_rows, d), out_dtype),
        compiler_params=_params("parallel"),
        name="rmsnorm_in",
    )(xa, xb, w.reshape(1, d))


def _rmsnorm_t_kernel(x_ref, w_ref, o_ref):
    x = x_ref[...]
    ms = jnp.mean(x * x, axis=-1, keepdims=True)
    o_ref[...] = jnp.transpose(x * lax.rsqrt(ms + NORM_EPS) * w_ref[...]).astype(o_ref.dtype)


def _rmsnorm_t(x, w, out_dtype):
    t, d = x.shape
    bm = _largest_divisor(t, (256, 128))
    return pl.pallas_call(
        _rmsnorm_t_kernel,
        grid=(t // bm,),
        in_specs=[pl.BlockSpec((bm, d), lambda i: (i, 0)),
                  pl.BlockSpec((1, d), lambda i: (0, 0))],
        out_specs=pl.BlockSpec((d, bm), lambda i: (0, i)),
        out_shape=jax.ShapeDtypeStruct((d, t), out_dtype),
        compiler_params=_params("parallel"),
        name="rmsnorm_t",
    )(x, w.reshape(1, d))


def _add_rmsnorm_kernel(a_ref, b_ref, w_ref, o_ref):
    x = a_ref[...] + b_ref[...]
    ms = jnp.mean(x * x, axis=-1, keepdims=True)
    o_ref[...] = x * lax.rsqrt(ms + NORM_EPS) * w_ref[...]


def _add_rmsnorm(a, b, w, row_off, rows):
    d = a.shape[1]
    bm = _row_block(rows, row_off, (256, 128, 64, 32, 16, 8))
    off = row_off // bm
    return pl.pallas_call(
        _add_rmsnorm_kernel,
        grid=(rows // bm,),
        in_specs=[pl.BlockSpec((bm, d), lambda i: (off + i, 0)),
                  pl.BlockSpec((bm, d), lambda i: (off + i, 0)),
                  pl.BlockSpec((1, d), lambda i: (0, 0))],
        out_specs=pl.BlockSpec((bm, d), lambda i: (i, 0)),
        out_shape=jax.ShapeDtypeStruct((rows, d), F32),
        compiler_params=_params("parallel"),
        name="add_rmsnorm",
    )(a, b, w.reshape(1, d))


def _in_proj_kernel(x_ref, wa_ref, wb_ref, o_ref, wbuf_ref, *, n_plain, shift, chunk):
    j = pl.program_id(0)
    i = pl.program_id(1)
    bn = wa_ref.shape[0]

    @pl.when(jnp.logical_and(i == 0, j < n_plain))
    def _():
        def body(c, carry):
            rows = pl.ds(pl.multiple_of(c * chunk, chunk), chunk)
            wbuf_ref[rows, :] = wa_ref[rows, :].astype(BF16)
            return carry
        lax.fori_loop(0, bn // chunk, body, 0)

    @pl.when(jnp.logical_and(i == 0, j >= n_plain))
    def _():
        def body(c, carry):
            dst = pl.ds(pl.multiple_of(c * chunk, chunk), chunk)
            src = pl.ds(pl.multiple_of(c * chunk + shift, shift), chunk)
            wbuf_ref[dst, :] = wa_ref[src, :].astype(BF16)
            return carry
        lax.fori_loop(0, (bn - shift) // chunk, body, 0)
        wbuf_ref[bn - shift:bn, :] = wb_ref[...].astype(BF16)

    o_ref[...] = lax.dot_general(x_ref[...], wbuf_ref[...], (((1,), (1,)), ((), ())),
                                 preferred_element_type=F32)


def _in_proj(x, w_t):
    m, k = x.shape
    bm = _largest_divisor(m, (512, 256, 128, 64, 32, 16, 8))
    bn = 1024
    shift = SSD_HEADS
    assert DT_COL % bn == 0 and PROJ_WIDTH % bn == 0 and bn % shift == 0
    return pl.pallas_call(
        functools.partial(_in_proj_kernel, n_plain=DT_COL // bn, shift=shift, chunk=shift),
        grid=(PROJ_WIDTH // bn, m // bm),
        in_specs=[pl.BlockSpec((bm, k), lambda j, i: (i, 0)),
                  pl.BlockSpec((bn, k), lambda j, i: (j, 0)),
                  pl.BlockSpec((shift, k), lambda j, i: ((bn // shift) * (j + 1), 0))],
        out_specs=pl.BlockSpec((bm, bn), lambda j, i: (i, j)),
        out_shape=jax.ShapeDtypeStruct((m, PROJ_WIDTH), F32),
        scratch_shapes=[pltpu.VMEM((bn, k), BF16)],
        compiler_params=pltpu.CompilerParams(dimension_semantics=("arbitrary", "arbitrary"),
                                             vmem_limit_bytes=IN_PROJ_VMEM_LIMIT_BYTES),
        name="in_proj",
    )(x, w_t, w_t)


def _split3_dot(x, rhs_bf16):
    hi = x.astype(BF16)
    rest = x - hi.astype(F32)
    mid = rest.astype(BF16)
    lo = (rest - mid.astype(F32)).astype(BF16)
    return (jnp.dot(hi, rhs_bf16, preferred_element_type=F32)
            + jnp.dot(mid, rhs_bf16, preferred_element_type=F32)
            + jnp.dot(lo, rhs_bf16, preferred_element_type=F32))


def _dt_kernel(x_ref, w_ref, b_ref, ex_ref, dt_ref, dtx_ref):
    a = lax.dot_general(x_ref[...], w_ref[...], (((1,), (1,)), ((), ())),
                        preferred_element_type=F32) + b_ref[...]
    dt = jnp.maximum(a, 0.0) + jnp.log1p(jnp.exp(-jnp.abs(a)))
    dt_ref[...] = dt
    dtx_ref[...] = _split3_dot(dt, ex_ref[...])


def _dt_proj(x, w_t, b):
    m, k = x.shape
    n = w_t.shape[0]
    bm = _largest_divisor(m, (512, 256, 128, 64, 32, 16, 8))
    expand = (jnp.arange(SSD_INNER)[None, :] // SSD_HEADDIM == jnp.arange(n)[:, None]).astype(BF16)
    return pl.pallas_call(
        _dt_kernel,
        grid=(m // bm,),
        in_specs=[pl.BlockSpec((bm, k), lambda i: (i, 0)),
                  pl.BlockSpec((n, k), lambda i: (0, 0)),
                  pl.BlockSpec((1, n), lambda i: (0, 0)),
                  pl.BlockSpec((n, SSD_INNER), lambda i: (0, 0))],
        out_specs=[pl.BlockSpec((bm, n), lambda i: (i, 0)),
                   pl.BlockSpec((bm, SSD_INNER), lambda i: (i, 0))],
        out_shape=[jax.ShapeDtypeStruct((m, n), F32),
                   jax.ShapeDtypeStruct((m, SSD_INNER), F32)],
        compiler_params=_params("parallel"),
        name="dt_proj",
    )(x, w_t, b, expand)


def _merge_kernel(ya1_ref, ya2_ref, yb1_ref, yb2_ref, wa_ref, wb_ref, ga_ref, gb_ref, o_ref, *, n_first):
    i = pl.program_id(1)
    ya = _stacked_block(i, n_first, ya1_ref, ya2_ref)
    yb = _stacked_block(i, n_first, yb1_ref, yb2_ref)
    pa = jnp.dot(ya, wa_ref[...], preferred_element_type=F32)
    pb = jnp.dot(yb, wb_ref[...], preferred_element_type=F32)
    o_ref[...] = (_sigmoid(ga_ref[...]) * pa + _sigmoid(gb_ref[...]) * pb).astype(o_ref.dtype)


def _gated_merge(ya_parts, yb_parts, wa, wb, proj):
    (m1, k), m2 = ya_parts[0].shape, ya_parts[1].shape[0]
    n = wa.shape[1]
    bm = _row_block(m1, m2, (512, 256, 128, 64, 32, 16, 8))
    bn = 512
    n_first = m1 // bm
    y_first, y_second = _stacked_specs((bm, k), n_first, 1)
    return pl.pallas_call(
        functools.partial(_merge_kernel, n_first=n_first),
        grid=(n // bn, (m1 + m2) // bm),
        in_specs=[y_first, y_second, y_first, y_second,
                  pl.BlockSpec((k, bn), lambda j, i: (0, j)),
                  pl.BlockSpec((k, bn), lambda j, i: (0, j)),
                  pl.BlockSpec((bm, bn), lambda j, i: (i, COL_GA // bn + j)),
                  pl.BlockSpec((bm, bn), lambda j, i: (i, COL_GB // bn + j))],
        out_specs=pl.BlockSpec((bm, bn), lambda j, i: (i, j)),
        out_shape=jax.ShapeDtypeStruct((m1 + m2, n), BF16),
        compiler_params=_params("parallel", "parallel"),
        name="gated_merge",
    )(ya_parts[0], ya_parts[1], yb_parts[0], yb_parts[1], wa, wb, proj, proj)


def _residual_matmul_kernel(x_ref, w_ref, r1_ref, r2_ref, o_ref, *, n_first):
    r = _stacked_block(pl.program_id(1), n_first, r1_ref, r2_ref)
    o_ref[...] = r + jnp.dot(x_ref[...], w_ref[...], preferred_element_type=F32)


def _residual_matmul(x, w, r_parts):
    m, k = x.shape
    n = w.shape[1]
    m1, m2 = r_parts[0].shape[0], r_parts[1].shape[0]
    bm = _row_block(m1, m2, (512, 256, 128, 64, 32, 16, 8))
    bn = 1024
    n_first = m1 // bm
    r_first, r_second = _stacked_specs((bm, bn), n_first, 1, col_of=lambda j, i: j)
    return pl.pallas_call(
        functools.partial(_residual_matmul_kernel, n_first=n_first),
        grid=(n // bn, m // bm),
        in_specs=[pl.BlockSpec((bm, k), lambda j, i: (i, 0)),
                  pl.BlockSpec((k, bn), lambda j, i: (0, j)),
                  r_first, r_second],
        out_specs=pl.BlockSpec((bm, bn), lambda j, i: (i, j)),
        out_shape=jax.ShapeDtypeStruct((m, n), F32),
        compiler_params=_params("parallel", "parallel"),
        name="residual_matmul",
    )(x, w, r_parts[0], r_parts[1])


CONV_PAD = 8


def _conv_kernel(u_ref, buf_ref, w_ref, b_ref, o_ref, cache_ref, pad_ref, *, tb):
    t = pl.program_id(2)
    lo = CONV_PAD - (SSD_CONV - 1)

    @pl.when(t == 0)
    def _():
        pad_ref[lo:CONV_PAD, :] = buf_ref[0]

    @pl.when(t > 0)
    def _():
        pad_ref[lo:CONV_PAD, :] = pad_ref[tb + lo:tb + CONV_PAD, :]

    pad_ref[CONV_PAD:CONV_PAD + tb, :] = u_ref[...]
    acc = b_ref[...]
    for j in range(SSD_CONV):
        acc = acc + pad_ref[lo + j:lo + j + tb, :] * w_ref[j:j + 1, :]
    o_ref[...] = _silu(acc)

    @pl.when(t == pl.num_programs(2) - 1)
    def _():
        cache_ref[0] = pad_ref[tb + lo:tb + CONV_PAD, :]


def _conv_silu(proj, row_off, bt, seqlen, buf, conv_w, conv_b):
    tb = _largest_divisor(seqlen, (512, 256, 128, 64, 32))
    cb = 2048
    nt = seqlen // tb
    roff = row_off // tb
    coff = COL_XBC // cb
    return pl.pallas_call(
        functools.partial(_conv_kernel, tb=tb),
        grid=(SSD_CONV_DIM // cb, bt, nt),
        in_specs=[pl.BlockSpec((tb, cb), lambda c, b, t: (roff + b * nt + t, coff + c)),
                  pl.BlockSpec((1, SSD_CONV - 1, cb), lambda c, b, t: (b, 0, c)),
                  pl.BlockSpec((SSD_CONV, cb), lambda c, b, t: (0, c)),
                  pl.BlockSpec((1, cb), lambda c, b, t: (0, c))],
        out_specs=[pl.BlockSpec((tb, cb), lambda c, b, t: (b * nt + t, c)),
                   pl.BlockSpec((1, SSD_CONV - 1, cb), lambda c, b, t: (b, 0, c))],
        out_shape=[jax.ShapeDtypeStruct((bt * seqlen, SSD_CONV_DIM), F32),
                   jax.ShapeDtypeStruct((bt, SSD_CONV - 1, SSD_CONV_DIM), F32)],
        scratch_shapes=[pltpu.VMEM((CONV_PAD + tb, cb), F32)],
        compiler_params=_params("parallel", "parallel", "arbitrary"),
        name="conv_silu",
    )(proj, buf, conv_w, conv_b.reshape(1, SSD_CONV_DIM))


def _ssd_kernel(x_ref, b_ref, c_ref, z_ref, dtx_ref, dtt_ref, alx_ref, alt_ref, dsk_ref, nw_ref,
                s0_ref, y_ref, sout_ref, st_ref, *, q, nchunk, ng):
    t = pl.program_id(2)
    p, r, gw, n = SSD_HEADDIM, SSD_HEADS_PER_GROUP, SSD_GROUP_WIDTH, SSD_STATE

    @pl.when(t == 0)
    def _():
        st_ref[...] = s0_ref[0]

    row = lax.broadcasted_iota(jnp.int32, (q, q), 0)
    col = lax.broadcasted_iota(jnp.int32, (q, q), 1)
    lower = row >= col
    tril = lower.astype(BF16)
    triu = (row <= col).astype(BF16)
    a_x = -jnp.exp(alx_ref[...])
    a_t = -jnp.exp(alt_ref[...]).reshape(ng * r, 1)
    d_skip = dsk_ref[...]
    norm_w = nw_ref[...]
    nt_dims = (((1,), (1,)), ((), ()))
    tn_dims = (((0,), (0,)), ((), ()))

    def chunk(ci, carry):
        rows = pl.ds(pl.multiple_of(ci * q, q), q)
        x = x_ref[rows, :]
        dtx = dtx_ref[rows, :]
        cum = _cumsum_rows(tril, dtx * a_x)
        cum_t = _split3_dot(dtt_ref[:, ci].reshape(ng * r, q) * a_t, triu)
        cum_last = cum[q - 1:q, :]
        decay_in = jnp.exp(cum)
        xdt = (x * dtx).astype(BF16)
        xt = (x * (jnp.exp(cum_last - cum) * dtx)).astype(BF16)
        decay_st = jnp.exp(cum_last)
        gate = _silu(z_ref[rows, :])
        skip = d_skip * x
        for gi in range(ng):
            cols = slice(gi * gw, (gi + 1) * gw)
            bm = b_ref[rows, gi * n:(gi + 1) * n].astype(BF16)
            cm = c_ref[rows, gi * n:(gi + 1) * n].astype(BF16)
            cb = lax.dot_general(cm, bm, nt_dims, preferred_element_type=F32)
            st = st_ref[gi]
            y = jnp.dot(cm, st.astype(BF16), preferred_element_type=F32) * decay_in[:, cols]
            parts = []
            for h in range(r):
                c0 = gi * gw + h * p
                seg = cum[:, c0:c0 + q] - cum_t[gi * r + h:gi * r + h + 1, :]
                mix = (cb * jnp.where(lower, jnp.exp(seg), 0.0)).astype(BF16)
                parts.append(jnp.dot(mix, xdt[:, c0:c0 + p], preferred_element_type=F32))
            y = (y + jnp.concatenate(parts, axis=1) + skip[:, cols]) * gate[:, cols]
            ms = jnp.mean(y * y, axis=-1, keepdims=True)
            y_ref[rows, cols] = (y * lax.rsqrt(ms + NORM_EPS) * norm_w[:, cols]).astype(y_ref.dtype)
            st_ref[gi] = st * decay_st[:, cols] + lax.dot_general(
                bm, xt[:, cols], tn_dims, preferred_element_type=F32)
        return carry

    lax.fori_loop(0, nchunk, chunk, 0)

    @pl.when(t == pl.num_programs(2) - 1)
    def _():
        sout_ref[0] = st_ref[...]


def _ssd_scan(xconv, proj, dtx, dt, row_off, bt, seqlen, a_log, d_skip, ssd_norm_w, state0):
    g, r, p, n, gw = SSD_GROUPS, SSD_HEADS_PER_GROUP, SSD_HEADDIM, SSD_STATE, SSD_GROUP_WIDTH
    ng = SSD_GROUPS_PER_STEP
    q = min(CHUNK, seqlen)
    tb = _largest_divisor(seqlen, (512, 256, 128, 64, 32))
    nchunk = tb // q
    nt = seqlen // tb
    rows = bt * seqlen
    roff = row_off // tb
    dtt = dt[row_off:row_off + rows, :SSD_HEADS].reshape(rows // q, q, g, r).transpose(2, 0, 3, 1)
    alx = jnp.repeat(a_log, p).reshape(1, SSD_INNER)
    alt = a_log.reshape(g, r, 1)
    dsk = jnp.repeat(d_skip, p).reshape(1, SSD_INNER)
    s0 = state0.reshape(bt, g, r, p, n).transpose(0, 1, 4, 2, 3).reshape(bt, g, n, gw)
    bcol = SSD_INNER // (ng * n)
    ccol = bcol + g // ng
    wide = ng * gw
    y, sout = pl.pallas_call(
        functools.partial(_ssd_kernel, q=q, nchunk=nchunk, ng=ng),
        grid=(bt, g // ng, nt),
        in_specs=[pl.BlockSpec((tb, wide), lambda b, gi, t: (b * nt + t, gi)),
                  pl.BlockSpec((tb, ng * n), lambda b, gi, t: (b * nt + t, bcol + gi)),
                  pl.BlockSpec((tb, ng * n), lambda b, gi, t: (b * nt + t, ccol + gi)),
                  pl.BlockSpec((tb, wide), lambda b, gi, t: (roff + b * nt + t, COL_Z // wide + gi)),
                  pl.BlockSpec((tb, wide), lambda b, gi, t: (roff + b * nt + t, gi)),
                  pl.BlockSpec((ng, nchunk, r, q), lambda b, gi, t: (gi, b * nt + t, 0, 0)),
                  pl.BlockSpec((1, wide), lambda b, gi, t: (0, gi)),
                  pl.BlockSpec((ng, r, 1), lambda b, gi, t: (gi, 0, 0)),
                  pl.BlockSpec((1, wide), lambda b, gi, t: (0, gi)),
                  pl.BlockSpec((1, wide), lambda b, gi, t: (0, gi)),
                  pl.BlockSpec((1, ng, n, gw), lambda b, gi, t: (b, gi, 0, 0))],
        out_specs=[pl.BlockSpec((tb, wide), lambda b, gi, t: (b * nt + t, gi)),
                   pl.BlockSpec((1, ng, n, gw), lambda b, gi, t: (b, gi, 0, 0))],
        out_shape=[jax.ShapeDtypeStruct((rows, SSD_INNER), BF16),
                   jax.ShapeDtypeStruct((bt, g, n, gw), F32)],
        scratch_shapes=[pltpu.VMEM((ng, n, gw), F32)],
        compiler_params=_params("parallel", "parallel", "arbitrary"),
        name="ssd_scan",
    )(xconv, xconv, xconv, proj, dtx, dtt, alx, alt, dsk, ssd_norm_w.reshape(1, SSD_INNER), s0)
    state = sout.reshape(bt, g, n, r, p).transpose(0, 1, 3, 4, 2).reshape(bt, SSD_HEADS, p, n)
    return y, state


def _cumsum_rows(tril_bf16, x):
    hi = x.astype(BF16)
    rest = x - hi.astype(F32)
    mid = rest.astype(BF16)
    lo = (rest - mid.astype(F32)).astype(BF16)
    return (jnp.dot(tril_bf16, hi, preferred_element_type=F32)
            + jnp.dot(tril_bf16, mid, preferred_element_type=F32)
            + jnp.dot(tril_bf16, lo, preferred_element_type=F32))


def _hgrn_kernel(q_ref, f_ref, i_ref, g_ref, lbt_ref, nw_ref, s0_ref, o_ref, sout_ref, st_ref,
                 *, q, nchunk, nh):
    t = pl.program_id(2)

    @pl.when(t == 0)
    def _():
        st_ref[...] = s0_ref[0]

    hq = q // 2
    row = lax.broadcasted_iota(jnp.int32, (q, q), 0)
    col = lax.broadcasted_iota(jnp.int32, (q, q), 1)
    lower = row >= col
    tril = lower.astype(BF16)
    left_half = lax.broadcasted_iota(jnp.int32, (hq, LANES), 1) < hq
    k_pad = jnp.zeros((LANES - 3 * hq, nh * HG_K), BF16)
    tab = lbt_ref[...]
    e = jnp.exp(tab - jnp.max(tab, axis=0, keepdims=True))
    lb = e[0:1, :] / jnp.sum(e, axis=0, keepdims=True)
    norm_w = nw_ref[...]
    nt_dims = (((1,), (1,)), ((), ()))
    tn_dims = (((0,), (0,)), ((), ()))

    def decayed(x, cum_rows, ref_row, sign):
        return (x * jnp.exp(sign * (cum_rows - ref_row))).astype(BF16)

    def chunk(ci, carry):
        rows = pl.ds(pl.multiple_of(ci * q, q), q)
        qq = _silu(q_ref[rows, :]) * (HG_K ** -0.5)
        f = lb + (1.0 - lb) * _sigmoid(f_ref[rows, :])
        kk = 1.0 - f
        v = i_ref[rows, :].astype(BF16)
        gate = _silu(g_ref[rows, :])
        cum = _cumsum_rows(tril, jnp.log(f))
        top, bot = slice(0, hq), slice(hq, q)
        mid_top = cum[hq // 2 - 1:hq // 2, :]
        mid_bot = cum[hq + hq // 2 - 1:hq + hq // 2, :]
        edge = cum[hq - 1:hq, :]
        cum_last = cum[q - 1:q, :]
        q_tt = decayed(qq[top], cum[top], mid_top, 1.0)
        k_tt = decayed(kk[top], cum[top], mid_top, -1.0)
        q_bb = decayed(qq[bot], cum[bot], mid_bot, 1.0)
        k_bb = decayed(kk[bot], cum[bot], mid_bot, -1.0)
        q_bt = decayed(qq[bot], cum[bot], edge, 1.0)
        k_bt = decayed(kk[top], cum[top], edge, -1.0)
        qe = (qq * jnp.exp(cum)).astype(BF16)
        kt = (kk * jnp.exp(cum_last - cum)).astype(BF16)
        dec = jnp.exp(cum_last)
        q_stack = jnp.concatenate([q_tt, q_bb, q_bt], axis=0)
        k_stack = jnp.concatenate([k_tt, k_bb, k_bt, k_pad], axis=0)
        for h in range(nh):
            sl = slice(h * HG_K, (h + 1) * HG_K)
            s_all = lax.dot_general(q_stack[:, sl], k_stack[:, sl], nt_dims,
                                    preferred_element_type=F32)
            bottom_left = pltpu.roll(s_all[2 * hq:3 * hq, :], LANES - 2 * hq, axis=1)
            s_bot = jnp.where(left_half, bottom_left, s_all[hq:2 * hq, :])
            att = jnp.concatenate([s_all[0:hq, 0:q], s_bot[:, 0:q]], axis=0)
            att = jnp.where(lower, att, 0.0).astype(BF16)
            st = st_ref[h]
            o = jnp.dot(att, v[:, sl], preferred_element_type=F32) + lax.dot_general(
                qe[:, sl], st.astype(BF16), nt_dims, preferred_element_type=F32)
            ms = jnp.mean(o * o, axis=-1, keepdims=True)
            o_ref[rows, sl] = (o * lax.rsqrt(ms + NORM_EPS) * norm_w * gate[:, sl]).astype(o_ref.dtype)
            st_ref[h] = st * dec[:, sl] + lax.dot_general(v[:, sl], kt[:, sl], tn_dims,
                                                          preferred_element_type=F32)
        return carry

    lax.fori_loop(0, nchunk, chunk, 0)

    @pl.when(t == pl.num_programs(2) - 1)
    def _():
        sout_ref[0] = st_ref[...]


def _hgrn_scan(proj, row_off, bt, seqlen, lb_table, hg_norm_w, state0):
    q = min(CHUNK, seqlen)
    tb = _largest_divisor(seqlen, (512, 256, 128, 64, 32))
    nchunk = tb // q
    nt = seqlen // tb
    rows = bt * seqlen
    roff = row_off // tb
    nh = HGRN_HEADS_PER_STEP
    hw = nh * HG_K
    s0 = state0.transpose(0, 1, 3, 2)
    nrow = lb_table.shape[0]

    def col_spec(col0):
        return pl.BlockSpec((tb, hw), lambda b, h, t: (roff + b * nt + t, col0 // hw + h))

    o, sout = pl.pallas_call(
        functools.partial(_hgrn_kernel, q=q, nchunk=nchunk, nh=nh),
        grid=(bt, HG_HEADS // nh, nt),
        in_specs=[col_spec(COL_HQ), col_spec(COL_HF), col_spec(COL_HI), col_spec(COL_HG),
                  pl.BlockSpec((nrow, hw), lambda b, h, t: (0, h)),
                  pl.BlockSpec((1, HG_V), lambda b, h, t: (0, 0)),
                  pl.BlockSpec((1, nh, HG_V, HG_K), lambda b, h, t: (b, h, 0, 0))],
        out_specs=[pl.BlockSpec((tb, hw), lambda b, h, t: (b * nt + t, h)),
                   pl.BlockSpec((1, nh, HG_V, HG_K), lambda b, h, t: (b, h, 0, 0))],
        out_shape=[jax.ShapeDtypeStruct((rows, HG_HEADS * HG_V), BF16),
                   jax.ShapeDtypeStruct((bt, HG_HEADS, HG_V, HG_K), F32)],
        scratch_shapes=[pltpu.VMEM((nh, HG_V, HG_K), F32)],
        compiler_params=_params("parallel", "parallel", "arbitrary"),
        name="hgrn_scan",
    )(proj, proj, proj, proj, lb_table, hg_norm_w.reshape(1, HG_V), s0)
    return o, sout.transpose(0, 1, 3, 2)


def _top_values(s, count):
    vals = []
    work = s
    for _ in range(count):
        m = jnp.max(work, axis=0, keepdims=True)
        vals.append(m)
        work = jnp.where(work == m, -jnp.inf, work)
    return jnp.concatenate(vals, axis=0)


def _peer_query_kernel(w_ref, x_ref, sk_ref, thr_ref, e0_ref, s1_ref, e1_ref):
    half = PEER_QDIM // 2
    nh = thr_ref.shape[0]
    qt = jnp.dot(w_ref[...], x_ref[...], preferred_element_type=F32)
    for h in range(nh):
        q0 = qt[h * PEER_QDIM:h * PEER_QDIM + half]
        q1 = qt[h * PEER_QDIM + half:(h + 1) * PEER_QDIM]
        s0 = jnp.dot(sk_ref[0], q0, precision=HIGHEST, preferred_element_type=F32)
        s1 = jnp.dot(sk_ref[1], q1, precision=HIGHEST, preferred_element_type=F32)
        sv0 = _top_values(s0, PEER_TOPK)
        sv1 = _top_values(s1, PEER_TOPK)
        cand = [sv0[0:1] + sv1]
        for a in range(1, 8):
            cand.append(sv0[a:a + 1] + sv1[0:8])
        cand.append(sv0[8:16] + sv1[0:1])
        top = _top_values(jnp.concatenate(cand, axis=0), PEER_TOPK)
        z = jnp.sum(jnp.exp(top - top[0:1]), axis=0, keepdims=True)
        tau = top[PEER_TOPK - 1:PEER_TOPK]
        thr = jnp.full(s0.shape, jnp.inf, F32)
        for b in range(PEER_TOPK):
            thr = jnp.where(s0 + sv1[b:b + 1] >= tau, sv1[b:b + 1], thr)
        thr_ref[h] = thr
        s1_ref[h] = s1
        e0_ref[h] = jnp.exp(s0 - sv0[0:1]) / z
        e1_ref[h] = jnp.exp(s1 - sv1[0:1])


def _peer_query(w_pq_t, xn_t, sub_keys):
    d, t = xn_t.shape
    tb = _largest_divisor(t, (256, 128))
    nh = PEER_QUERY_HEADS_PER_STEP
    keyed = jax.ShapeDtypeStruct((PEER_HEADS, PEER_KEYS, t), F32)
    key_spec = pl.BlockSpec((nh, PEER_KEYS, tb), lambda i, h: (h, 0, i))
    return pl.pallas_call(
        _peer_query_kernel,
        grid=(t // tb, PEER_HEADS // nh),
        in_specs=[pl.BlockSpec((nh * PEER_QDIM, d), lambda i, h: (h, 0)),
                  pl.BlockSpec((d, tb), lambda i, h: (0, i)),
                  pl.BlockSpec((2, PEER_KEYS, PEER_QDIM // 2), lambda i, h: (0, 0, 0))],
        out_specs=[key_spec, key_spec, key_spec, key_spec],
        out_shape=[keyed, keyed, keyed, keyed],
        compiler_params=_params("parallel", "arbitrary"),
        name="peer_query",
    )(w_pq_t, xn_t, sub_keys)


def _gelu_tanh(x):
    return 0.5 * x * (1.0 + jnp.tanh(0.7978845608028654 * (x + 0.044715 * (x * x * x))))


def _peer_expert_kernel(x_ref, u_ref, v_ref, thr_ref, e0_ref, s1_ref, e1_ref, o_ref, w_ref, *, nb, tb):
    e = pl.program_id(1)

    @pl.when(e == 0)
    def _():
        o_ref[...] = jnp.zeros_like(o_ref)

    act = _gelu_tanh(jnp.dot(u_ref[...], x_ref[...], preferred_element_type=F32))
    for i in range(nb):
        for tl in range(tb // LANES):
            lanes = pl.ds(tl * LANES, LANES)
            acc = jnp.zeros((PEER_KEYS, LANES), F32)
            for h in range(PEER_HEADS):
                thr = thr_ref[h, 0, i:i + 1, lanes]
                c0 = e0_ref[h, 0, i:i + 1, lanes]
                acc = acc + jnp.where(s1_ref[h, :, lanes] >= thr, e1_ref[h, :, lanes] * c0, 0.0)
            w_ref[i * PEER_KEYS:(i + 1) * PEER_KEYS, lanes] = acc
    g = (w_ref[...] * act).astype(BF16)
    o_ref[...] += lax.dot_general(g, v_ref[...], (((0,), (0,)), ((), ())),
                                  preferred_element_type=F32)


def _peer_experts(xn_t, u, v, thr, e0, s1, e1):
    d, t = xn_t.shape
    n_exp = u.shape[0]
    tb = _largest_divisor(t, (512, 256, 128))
    nb = 4
    eb = nb * PEER_KEYS
    once = pl.Buffered(1)
    keyed = pl.BlockSpec((PEER_HEADS, PEER_KEYS, tb), lambda i, e: (0, 0, i), pipeline_mode=once)
    rows = pl.BlockSpec((PEER_HEADS, 1, nb, tb), lambda i, e: (0, e, 0, i))
    thr = thr.reshape(PEER_HEADS, PEER_KEYS // nb, nb, t)
    e0 = e0.reshape(PEER_HEADS, PEER_KEYS // nb, nb, t)
    return pl.pallas_call(
        functools.partial(_peer_expert_kernel, nb=nb, tb=tb),
        grid=(t // tb, n_exp // eb),
        in_specs=[pl.BlockSpec((d, tb), lambda i, e: (0, i), pipeline_mode=once),
                  pl.BlockSpec((eb, d), lambda i, e: (e, 0)),
                  pl.BlockSpec((eb, d), lambda i, e: (e, 0)),
                  rows, rows, keyed, keyed],
        out_specs=pl.BlockSpec((tb, d), lambda i, e: (i, 0)),
        out_shape=jax.ShapeDtypeStruct((t, d), F32),
        scratch_shapes=[pltpu.VMEM((eb, tb), F32)],
        compiler_params=_params("parallel", "arbitrary"),
        name="peer_experts",
    )(xn_t, u, v, thr, e0, s1, e1)


def _branches(proj, dtx, dt, row_off, bt, seqlen, conv_buf, s_ssd, s_hg, conv_w, conv_b, a_log, d_skip,
              ssd_norm_w, lb_table, hg_norm_w):
    xconv, new_conv = _conv_silu(proj, row_off, bt, seqlen, conv_buf, conv_w, conv_b)
    y_ssd, s_ssd_new = _ssd_scan(xconv, proj, dtx, dt, row_off, bt, seqlen,
                                 a_log, d_skip, ssd_norm_w, s_ssd)
    o_hg, s_hg_new = _hgrn_scan(proj, row_off, bt, seqlen, lb_table, hg_norm_w, s_hg)
    return y_ssd, o_hg, s_ssd_new, new_conv, s_hg_new


def kernel(x_prompt, x_sample, state_ssd, cache_ssd_conv, state_hgrn, lb_table, norm1_w, w_in, conv_w, conv_b, dt_bias, a_log, d_skip, ssd_norm_w, w_ssd_out, hg_norm_w, w_hg_out, w_o, norm2_w, w_pq, sub_keys, u_experts, v_experts, final_norm_w):
    assert w_in.shape[0] == 1 and lb_table.shape[0] == 2, "single-layer stack"
    bp, lp, d = x_prompt.shape
    bs, ls, _ = x_sample.shape
    tp, ts = bp * lp, bs * ls
    x_parts = (x_prompt.reshape(tp, d), x_sample.reshape(ts, d))

    w_in_t = w_in[0].T
    w_dt = jnp.pad(w_in_t[DT_COL:DT_COL + SSD_HEADS], ((0, LANES - SSD_HEADS), (0, 0))).astype(BF16)
    b_dt = jnp.pad(dt_bias[0], (0, LANES - SSD_HEADS)).reshape(1, LANES)

    xn = _rmsnorm_stacked(*x_parts, norm1_w[0], BF16)
    proj = _in_proj(xn, w_in_t)
    dt, dtx = _dt_proj(xn, w_dt, b_dt)

    zeros = functools.partial(jnp.zeros, dtype=F32)
    common = (conv_w[0], conv_b[0], a_log[0], d_skip[0], ssd_norm_w[0], lb_table, hg_norm_w[0])
    ya_p, yb_p, ssd_p, conv_p, hg_p = _branches(
        proj, dtx, dt, 0, bp, lp,
        zeros((bp, SSD_CONV - 1, SSD_CONV_DIM)), zeros((bp, SSD_HEADS, SSD_HEADDIM, SSD_STATE)),
        zeros((bp, HG_HEADS, HG_K, HG_V)), *common)
    ya_s, yb_s, ssd_s, conv_s, hg_s = _branches(
        proj, dtx, dt, tp, bs, ls, cache_ssd_conv[0], state_ssd[0], state_hgrn[0], *common)

    merged = _gated_merge((ya_p, ya_s), (yb_p, yb_s), w_ssd_out[0].astype(BF16),
                          w_hg_out[0].astype(BF16), proj)
    h = _residual_matmul(merged, w_o[0].astype(BF16), x_parts)

    hn_t = _rmsnorm_t(h, norm2_w[0], BF16)
    thr, e0, s1, e1 = _peer_query(w_pq[0].T.astype(BF16), hn_t, sub_keys[0])
    peer = _peer_experts(hn_t, u_experts[0].astype(BF16), v_experts[0].astype(BF16),
                         thr, e0, s1, e1)
    y_prompt = _add_rmsnorm(h, peer, final_norm_w, 0, tp).reshape(bp, lp, d)
    y_sample = _add_rmsnorm(h, peer, final_norm_w, tp, ts).reshape(bs, ls, d)
    return (y_prompt, y_sample, ssd_p[None], conv_p[None], hg_p[None],
            ssd_s[None], conv_s[None], hg_s[None])
```

```python
import functools

import jax
import jax.numpy as jnp
from jax import lax
from jax.experimental import pallas as pl
from jax.experimental.pallas import tpu as pltpu

F32 = jnp.float32
BF16 = jnp.bfloat16
HIGHEST = lax.Precision.HIGHEST

NORM_EPS = 1e-6
LANES = 128
VMEM_LIMIT_BYTES = 56 * 1024 * 1024
IN_PROJ_VMEM_LIMIT_BYTES = 62 * 1024 * 1024

D_MODEL = 4096
SSD_GROUPS = 8
SSD_HEADS_PER_GROUP = 8
SSD_HEADDIM = 64
SSD_STATE = 128
SSD_HEADS = SSD_GROUPS * SSD_HEADS_PER_GROUP
SSD_INNER = SSD_HEADS * SSD_HEADDIM
SSD_GROUP_WIDTH = SSD_HEADS_PER_GROUP * SSD_HEADDIM
SSD_CONV = 4
SSD_CONV_DIM = SSD_INNER + 2 * SSD_GROUPS * SSD_STATE
HG_HEADS = 32
HG_K = 128
HG_V = 128
CHUNK = 64
PEER_HEADS = 8
PEER_KEYS = 128
PEER_TOPK = 16
PEER_QDIM = 256
PEER_QUERY_HEADS_PER_STEP = 4
SSD_GROUPS_PER_STEP = 2
HGRN_HEADS_PER_STEP = 8

COL_Z = 0
COL_XBC = SSD_INNER
COL_HQ = COL_XBC + SSD_CONV_DIM
COL_HF = COL_HQ + HG_HEADS * HG_K
COL_HI = COL_HF + HG_HEADS * HG_K
COL_HG = COL_HI + HG_HEADS * HG_V
COL_GA = COL_HG + HG_HEADS * HG_V
COL_GB = COL_GA + D_MODEL
PROJ_WIDTH = COL_GB + D_MODEL
DT_COL = SSD_INNER + SSD_CONV_DIM


def _largest_divisor(n, candidates):
    for c in candidates:
        if n % c == 0:
            return c
    raise ValueError(f"no block size among {candidates} divides {n}")


def _params(*semantics):
    return pltpu.CompilerParams(dimension_semantics=semantics,
                                vmem_limit_bytes=VMEM_LIMIT_BYTES)


def _sigmoid(x):
    return 1.0 / (1.0 + jnp.exp(-x))


def _silu(x):
    return x * _sigmoid(x)


def _stacked_specs(block, n_first, row_pos, col_of=None):
    def first(*idx):
        return (jnp.minimum(idx[row_pos], n_first - 1), 0 if col_of is None else col_of(*idx))

    def second(*idx):
        return (jnp.maximum(idx[row_pos] - n_first, 0), 0 if col_of is None else col_of(*idx))

    return pl.BlockSpec(block, first), pl.BlockSpec(block, second)


def _stacked_block(i, n_first, first_ref, second_ref):
    return jnp.where(i < n_first, first_ref[...], second_ref[...])


def _row_block(n_first_rows, n_second_rows, candidates):
    for c in candidates:
        if n_first_rows % c == 0 and n_second_rows % c == 0:
            return c
    raise ValueError(f"no block size among {candidates} divides {n_first_rows} and {n_second_rows}")


def _rmsnorm_stacked_kernel(xa_ref, xb_ref, w_ref, o_ref, *, n_first):
    x = _stacked_block(pl.program_id(0), n_first, xa_ref, xb_ref)
    ms = jnp.mean(x * x, axis=-1, keepdims=True)
    o_ref[...] = (x * lax.rsqrt(ms + NORM_EPS) * w_ref[...]).astype(o_ref.dtype)


def _rmsnorm_stacked(xa, xb, w, out_dtype):
    (ta, d), tb_rows = xa.shape, xb.shape[0]
    bm = _row_block(ta, tb_rows, (256, 128, 64, 32, 16, 8))
    n_first = ta // bm
    spec_a, spec_b = _stacked_specs((bm, d), n_first, 0)
    return pl.pallas_call(
        functools.partial(_rmsnorm_stacked_kernel, n_first=n_first),
        grid=((ta + tb_rows) // bm,),
        in_specs=[spec_a, spec_b, pl.BlockSpec((1, d), lambda i: (0, 0))],
        out_specs=pl.BlockSpec((bm, d), lambda i: (i, 0)),
        out_shape=jax.ShapeDtypeStruct((ta + tb_rows, d), out_dtype),
        compiler_params=_params("parallel"),
        name="rmsnorm_in",
    )(xa, xb, w.reshape(1, d))


def _rmsnorm_t_kernel(x_ref, w_ref, o_ref):
    x = x_ref[...]
    ms = jnp.mean(x * x, axis=-1, keepdims=True)
    o_ref[...] = jnp.transpose(x * lax.rsqrt(ms + NORM_EPS) * w_ref[...]).astype(o_ref.dtype)


def _rmsnorm_t(x, w, out_dtype):
    t, d = x.shape
    bm = _largest_divisor(t, (256, 128))
    return pl.pallas_call(
        _rmsnorm_t_kernel,
        grid=(t // bm,),
        in_specs=[pl.BlockSpec((bm, d), lambda i: (i, 0)),
                  pl.BlockSpec((1, d), lambda i: (0, 0))],
        out_specs=pl.BlockSpec((d, bm), lambda i: (0, i)),
        out_shape=jax.ShapeDtypeStruct((d, t), out_dtype),
        compiler_params=_params("parallel"),
        name="rmsnorm_t",
    )(x, w.reshape(1, d))


def _add_rmsnorm_kernel(a_ref, b_ref, w_ref, o_ref):
    x = a_ref[...] + b_ref[...]
    ms = jnp.mean(x * x, axis=-1, keepdims=True)
    o_ref[...] = x * lax.rsqrt(ms + NORM_EPS) * w_ref[...]


def _add_rmsnorm(a, b, w, row_off, rows):
    d = a.shape[1]
    bm = _row_block(rows, row_off, (256, 128, 64, 32, 16, 8))
    off = row_off // bm
    return pl.pallas_call(
        _add_rmsnorm_kernel,
        grid=(rows // bm,),
        in_specs=[pl.BlockSpec((bm, d), lambda i: (off + i, 0)),
                  pl.BlockSpec((bm, d), lambda i: (off + i, 0)),
                  pl.BlockSpec((1, d), lambda i: (0, 0))],
        out_specs=pl.BlockSpec((bm, d), lambda i: (i, 0)),
        out_shape=jax.ShapeDtypeStruct((rows, d), F32),
        compiler_params=_params("parallel"),
        name="add_rmsnorm",
    )(a, b, w.reshape(1, d))


def _in_proj_kernel(x_ref, wa_ref, wb_ref, ta_ref, tb_ref, o_ref, ta_out_ref, tb_out_ref, wbuf_ref,
                    *, n_plain, shift, chunk):
    ta_out_ref[...] = ta_ref[...].astype(ta_out_ref.dtype)
    tb_out_ref[...] = tb_ref[...].astype(tb_out_ref.dtype)
    j = pl.program_id(0)
    i = pl.program_id(1)
    bn = wa_ref.shape[0]

    @pl.when(jnp.logical_and(i == 0, j < n_plain))
    def _():
        def body(c, carry):
            rows = pl.ds(pl.multiple_of(c * chunk, chunk), chunk)
            wbuf_ref[rows, :] = wa_ref[rows, :].astype(BF16)
            return carry
        lax.fori_loop(0, bn // chunk, body, 0)

    @pl.when(jnp.logical_and(i == 0, j >= n_plain))
    def _():
        def body(c, carry):
            dst = pl.ds(pl.multiple_of(c * chunk, chunk), chunk)
            src = pl.ds(pl.multiple_of(c * chunk + shift, shift), chunk)
            wbuf_ref[dst, :] = wa_ref[src, :].astype(BF16)
            return carry
        lax.fori_loop(0, (bn - shift) // chunk, body, 0)
        wbuf_ref[bn - shift:bn, :] = wb_ref[...].astype(BF16)

    o_ref[...] = lax.dot_general(x_ref[...], wbuf_ref[...], (((1,), (1,)), ((), ())),
                                 preferred_element_type=F32)


def _in_proj(x, w_t, table_a, table_b):
    m, k = x.shape
    bm = _largest_divisor(m, (512, 256, 128, 64, 32, 16, 8))
    bn = 1024
    shift = SSD_HEADS
    assert DT_COL % bn == 0 and PROJ_WIDTH % bn == 0 and bn % shift == 0
    n_i = m // bm
    steps = (PROJ_WIDTH // bn) * n_i
    rows, width = table_a.shape
    assert table_b.shape == table_a.shape
    slab = next(r for r in range(16, rows + 1, 16) if rows % r == 0 and rows // r <= steps)
    last = rows // slab - 1
    slab_spec = pl.BlockSpec((slab, width), lambda j, i: (jnp.minimum(j * n_i + i, last), 0))
    table_bf16 = jax.ShapeDtypeStruct((rows, width), BF16)
    return pl.pallas_call(
        functools.partial(_in_proj_kernel, n_plain=DT_COL // bn, shift=shift, chunk=shift),
        grid=(PROJ_WIDTH // bn, n_i),
        in_specs=[pl.BlockSpec((bm, k), lambda j, i: (i, 0)),
                  pl.BlockSpec((bn, k), lambda j, i: (j, 0)),
                  pl.BlockSpec((shift, k), lambda j, i: ((bn // shift) * (j + 1), 0)),
                  slab_spec, slab_spec],
        out_specs=[pl.BlockSpec((bm, bn), lambda j, i: (i, j)), slab_spec, slab_spec],
        out_shape=[jax.ShapeDtypeStruct((m, PROJ_WIDTH), F32), table_bf16, table_bf16],
        scratch_shapes=[pltpu.VMEM((bn, k), BF16)],
        compiler_params=pltpu.CompilerParams(dimension_semantics=("arbitrary", "arbitrary"),
                                             vmem_limit_bytes=IN_PROJ_VMEM_LIMIT_BYTES),
        name="in_proj",
    )(x, w_t, w_t, table_a, table_b)


def _split3_dot(x, rhs_bf16):
    hi = x.astype(BF16)
    rest = x - hi.astype(F32)
    mid = rest.astype(BF16)
    lo = (rest - mid.astype(F32)).astype(BF16)
    return (jnp.dot(hi, rhs_bf16, preferred_element_type=F32)
            + jnp.dot(mid, rhs_bf16, preferred_element_type=F32)
            + jnp.dot(lo, rhs_bf16, preferred_element_type=F32))


def _dt_kernel(x_ref, w_ref, b_ref, ex_ref, dt_ref, dtx_ref):
    a = lax.dot_general(x_ref[...], w_ref[...], (((1,), (1,)), ((), ())),
                        preferred_element_type=F32) + b_ref[...]
    dt = jnp.maximum(a, 0.0) + jnp.log1p(jnp.exp(-jnp.abs(a)))
    dt_ref[...] = dt
    dtx_ref[...] = _split3_dot(dt, ex_ref[...])


def _dt_proj(x, w_t, b):
    m, k = x.shape
    n = w_t.shape[0]
    bm = _largest_divisor(m, (512, 256, 128, 64, 32, 16, 8))
    expand = (jnp.arange(SSD_INNER)[None, :] // SSD_HEADDIM == jnp.arange(n)[:, None]).astype(BF16)
    return pl.pallas_call(
        _dt_kernel,
        grid=(m // bm,),
        in_specs=[pl.BlockSpec((bm, k), lambda i: (i, 0)),
                  pl.BlockSpec((n, k), lambda i: (0, 0)),
                  pl.BlockSpec((1, n), lambda i: (0, 0)),
                  pl.BlockSpec((n, SSD_INNER), lambda i: (0, 0))],
        out_specs=[pl.BlockSpec((bm, n), lambda i: (i, 0)),
                   pl.BlockSpec((bm, SSD_INNER), lambda i: (i, 0))],
        out_shape=[jax.ShapeDtypeStruct((m, n), F32),
                   jax.ShapeDtypeStruct((m, SSD_INNER), F32)],
        compiler_params=_params("parallel"),
        name="dt_proj",
    )(x, w_t, b, expand)


def _merge_kernel(ya1_ref, ya2_ref, yb1_ref, yb2_ref, wa_ref, wb_ref, ga_ref, gb_ref, o_ref, *, n_first):
    i = pl.program_id(1)
    ya = _stacked_block(i, n_first, ya1_ref, ya2_ref)
    yb = _stacked_block(i, n_first, yb1_ref, yb2_ref)
    pa = jnp.dot(ya, wa_ref[...], preferred_element_type=F32)
    pb = jnp.dot(yb, wb_ref[...], preferred_element_type=F32)
    o_ref[...] = (_sigmoid(ga_ref[...]) * pa + _sigmoid(gb_ref[...]) * pb).astype(o_ref.dtype)


def _gated_merge(ya_parts, yb_parts, wa, wb, proj):
    (m1, k), m2 = ya_parts[0].shape, ya_parts[1].shape[0]
    n = wa.shape[1]
    bm = _row_block(m1, m2, (512, 256, 128, 64, 32, 16, 8))
    bn = 512
    n_first = m1 // bm
    y_first, y_second = _stacked_specs((bm, k), n_first, 1)
    return pl.pallas_call(
        functools.partial(_merge_kernel, n_first=n_first),
        grid=(n // bn, (m1 + m2) // bm),
        in_specs=[y_first, y_second, y_first, y_second,
                  pl.BlockSpec((k, bn), lambda j, i: (0, j)),
                  pl.BlockSpec((k, bn), lambda j, i: (0, j)),
                  pl.BlockSpec((bm, bn), lambda j, i: (i, COL_GA // bn + j)),
                  pl.BlockSpec((bm, bn), lambda j, i: (i, COL_GB // bn + j))],
        out_specs=pl.BlockSpec((bm, bn), lambda j, i: (i, j)),
        out_shape=jax.ShapeDtypeStruct((m1 + m2, n), BF16),
        compiler_params=_params("parallel", "parallel"),
        name="gated_merge",
    )(ya_parts[0], ya_parts[1], yb_parts[0], yb_parts[1], wa, wb, proj, proj)


def _residual_matmul_kernel(x_ref, w_ref, r1_ref, r2_ref, o_ref, *, n_first):
    r = _stacked_block(pl.program_id(1), n_first, r1_ref, r2_ref)
    o_ref[...] = r + jnp.dot(x_ref[...], w_ref[...], preferred_element_type=F32)


def _residual_matmul(x, w, r_parts):
    m, k = x.shape
    n = w.shape[1]
    m1, m2 = r_parts[0].shape[0], r_parts[1].shape[0]
    bm = _row_block(m1, m2, (512, 256, 128, 64, 32, 16, 8))
    bn = 1024
    n_first = m1 // bm
    r_first, r_second = _stacked_specs((bm, bn), n_first, 1, col_of=lambda j, i: j)
    return pl.pallas_call(
        functools.partial(_residual_matmul_kernel, n_first=n_first),
        grid=(n // bn, m // bm),
        in_specs=[pl.BlockSpec((bm, k), lambda j, i: (i, 0)),
                  pl.BlockSpec((k, bn), lambda j, i: (0, j)),
                  r_first, r_second],
        out_specs=pl.BlockSpec((bm, bn), lambda j, i: (i, j)),
        out_shape=jax.ShapeDtypeStruct((m, n), F32),
        compiler_params=_params("parallel", "parallel"),
        name="residual_matmul",
    )(x, w, r_parts[0], r_parts[1])


CONV_PAD = 8


def _conv_kernel(u_ref, buf_ref, w_ref, b_ref, o_ref, cache_ref, pad_ref, *, tb):
    t = pl.program_id(2)
    lo = CONV_PAD - (SSD_CONV - 1)

    @pl.when(t == 0)
    def _():
        pad_ref[lo:CONV_PAD, :] = buf_ref[0]

    @pl.when(t > 0)
    def _():
        pad_ref[lo:CONV_PAD, :] = pad_ref[tb + lo:tb + CONV_PAD, :]

    pad_ref[CONV_PAD:CONV_PAD + tb, :] = u_ref[...]
    acc = b_ref[...]
    for j in range(SSD_CONV):
        acc = acc + pad_ref[lo + j:lo + j + tb, :] * w_ref[j:j + 1, :]
    o_ref[...] = _silu(acc)

    @pl.when(t == pl.num_programs(2) - 1)
    def _():
        cache_ref[0] = pad_ref[tb + lo:tb + CONV_PAD, :]


def _conv_silu(proj, row_off, bt, seqlen, buf, conv_w, conv_b):
    tb = _largest_divisor(seqlen, (512, 256, 128, 64, 32))
    cb = 2048
    nt = seqlen // tb
    roff = row_off // tb
    coff = COL_XBC // cb
    return pl.pallas_call(
        functools.partial(_conv_kernel, tb=tb),
        grid=(SSD_CONV_DIM // cb, bt, nt),
        in_specs=[pl.BlockSpec((tb, cb), lambda c, b, t: (roff + b * nt + t, coff + c)),
                  pl.BlockSpec((1, SSD_CONV - 1, cb), lambda c, b, t: (b, 0, c)),
                  pl.BlockSpec((SSD_CONV, cb), lambda c, b, t: (0, c)),
                  pl.BlockSpec((1, cb), lambda c, b, t: (0, c))],
        out_specs=[pl.BlockSpec((tb, cb), lambda c, b, t: (b * nt + t, c)),
                   pl.BlockSpec((1, SSD_CONV - 1, cb), lambda c, b, t: (b, 0, c))],
        out_shape=[jax.ShapeDtypeStruct((bt * seqlen, SSD_CONV_DIM), F32),
                   jax.ShapeDtypeStruct((bt, SSD_CONV - 1, SSD_CONV_DIM), F32)],
        scratch_shapes=[pltpu.VMEM((CONV_PAD + tb, cb), F32)],
        compiler_params=_params("parallel", "parallel", "arbitrary"),
        name="conv_silu",
    )(proj, buf, conv_w, conv_b.reshape(1, SSD_CONV_DIM))


def _ssd_kernel(x_ref, b_ref, c_ref, z_ref, dtx_ref, dtt_ref, alx_ref, alt_ref, dsk_ref, nw_ref,
                s0_ref, y_ref, sout_ref, st_ref, *, q, nchunk, ng):
    t = pl.program_id(2)
    p, r, gw, n = SSD_HEADDIM, SSD_HEADS_PER_GROUP, SSD_GROUP_WIDTH, SSD_STATE

    @pl.when(t == 0)
    def _():
        st_ref[...] = s0_ref[0]

    row = lax.broadcasted_iota(jnp.int32, (q, q), 0)
    col = lax.broadcasted_iota(jnp.int32, (q, q), 1)
    lower = row >= col
    tril = lower.astype(BF16)
    triu = (row <= col).astype(BF16)
    a_x = -jnp.exp(alx_ref[...])
    a_t = -jnp.exp(alt_ref[...]).reshape(ng * r, 1)
    d_skip = dsk_ref[...]
    norm_w = nw_ref[...]
    nt_dims = (((1,), (1,)), ((), ()))
    tn_dims = (((0,), (0,)), ((), ()))

    def chunk(ci, carry):
        rows = pl.ds(pl.multiple_of(ci * q, q), q)
        x = x_ref[rows, :]
        dtx = dtx_ref[rows, :]
        cum = _cumsum_rows(tril, dtx * a_x)
        cum_t = _split3_dot(dtt_ref[:, ci].reshape(ng * r, q) * a_t, triu)
        cum_last = cum[q - 1:q, :]
        decay_in = jnp.exp(cum)
        xdt = (x * dtx).astype(BF16)
        xt = (x * (jnp.exp(cum_last - cum) * dtx)).astype(BF16)
        decay_st = jnp.exp(cum_last)
        gate = _silu(z_ref[rows, :])
        skip = d_skip * x
        for gi in range(ng):
            cols = slice(gi * gw, (gi + 1) * gw)
            bm = b_ref[rows, gi * n:(gi + 1) * n].astype(BF16)
            cm = c_ref[rows, gi * n:(gi + 1) * n].astype(BF16)
            cb = lax.dot_general(cm, bm, nt_dims, preferred_element_type=F32)
            st = st_ref[gi]
            y = jnp.dot(cm, st.astype(BF16), preferred_element_type=F32) * decay_in[:, cols]
            parts = []
            for h in range(r):
                c0 = gi * gw + h * p
                seg = cum[:, c0:c0 + q] - cum_t[gi * r + h:gi * r + h + 1, :]
                mix = (cb * jnp.where(lower, jnp.exp(seg), 0.0)).astype(BF16)
                parts.append(jnp.dot(mix, xdt[:, c0:c0 + p], preferred_element_type=F32))
            y = (y + jnp.concatenate(parts, axis=1) + skip[:, cols]) * gate[:, cols]
            ms = jnp.mean(y * y, axis=-1, keepdims=True)
            y_ref[rows, cols] = (y * lax.rsqrt(ms + NORM_EPS) * norm_w[:, cols]).astype(y_ref.dtype)
            st_ref[gi] = st * decay_st[:, cols] + lax.dot_general(
                bm, xt[:, cols], tn_dims, preferred_element_type=F32)
        return carry

    lax.fori_loop(0, nchunk, chunk, 0)

    @pl.when(t == pl.num_programs(2) - 1)
    def _():
        sout_ref[0] = st_ref[...]


def _ssd_scan(xconv, proj, dtx, dt, row_off, bt, seqlen, a_log, d_skip, ssd_norm_w, state0):
    g, r, p, n, gw = SSD_GROUPS, SSD_HEADS_PER_GROUP, SSD_HEADDIM, SSD_STATE, SSD_GROUP_WIDTH
    ng = SSD_GROUPS_PER_STEP
    q = min(CHUNK, seqlen)
    tb = _largest_divisor(seqlen, (512, 256, 128, 64, 32))
    nchunk = tb // q
    nt = seqlen // tb
    rows = bt * seqlen
    roff = row_off // tb
    dtt = dt[row_off:row_off + rows, :SSD_HEADS].reshape(rows // q, q, g, r).transpose(2, 0, 3, 1)
    alx = jnp.repeat(a_log, p).reshape(1, SSD_INNER)
    alt = a_log.reshape(g, r, 1)
    dsk = jnp.repeat(d_skip, p).reshape(1, SSD_INNER)
    s0 = state0.reshape(bt, g, r, p, n).transpose(0, 1, 4, 2, 3).reshape(bt, g, n, gw)
    bcol = SSD_INNER // (ng * n)
    ccol = bcol + g // ng
    wide = ng * gw
    y, sout = pl.pallas_call(
        functools.partial(_ssd_kernel, q=q, nchunk=nchunk, ng=ng),
        grid=(bt, g // ng, nt),
        in_specs=[pl.BlockSpec((tb, wide), lambda b, gi, t: (b * nt + t, gi)),
                  pl.BlockSpec((tb, ng * n), lambda b, gi, t: (b * nt + t, bcol + gi)),
                  pl.BlockSpec((tb, ng * n), lambda b, gi, t: (b * nt + t, ccol + gi)),
                  pl.BlockSpec((tb, wide), lambda b, gi, t: (roff + b * nt + t, COL_Z // wide + gi)),
                  pl.BlockSpec((tb, wide), lambda b, gi, t: (roff + b * nt + t, gi)),
                  pl.BlockSpec((ng, nchunk, r, q), lambda b, gi, t: (gi, b * nt + t, 0, 0)),
                  pl.BlockSpec((1, wide), lambda b, gi, t: (0, gi)),
                  pl.BlockSpec((ng, r, 1), lambda b, gi, t: (gi, 0, 0)),
                  pl.BlockSpec((1, wide), lambda b, gi, t: (0, gi)),
                  pl.BlockSpec((1, wide), lambda b, gi, t: (0, gi)),
                  pl.BlockSpec((1, ng, n, gw), lambda b, gi, t: (b, gi, 0, 0))],
        out_specs=[pl.BlockSpec((tb, wide), lambda b, gi, t: (b * nt + t, gi)),
                   pl.BlockSpec((1, ng, n, gw), lambda b, gi, t: (b, gi, 0, 0))],
        out_shape=[jax.ShapeDtypeStruct((rows, SSD_INNER), BF16),
                   jax.ShapeDtypeStruct((bt, g, n, gw), F32)],
        scratch_shapes=[pltpu.VMEM((ng, n, gw), F32)],
        compiler_params=_params("parallel", "parallel", "arbitrary"),
        name="ssd_scan",
    )(xconv, xconv, xconv, proj, dtx, dtt, alx, alt, dsk, ssd_norm_w.reshape(1, SSD_INNER), s0)
    state = sout.reshape(bt, g, n, r, p).transpose(0, 1, 3, 4, 2).reshape(bt, SSD_HEADS, p, n)
    return y, state


def _cumsum_rows(tril_bf16, x):
    hi = x.astype(BF16)
    rest = x - hi.astype(F32)
    mid = rest.astype(BF16)
    lo = (rest - mid.astype(F32)).astype(BF16)
    return (jnp.dot(tril_bf16, hi, preferred_element_type=F32)
            + jnp.dot(tril_bf16, mid, preferred_element_type=F32)
            + jnp.dot(tril_bf16, lo, preferred_element_type=F32))


def _hgrn_kernel(q_ref, f_ref, i_ref, g_ref, lbt_ref, nw_ref, s0_ref, o_ref, sout_ref, st_ref,
                 *, q, nchunk, nh):
    t = pl.program_id(2)

    @pl.when(t == 0)
    def _():
        st_ref[...] = s0_ref[0]

    hq = q // 2
    row = lax.broadcasted_iota(jnp.int32, (q, q), 0)
    col = lax.broadcasted_iota(jnp.int32, (q, q), 1)
    lower = row >= col
    tril = lower.astype(BF16)
    left_half = lax.broadcasted_iota(jnp.int32, (hq, LANES), 1) < hq
    k_pad = jnp.zeros((LANES - 3 * hq, nh * HG_K), BF16)
    tab = lbt_ref[...]
    e = jnp.exp(tab - jnp.max(tab, axis=0, keepdims=True))
    lb = e[0:1, :] / jnp.sum(e, axis=0, keepdims=True)
    norm_w = nw_ref[...]
    nt_dims = (((1,), (1,)), ((), ()))
    tn_dims = (((0,), (0,)), ((), ()))

    def decayed(x, cum_rows, ref_row, sign):
        return (x * jnp.exp(sign * (cum_rows - ref_row))).astype(BF16)

    def chunk(ci, carry):
        rows = pl.ds(pl.multiple_of(ci * q, q), q)
        qq = _silu(q_ref[rows, :]) * (HG_K ** -0.5)
        f = lb + (1.0 - lb) * _sigmoid(f_ref[rows, :])
        kk = 1.0 - f
        v = i_ref[rows, :].astype(BF16)
        gate = _silu(g_ref[rows, :])
        cum = _cumsum_rows(tril, jnp.log(f))
        top, bot = slice(0, hq), slice(hq, q)
        mid_top = cum[hq // 2 - 1:hq // 2, :]
        mid_bot = cum[hq + hq // 2 - 1:hq + hq // 2, :]
        edge = cum[hq - 1:hq, :]
        cum_last = cum[q - 1:q, :]
        q_tt = decayed(qq[top], cum[top], mid_top, 1.0)
        k_tt = decayed(kk[top], cum[top], mid_top, -1.0)
        q_bb = decayed(qq[bot], cum[bot], mid_bot, 1.0)
        k_bb = decayed(kk[bot], cum[bot], mid_bot, -1.0)
        q_bt = decayed(qq[bot], cum[bot], edge, 1.0)
        k_bt = decayed(kk[top], cum[top], edge, -1.0)
        qe = (qq * jnp.exp(cum)).astype(BF16)
        kt = (kk * jnp.exp(cum_last - cum)).astype(BF16)
        dec = jnp.exp(cum_last)
        q_stack = jnp.concatenate([q_tt, q_bb, q_bt], axis=0)
        k_stack = jnp.concatenate([k_tt, k_bb, k_bt, k_pad], axis=0)
        for h in range(nh):
            sl = slice(h * HG_K, (h + 1) * HG_K)
            s_all = lax.dot_general(q_stack[:, sl], k_stack[:, sl], nt_dims,
                                    preferred_element_type=F32)
            bottom_left = pltpu.roll(s_all[2 * hq:3 * hq, :], LANES - 2 * hq, axis=1)
            s_bot = jnp.where(left_half, bottom_left, s_all[hq:2 * hq, :])
            att = jnp.concatenate([s_all[0:hq, 0:q], s_bot[:, 0:q]], axis=0)
            att = jnp.where(lower, att, 0.0).astype(BF16)
            st = st_ref[h]
            o = jnp.dot(att, v[:, sl], preferred_element_type=F32) + lax.dot_general(
                qe[:, sl], st.astype(BF16), nt_dims, preferred_element_type=F32)
            ms = jnp.mean(o * o, axis=-1, keepdims=True)
            o_ref[rows, sl] = (o * lax.rsqrt(ms + NORM_EPS) * norm_w * gate[:, sl]).astype(o_ref.dtype)
            st_ref[h] = st * dec[:, sl] + lax.dot_general(v[:, sl], kt[:, sl], tn_dims,
                                                          preferred_element_type=F32)
        return carry

    lax.fori_loop(0, nchunk, chunk, 0)

    @pl.when(t == pl.num_programs(2) - 1)
    def _():
        sout_ref[0] = st_ref[...]


def _hgrn_scan(proj, row_off, bt, seqlen, lb_table, hg_norm_w, state0):
    q = min(CHUNK, seqlen)
    tb = _largest_divisor(seqlen, (512, 256, 128, 64, 32))
    nchunk = tb // q
    nt = seqlen // tb
    rows = bt * seqlen
    roff = row_off // tb
    nh = HGRN_HEADS_PER_STEP
    hw = nh * HG_K
    s0 = state0.transpose(0, 1, 3, 2)
    nrow = lb_table.shape[0]

    def col_spec(col0):
        return pl.BlockSpec((tb, hw), lambda b, h, t: (roff + b * nt + t, col0 // hw + h))

    o, sout = pl.pallas_call(
        functools.partial(_hgrn_kernel, q=q, nchunk=nchunk, nh=nh),
        grid=(bt, HG_HEADS // nh, nt),
        in_specs=[col_spec(COL_HQ), col_spec(COL_HF), col_spec(COL_HI), col_spec(COL_HG),
                  pl.BlockSpec((nrow, hw), lambda b, h, t: (0, h)),
                  pl.BlockSpec((1, HG_V), lambda b, h, t: (0, 0)),
                  pl.BlockSpec((1, nh, HG_V, HG_K), lambda b, h, t: (b, h, 0, 0))],
        out_specs=[pl.BlockSpec((tb, hw), lambda b, h, t: (b * nt + t, h)),
                   pl.BlockSpec((1, nh, HG_V, HG_K), lambda b, h, t: (b, h, 0, 0))],
        out_shape=[jax.ShapeDtypeStruct((rows, HG_HEADS * HG_V), BF16),
                   jax.ShapeDtypeStruct((bt, HG_HEADS, HG_V, HG_K), F32)],
        scratch_shapes=[pltpu.VMEM((nh, HG_V, HG_K), F32)],
        compiler_params=_params("parallel", "parallel", "arbitrary"),
        name="hgrn_scan",
    )(proj, proj, proj, proj, lb_table, hg_norm_w.reshape(1, HG_V), s0)
    return o, sout.transpose(0, 1, 3, 2)


def _top_values(s, count):
    vals = []
    work = s
    for _ in range(count):
        m = jnp.max(work, axis=0, keepdims=True)
        vals.append(m)
        work = jnp.where(work == m, -jnp.inf, work)
    return jnp.concatenate(vals, axis=0)


def _peer_query_kernel(w_ref, x_ref, sk_ref, thr_ref, e0_ref, s1_ref, e1_ref):
    half = PEER_QDIM // 2
    nh = thr_ref.shape[0]
    qt = jnp.dot(w_ref[...], x_ref[...], preferred_element_type=F32)
    for h in range(nh):
        q0 = qt[h * PEER_QDIM:h * PEER_QDIM + half]
        q1 = qt[h * PEER_QDIM + half:(h + 1) * PEER_QDIM]
        s0 = jnp.dot(sk_ref[0], q0, precision=HIGHEST, preferred_element_type=F32)
        s1 = jnp.dot(sk_ref[1], q1, precision=HIGHEST, preferred_element_type=F32)
        sv0 = _top_values(s0, PEER_TOPK)
        sv1 = _top_values(s1, PEER_TOPK)
        cand = [sv0[0:1] + sv1]
        for a in range(1, 8):
            cand.append(sv0[a:a + 1] + sv1[0:8])
        cand.append(sv0[8:16] + sv1[0:1])
        top = _top_values(jnp.concatenate(cand, axis=0), PEER_TOPK)
        z = jnp.sum(jnp.exp(top - top[0:1]), axis=0, keepdims=True)
        tau = top[PEER_TOPK - 1:PEER_TOPK]
        thr = jnp.full(s0.shape, jnp.inf, F32)
        for b in range(PEER_TOPK):
            thr = jnp.where(s0 + sv1[b:b + 1] >= tau, sv1[b:b + 1], thr)
        thr_ref[h] = thr
        s1_ref[h] = s1
        e0_ref[h] = jnp.exp(s0 - sv0[0:1]) / z
        e1_ref[h] = jnp.exp(s1 - sv1[0:1])


def _peer_query(w_pq_t, xn_t, sub_keys):
    d, t = xn_t.shape
    tb = _largest_divisor(t, (256, 128))
    nh = PEER_QUERY_HEADS_PER_STEP
    keyed = jax.ShapeDtypeStruct((PEER_HEADS, PEER_KEYS, t), F32)
    key_spec = pl.BlockSpec((nh, PEER_KEYS, tb), lambda i, h: (h, 0, i))
    return pl.pallas_call(
        _peer_query_kernel,
        grid=(t // tb, PEER_HEADS // nh),
        in_specs=[pl.BlockSpec((nh * PEER_QDIM, d), lambda i, h: (h, 0)),
                  pl.BlockSpec((d, tb), lambda i, h: (0, i)),
                  pl.BlockSpec((2, PEER_KEYS, PEER_QDIM // 2), lambda i, h: (0, 0, 0))],
        out_specs=[key_spec, key_spec, key_spec, key_spec],
        out_shape=[keyed, keyed, keyed, keyed],
        compiler_params=_params("parallel", "arbitrary"),
        name="peer_query",
    )(w_pq_t, xn_t, sub_keys)


def _gelu_tanh(x):
    return 0.5 * x * (1.0 + jnp.tanh(0.7978845608028654 * (x + 0.044715 * (x * x * x))))


def _peer_expert_kernel(x_ref, u_ref, v_ref, thr_ref, e0_ref, s1_ref, e1_ref, o_ref, w_ref, *, nb, tb):
    e = pl.program_id(1)

    @pl.when(e == 0)
    def _():
        o_ref[...] = jnp.zeros_like(o_ref)

    act = _gelu_tanh(jnp.dot(u_ref[...], x_ref[...], preferred_element_type=F32))

    for i in range(nb):
        for tl in range(tb // LANES):
            lanes = pl.ds(tl * LANES, LANES)
            acc = jnp.zeros((PEER_KEYS, LANES), F32)
            for h in range(PEER_HEADS):
                thr = thr_ref[h, 0, i:i + 1, lanes]
                c0 = e0_ref[h, 0, i:i + 1, lanes]
                acc = acc + jnp.where(s1_ref[h, :, lanes] >= thr, e1_ref[h, :, lanes] * c0, 0.0)
            w_ref[i * PEER_KEYS:(i + 1) * PEER_KEYS, lanes] = acc
    g = (w_ref[...] * act).astype(BF16)
    o_ref[...] += lax.dot_general(g, v_ref[...], (((0,), (0,)), ((), ())),
                                  preferred_element_type=F32)


def _peer_experts(xn_t, u, v, thr, e0, s1, e1):
    d, t = xn_t.shape
    n_exp = u.shape[0]
    tb = _largest_divisor(t, (512, 256, 128))
    nb = 4
    eb = nb * PEER_KEYS
    once = pl.Buffered(1)
    keyed = pl.BlockSpec((PEER_HEADS, PEER_KEYS, tb), lambda i, e: (0, 0, i), pipeline_mode=once)
    rows = pl.BlockSpec((PEER_HEADS, 1, nb, tb), lambda i, e: (0, e, 0, i))
    thr = thr.reshape(PEER_HEADS, PEER_KEYS // nb, nb, t)
    e0 = e0.reshape(PEER_HEADS, PEER_KEYS // nb, nb, t)
    return pl.pallas_call(
        functools.partial(_peer_expert_kernel, nb=nb, tb=tb),
        grid=(t // tb, n_exp // eb),
        in_specs=[pl.BlockSpec((d, tb), lambda i, e: (0, i), pipeline_mode=once),
                  pl.BlockSpec((eb, d), lambda i, e: (e, 0)),
                  pl.BlockSpec((eb, d), lambda i, e: (e, 0)),
                  rows, rows, keyed, keyed],
        out_specs=pl.BlockSpec((tb, d), lambda i, e: (i, 0)),
        out_shape=jax.ShapeDtypeStruct((t, d), F32),
        scratch_shapes=[pltpu.VMEM((eb, tb), F32)],
        compiler_params=_params("parallel", "arbitrary"),
        name="peer_experts",
    )(xn_t, u, v, thr, e0, s1, e1)


def _branches(proj, dtx, dt, row_off, bt, seqlen, conv_buf, s_ssd, s_hg, conv_w, conv_b, a_log, d_skip,
              ssd_norm_w, lb_table, hg_norm_w):
    xconv, new_conv = _conv_silu(proj, row_off, bt, seqlen, conv_buf, conv_w, conv_b)
    y_ssd, s_ssd_new = _ssd_scan(xconv, proj, dtx, dt, row_off, bt, seqlen,
                                 a_log, d_skip, ssd_norm_w, s_ssd)
    o_hg, s_hg_new = _hgrn_scan(proj, row_off, bt, seqlen, lb_table, hg_norm_w, s_hg)
    return y_ssd, o_hg, s_ssd_new, new_conv, s_hg_new


def kernel(x_prompt, x_sample, state_ssd, cache_ssd_conv, state_hgrn, lb_table, norm1_w, w_in, conv_w, conv_b, dt_bias, a_log, d_skip, ssd_norm_w, w_ssd_out, hg_norm_w, w_hg_out, w_o, norm2_w, w_pq, sub_keys, u_experts, v_experts, final_norm_w):
    assert w_in.shape[0] == 1 and lb_table.shape[0] == 2, "single-layer stack"
    bp, lp, d = x_prompt.shape
    bs, ls, _ = x_sample.shape
    tp, ts = bp * lp, bs * ls
    x_parts = (x_prompt.reshape(tp, d), x_sample.reshape(ts, d))

    w_in_t = w_in[0].T
    w_dt = jnp.pad(w_in_t[DT_COL:DT_COL + SSD_HEADS], ((0, LANES - SSD_HEADS), (0, 0))).astype(BF16)
    b_dt = jnp.pad(dt_bias[0], (0, LANES - SSD_HEADS)).reshape(1, LANES)

    xn = _rmsnorm_stacked(*x_parts, norm1_w[0], BF16)
    proj, u_bf16, v_bf16 = _in_proj(xn, w_in_t, u_experts[0], v_experts[0])
    dt, dtx = _dt_proj(xn, w_dt, b_dt)

    zeros = functools.partial(jnp.zeros, dtype=F32)
    common = (conv_w[0], conv_b[0], a_log[0], d_skip[0], ssd_norm_w[0], lb_table, hg_norm_w[0])
    ya_p, yb_p, ssd_p, conv_p, hg_p = _branches(
        proj, dtx, dt, 0, bp, lp,
        zeros((bp, SSD_CONV - 1, SSD_CONV_DIM)), zeros((bp, SSD_HEADS, SSD_HEADDIM, SSD_STATE)),
        zeros((bp, HG_HEADS, HG_K, HG_V)), *common)
    ya_s, yb_s, ssd_s, conv_s, hg_s = _branches(
        proj, dtx, dt, tp, bs, ls, cache_ssd_conv[0], state_ssd[0], state_hgrn[0], *common)

    merged = _gated_merge((ya_p, ya_s), (yb_p, yb_s), w_ssd_out[0].astype(BF16),
                          w_hg_out[0].astype(BF16), proj)
    h = _residual_matmul(merged, w_o[0].astype(BF16), x_parts)

    hn_t = _rmsnorm_t(h, norm2_w[0], BF16)
    thr, e0, s1, e1 = _peer_query(w_pq[0].T.astype(BF16), hn_t, sub_keys[0])
    peer = _peer_experts(hn_t, u_bf16, v_bf16, thr, e0, s1, e1)
    y_prompt = _add_rmsnorm(h, peer, final_norm_w, 0, tp).reshape(bp, lp, d)
    y_sample = _add_rmsnorm(h, peer, final_norm_w, tp, ts).reshape(bs, ls, d)
    return (y_prompt, y_sample, ssd_p[None], conv_p[None], hg_p[None],
            ssd_s[None], conv_s[None], hg_s[None])
```

```python
import functools

import jax
import jax.numpy as jnp
from jax import lax
from jax.experimental import pallas as pl
from jax.experimental.pallas import tpu as pltpu

F32 = jnp.float32
BF16 = jnp.bfloat16
HIGHEST = lax.Precision.HIGHEST

NORM_EPS = 1e-6
LANES = 128
VMEM_LIMIT_BYTES = 56 * 1024 * 1024
IN_PROJ_VMEM_LIMIT_BYTES = 62 * 1024 * 1024

D_MODEL = 4096
SSD_GROUPS = 8
SSD_HEADS_PER_GROUP = 8
SSD_HEADDIM = 64
SSD_STATE = 128
SSD_HEADS = SSD_GROUPS * SSD_HEADS_PER_GROUP
SSD_INNER = SSD_HEADS * SSD_HEADDIM
SSD_GROUP_WIDTH = SSD_HEADS_PER_GROUP * SSD_HEADDIM
SSD_CONV = 4
SSD_CONV_DIM = SSD_INNER + 2 * SSD_GROUPS * SSD_STATE
HG_HEADS = 32
HG_K = 128
HG_V = 128
CHUNK = 64
PEER_HEADS = 8
PEER_KEYS = 128
PEER_TOPK = 16
PEER_QDIM = 256
PEER_QUERY_HEADS_PER_STEP = 4
SSD_GROUPS_PER_STEP = 4
HGRN_HEADS_PER_STEP = 8

COL_Z = 0
COL_XBC = SSD_INNER
COL_HQ = COL_XBC + SSD_CONV_DIM
COL_HF = COL_HQ + HG_HEADS * HG_K
COL_HI = COL_HF + HG_HEADS * HG_K
COL_HG = COL_HI + HG_HEADS * HG_V
COL_GA = COL_HG + HG_HEADS * HG_V
COL_GB = COL_GA + D_MODEL
PROJ_WIDTH = COL_GB + D_MODEL
DT_COL = SSD_INNER + SSD_CONV_DIM


def _largest_divisor(n, candidates):
    for c in candidates:
        if n % c == 0:
            return c
    raise ValueError(f"no block size among {candidates} divides {n}")


def _params(*semantics):
    return pltpu.CompilerParams(dimension_semantics=semantics,
                                vmem_limit_bytes=VMEM_LIMIT_BYTES)


def _sigmoid(x):
    return 1.0 / (1.0 + jnp.exp(-x))


def _silu(x):
    return x * _sigmoid(x)


def _stacked_specs(block, n_first, row_pos, col_of=None):
    def first(*idx):
        return (jnp.minimum(idx[row_pos], n_first - 1), 0 if col_of is None else col_of(*idx))

    def second(*idx):
        return (jnp.maximum(idx[row_pos] - n_first, 0), 0 if col_of is None else col_of(*idx))

    return pl.BlockSpec(block, first), pl.BlockSpec(block, second)


def _stacked_block(i, n_first, first_ref, second_ref):
    return jnp.where(i < n_first, first_ref[...], second_ref[...])


def _row_block(n_first_rows, n_second_rows, candidates):
    for c in candidates:
        if n_first_rows % c == 0 and n_second_rows % c == 0:
            return c
    raise ValueError(f"no block size among {candidates} divides {n_first_rows} and {n_second_rows}")


def _rmsnorm_stacked_kernel(xa_ref, xb_ref, w_ref, o_ref, *, n_first):
    x = _stacked_block(pl.program_id(0), n_first, xa_ref, xb_ref)
    ms = jnp.mean(x * x, axis=-1, keepdims=True)
    o_ref[...] = (x * lax.rsqrt(ms + NORM_EPS) * w_ref[...]).astype(o_ref.dtype)


def _rmsnorm_stacked(xa, xb, w, out_dtype):
    (ta, d), tb_rows = xa.shape, xb.shape[0]
    bm = _row_block(ta, tb_rows, (256, 128, 64, 32, 16, 8))
    n_first = ta // bm
    spec_a, spec_b = _stacked_specs((bm, d), n_first, 0)
    return pl.pallas_call(
        functools.partial(_rmsnorm_stacked_kernel, n_first=n_first),
        grid=((ta + tb_rows) // bm,),
        in_specs=[spec_a, spec_b, pl.BlockSpec((1, d), lambda i: (0, 0))],
        out_specs=pl.BlockSpec((bm, d), lambda i: (i, 0)),
        out_shape=jax.ShapeDtypeStruct((ta + tb_rows, d), out_dtype),
        compiler_params=_params("parallel"),
        name="rmsnorm_in",
    )(xa, xb, w.reshape(1, d))


def _rmsnorm_t_kernel(x_ref, w_ref, o_ref):
    x = x_ref[...]
    ms = jnp.mean(x * x, axis=-1, keepdims=True)
    o_ref[...] = jnp.transpose(x * lax.rsqrt(ms + NORM_EPS) * w_ref[...]).astype(o_ref.dtype)


def _rmsnorm_t(x, w, out_dtype):
    t, d = x.shape
    bm = _largest_divisor(t, (256, 128))
    return pl.pallas_call(
        _rmsnorm_t_kernel,
        grid=(t // bm,),
        in_specs=[pl.BlockSpec((bm, d), lambda i: (i, 0)),
                  pl.BlockSpec((1, d), lambda i: (0, 0))],
        out_specs=pl.BlockSpec((d, bm), lambda i: (0, i)),
        out_shape=jax.ShapeDtypeStruct((d, t), out_dtype),
        compiler_params=_params("parallel"),
        name="rmsnorm_t",
    )(x, w.reshape(1, d))


def _add_rmsnorm_kernel(a_ref, b_ref, w_ref, o_ref):
    x = a_ref[...] + b_ref[...]
    ms = jnp.mean(x * x, axis=-1, keepdims=True)
    o_ref[...] = x * lax.rsqrt(ms + NORM_EPS) * w_ref[...]


def _add_rmsnorm(a, b, w, row_off, rows):
    d = a.shape[1]
    bm = _row_block(rows, row_off, (256, 128, 64, 32, 16, 8))
    off = row_off // bm
    return pl.pallas_call(
        _add_rmsnorm_kernel,
        grid=(rows // bm,),
        in_specs=[pl.BlockSpec((bm, d), lambda i: (off + i, 0)),
                  pl.BlockSpec((bm, d), lambda i: (off + i, 0)),
                  pl.BlockSpec((1, d), lambda i: (0, 0))],
        out_specs=pl.BlockSpec((bm, d), lambda i: (i, 0)),
        out_shape=jax.ShapeDtypeStruct((rows, d), F32),
        compiler_params=_params("parallel"),
        name="add_rmsnorm",
    )(a, b, w.reshape(1, d))


def _in_proj_kernel(x_ref, wa_ref, wb_ref, ta_ref, tb_ref, o_ref, ta_out_ref, tb_out_ref, wbuf_ref,
                    *, n_plain, shift, chunk):
    ta_out_ref[...] = ta_ref[...].astype(ta_out_ref.dtype)
    tb_out_ref[...] = tb_ref[...].astype(tb_out_ref.dtype)
    j = pl.program_id(0)
    i = pl.program_id(1)
    bn = wa_ref.shape[0]

    @pl.when(jnp.logical_and(i == 0, j < n_plain))
    def _():
        def body(c, carry):
            rows = pl.ds(pl.multiple_of(c * chunk, chunk), chunk)
            wbuf_ref[rows, :] = wa_ref[rows, :].astype(BF16)
            return carry
        lax.fori_loop(0, bn // chunk, body, 0)

    @pl.when(jnp.logical_and(i == 0, j >= n_plain))
    def _():
        def body(c, carry):
            dst = pl.ds(pl.multiple_of(c * chunk, chunk), chunk)
            src = pl.ds(pl.multiple_of(c * chunk + shift, shift), chunk)
            wbuf_ref[dst, :] = wa_ref[src, :].astype(BF16)
            return carry
        lax.fori_loop(0, (bn - shift) // chunk, body, 0)
        wbuf_ref[bn - shift:bn, :] = wb_ref[...].astype(BF16)

    o_ref[...] = lax.dot_general(x_ref[...], wbuf_ref[...], (((1,), (1,)), ((), ())),
                                 preferred_element_type=F32)


def _in_proj(x, w_t, table_a, table_b):
    m, k = x.shape
    bm = _largest_divisor(m, (512, 256, 128, 64, 32, 16, 8))
    bn = 1024
    shift = SSD_HEADS
    assert DT_COL % bn == 0 and PROJ_WIDTH % bn == 0 and bn % shift == 0
    n_i = m // bm
    steps = (PROJ_WIDTH // bn) * n_i
    rows, width = table_a.shape
    assert table_b.shape == table_a.shape
    slab = next(r for r in range(16, rows + 1, 16) if rows % r == 0 and rows // r <= steps)
    last = rows // slab - 1
    slab_spec = pl.BlockSpec((slab, width), lambda j, i: (jnp.minimum(j * n_i + i, last), 0))
    table_bf16 = jax.ShapeDtypeStruct((rows, width), BF16)
    return pl.pallas_call(
        functools.partial(_in_proj_kernel, n_plain=DT_COL // bn, shift=shift, chunk=shift),
        grid=(PROJ_WIDTH // bn, n_i),
        in_specs=[pl.BlockSpec((bm, k), lambda j, i: (i, 0)),
                  pl.BlockSpec((bn, k), lambda j, i: (j, 0)),
                  pl.BlockSpec((shift, k), lambda j, i: ((bn // shift) * (j + 1), 0)),
                  slab_spec, slab_spec],
        out_specs=[pl.BlockSpec((bm, bn), lambda j, i: (i, j)), slab_spec, slab_spec],
        out_shape=[jax.ShapeDtypeStruct((m, PROJ_WIDTH), F32), table_bf16, table_bf16],
        scratch_shapes=[pltpu.VMEM((bn, k), BF16)],
        compiler_params=pltpu.CompilerParams(dimension_semantics=("arbitrary", "arbitrary"),
                                             vmem_limit_bytes=IN_PROJ_VMEM_LIMIT_BYTES),
        name="in_proj",
    )(x, w_t, w_t, table_a, table_b)


def _split3_dot(x, rhs_bf16):
    hi = x.astype(BF16)
    rest = x - hi.astype(F32)
    mid = rest.astype(BF16)
    lo = (rest - mid.astype(F32)).astype(BF16)
    return (jnp.dot(hi, rhs_bf16, preferred_element_type=F32)
            + jnp.dot(mid, rhs_bf16, preferred_element_type=F32)
            + jnp.dot(lo, rhs_bf16, preferred_element_type=F32))


def _dt_kernel(x_ref, w_ref, b_ref, ex_ref, dt_ref, dtx_ref):
    a = lax.dot_general(x_ref[...], w_ref[...], (((1,), (1,)), ((), ())),
                        preferred_element_type=F32) + b_ref[...]
    dt = jnp.maximum(a, 0.0) + jnp.log1p(jnp.exp(-jnp.abs(a)))
    dt_ref[...] = dt
    dtx_ref[...] = _split3_dot(dt, ex_ref[...])


def _dt_proj(x, w_t, b):
    m, k = x.shape
    n = w_t.shape[0]
    bm = _largest_divisor(m, (512, 256, 128, 64, 32, 16, 8))
    expand = (jnp.arange(SSD_INNER)[None, :] // SSD_HEADDIM == jnp.arange(n)[:, None]).astype(BF16)
    return pl.pallas_call(
        _dt_kernel,
        grid=(m // bm,),
        in_specs=[pl.BlockSpec((bm, k), lambda i: (i, 0)),
                  pl.BlockSpec((n, k), lambda i: (0, 0)),
                  pl.BlockSpec((1, n), lambda i: (0, 0)),
                  pl.BlockSpec((n, SSD_INNER), lambda i: (0, 0))],
        out_specs=[pl.BlockSpec((bm, n), lambda i: (i, 0)),
                   pl.BlockSpec((bm, SSD_INNER), lambda i: (i, 0))],
        out_shape=[jax.ShapeDtypeStruct((m, n), F32),
                   jax.ShapeDtypeStruct((m, SSD_INNER), F32)],
        compiler_params=_params("parallel"),
        name="dt_proj",
    )(x, w_t, b, expand)


def _merge_kernel(ya1_ref, ya2_ref, yb1_ref, yb2_ref, wa_ref, wb_ref, ga_ref, gb_ref, o_ref, *, n_first):
    i = pl.program_id(1)
    ya = _stacked_block(i, n_first, ya1_ref, ya2_ref)
    yb = _stacked_block(i, n_first, yb1_ref, yb2_ref)
    pa = jnp.dot(ya, wa_ref[...], preferred_element_type=F32)
    pb = jnp.dot(yb, wb_ref[...], preferred_element_type=F32)
    o_ref[...] = (_sigmoid(ga_ref[...]) * pa + _sigmoid(gb_ref[...]) * pb).astype(o_ref.dtype)


def _gated_merge(ya_parts, yb_parts, wa, wb, proj):
    (m1, k), m2 = ya_parts[0].shape, ya_parts[1].shape[0]
    n = wa.shape[1]
    bm = _row_block(m1, m2, (512, 256, 128, 64, 32, 16, 8))
    bn = 512
    n_first = m1 // bm
    y_first, y_second = _stacked_specs((bm, k), n_first, 1)
    return pl.pallas_call(
        functools.partial(_merge_kernel, n_first=n_first),
        grid=(n // bn, (m1 + m2) // bm),
        in_specs=[y_first, y_second, y_first, y_second,
                  pl.BlockSpec((k, bn), lambda j, i: (0, j)),
                  pl.BlockSpec((k, bn), lambda j, i: (0, j)),
                  pl.BlockSpec((bm, bn), lambda j, i: (i, COL_GA // bn + j)),
                  pl.BlockSpec((bm, bn), lambda j, i: (i, COL_GB // bn + j))],
        out_specs=pl.BlockSpec((bm, bn), lambda j, i: (i, j)),
        out_shape=jax.ShapeDtypeStruct((m1 + m2, n), BF16),
        compiler_params=_params("parallel", "parallel"),
        name="gated_merge",
    )(ya_parts[0], ya_parts[1], yb_parts[0], yb_parts[1], wa, wb, proj, proj)


def _residual_matmul_kernel(x_ref, w_ref, r1_ref, r2_ref, o_ref, *, n_first):
    r = _stacked_block(pl.program_id(1), n_first, r1_ref, r2_ref)
    o_ref[...] = r + jnp.dot(x_ref[...], w_ref[...], preferred_element_type=F32)


def _residual_matmul(x, w, r_parts):
    m, k = x.shape
    n = w.shape[1]
    m1, m2 = r_parts[0].shape[0], r_parts[1].shape[0]
    bm = _row_block(m1, m2, (512, 256, 128, 64, 32, 16, 8))
    bn = 1024
    n_first = m1 // bm
    r_first, r_second = _stacked_specs((bm, bn), n_first, 1, col_of=lambda j, i: j)
    return pl.pallas_call(
        functools.partial(_residual_matmul_kernel, n_first=n_first),
        grid=(n // bn, m // bm),
        in_specs=[pl.BlockSpec((bm, k), lambda j, i: (i, 0)),
                  pl.BlockSpec((k, bn), lambda j, i: (0, j)),
                  r_first, r_second],
        out_specs=pl.BlockSpec((bm, bn), lambda j, i: (i, j)),
        out_shape=jax.ShapeDtypeStruct((m, n), F32),
        compiler_params=_params("parallel", "parallel"),
        name="residual_matmul",
    )(x, w, r_parts[0], r_parts[1])


CONV_PAD = 8


def _conv_kernel(u_ref, buf_ref, w_ref, b_ref, o_ref, cache_ref, pad_ref, *, tb):
    t = pl.program_id(2)
    lo = CONV_PAD - (SSD_CONV - 1)

    @pl.when(t == 0)
    def _():
        pad_ref[lo:CONV_PAD, :] = buf_ref[0]

    @pl.when(t > 0)
    def _():
        pad_ref[lo:CONV_PAD, :] = pad_ref[tb + lo:tb + CONV_PAD, :]

    pad_ref[CONV_PAD:CONV_PAD + tb, :] = u_ref[...]
    acc = b_ref[...]
    for j in range(SSD_CONV):
        acc = acc + pad_ref[lo + j:lo + j + tb, :] * w_ref[j:j + 1, :]
    o_ref[...] = _silu(acc)

    @pl.when(t == pl.num_programs(2) - 1)
    def _():
        cache_ref[0] = pad_ref[tb + lo:tb + CONV_PAD, :]


def _conv_silu(proj, row_off, bt, seqlen, buf, conv_w, conv_b):
    tb = _largest_divisor(seqlen, (512, 256, 128, 64, 32))
    cb = 2048
    nt = seqlen // tb
    roff = row_off // tb
    coff = COL_XBC // cb
    return pl.pallas_call(
        functools.partial(_conv_kernel, tb=tb),
        grid=(SSD_CONV_DIM // cb, bt, nt),
        in_specs=[pl.BlockSpec((tb, cb), lambda c, b, t: (roff + b * nt + t, coff + c)),
                  pl.BlockSpec((1, SSD_CONV - 1, cb), lambda c, b, t: (b, 0, c)),
                  pl.BlockSpec((SSD_CONV, cb), lambda c, b, t: (0, c)),
                  pl.BlockSpec((1, cb), lambda c, b, t: (0, c))],
        out_specs=[pl.BlockSpec((tb, cb), lambda c, b, t: (b * nt + t, c)),
                   pl.BlockSpec((1, SSD_CONV - 1, cb), lambda c, b, t: (b, 0, c))],
        out_shape=[jax.ShapeDtypeStruct((bt * seqlen, SSD_CONV_DIM), F32),
                   jax.ShapeDtypeStruct((bt, SSD_CONV - 1, SSD_CONV_DIM), F32)],
        scratch_shapes=[pltpu.VMEM((CONV_PAD + tb, cb), F32)],
        compiler_params=_params("parallel", "parallel", "arbitrary"),
        name="conv_silu",
    )(proj, buf, conv_w, conv_b.reshape(1, SSD_CONV_DIM))


def _ssd_kernel(x_ref, b_ref, c_ref, z_ref, dtx_ref, dtt_ref, alx_ref, alt_ref, dsk_ref, nw_ref,
                s0_ref, y_ref, sout_ref, st_ref, *, q, nchunk, ng):
    t = pl.program_id(2)
    p, r, gw, n = SSD_HEADDIM, SSD_HEADS_PER_GROUP, SSD_GROUP_WIDTH, SSD_STATE

    @pl.when(t == 0)
    def _():
        st_ref[...] = s0_ref[0]

    row = lax.broadcasted_iota(jnp.int32, (q, q), 0)
    col = lax.broadcasted_iota(jnp.int32, (q, q), 1)
    lower = row >= col
    tril = lower.astype(BF16)
    triu = (row <= col).astype(BF16)
    a_x = -jnp.exp(alx_ref[...])
    a_t = -jnp.exp(alt_ref[...]).reshape(ng * r, 1)
    d_skip = dsk_ref[...]
    norm_w = nw_ref[...]
    nt_dims = (((1,), (1,)), ((), ()))
    tn_dims = (((0,), (0,)), ((), ()))

    def chunk(ci, carry):
        rows = pl.ds(pl.multiple_of(ci * q, q), q)
        x = x_ref[rows, :]
        dtx = dtx_ref[rows, :]
        cum = _cumsum_rows(tril, dtx * a_x)
        cum_t = _split3_dot(dtt_ref[:, ci].reshape(ng * r, q) * a_t, triu)
        cum_last = cum[q - 1:q, :]
        decay_in = jnp.exp(cum)
        xdt = (x * dtx).astype(BF16)
        xt = (x * (jnp.exp(cum_last - cum) * dtx)).astype(BF16)
        decay_st = jnp.exp(cum_last)
        gate = _silu(z_ref[rows, :])
        skip = d_skip * x
        for gi in range(ng):
            cols = slice(gi * gw, (gi + 1) * gw)
            bm = b_ref[rows, gi * n:(gi + 1) * n].astype(BF16)
            cm = c_ref[rows, gi * n:(gi + 1) * n].astype(BF16)
            cb = lax.dot_general(cm, bm, nt_dims, preferred_element_type=F32)
            st = st_ref[gi]
            y = jnp.dot(cm, st.astype(BF16), preferred_element_type=F32) * decay_in[:, cols]
            parts = []
            for h in range(r):
                c0 = gi * gw + h * p
                seg = cum[:, c0:c0 + q] - cum_t[gi * r + h:gi * r + h + 1, :]
                mix = (cb * jnp.where(lower, jnp.exp(seg), 0.0)).astype(BF16)
                parts.append(jnp.dot(mix, xdt[:, c0:c0 + p], preferred_element_type=F32))
            y = (y + jnp.concatenate(parts, axis=1) + skip[:, cols]) * gate[:, cols]
            ms = jnp.mean(y * y, axis=-1, keepdims=True)
            y_ref[rows, cols] = (y * lax.rsqrt(ms + NORM_EPS) * norm_w[:, cols]).astype(y_ref.dtype)
            st_ref[gi] = st * decay_st[:, cols] + lax.dot_general(
                bm, xt[:, cols], tn_dims, preferred_element_type=F32)
        return carry

    lax.fori_loop(0, nchunk, chunk, 0)

    @pl.when(t == pl.num_programs(2) - 1)
    def _():
        sout_ref[0] = st_ref[...]


def _ssd_scan(xconv, proj, dtx, dt, row_off, bt, seqlen, a_log, d_skip, ssd_norm_w, state0):
    g, r, p, n, gw = SSD_GROUPS, SSD_HEADS_PER_GROUP, SSD_HEADDIM, SSD_STATE, SSD_GROUP_WIDTH
    ng = SSD_GROUPS_PER_STEP
    q = min(CHUNK, seqlen)
    tb = _largest_divisor(seqlen, (512, 256, 128, 64, 32))
    nchunk = tb // q
    nt = seqlen // tb
    rows = bt * seqlen
    roff = row_off // tb
    dtt = dt[row_off:row_off + rows, :SSD_HEADS].reshape(rows // q, q, g, r).transpose(2, 0, 3, 1)
    alx = jnp.repeat(a_log, p).reshape(1, SSD_INNER)
    alt = a_log.reshape(g, r, 1)
    dsk = jnp.repeat(d_skip, p).reshape(1, SSD_INNER)
    s0 = state0.reshape(bt, g, r, p, n).transpose(0, 1, 4, 2, 3).reshape(bt, g, n, gw)
    bcol = SSD_INNER // (ng * n)
    ccol = bcol + g // ng
    wide = ng * gw
    y, sout = pl.pallas_call(
        functools.partial(_ssd_kernel, q=q, nchunk=nchunk, ng=ng),
        grid=(bt, g // ng, nt),
        in_specs=[pl.BlockSpec((tb, wide), lambda b, gi, t: (b * nt + t, gi)),
                  pl.BlockSpec((tb, ng * n), lambda b, gi, t: (b * nt + t, bcol + gi)),
                  pl.BlockSpec((tb, ng * n), lambda b, gi, t: (b * nt + t, ccol + gi)),
                  pl.BlockSpec((tb, wide), lambda b, gi, t: (roff + b * nt + t, COL_Z // wide + gi)),
                  pl.BlockSpec((tb, wide), lambda b, gi, t: (roff + b * nt + t, gi)),
                  pl.BlockSpec((ng, nchunk, r, q), lambda b, gi, t: (gi, b * nt + t, 0, 0)),
                  pl.BlockSpec((1, wide), lambda b, gi, t: (0, gi)),
                  pl.BlockSpec((ng, r, 1), lambda b, gi, t: (gi, 0, 0)),
                  pl.BlockSpec((1, wide), lambda b, gi, t: (0, gi)),
                  pl.BlockSpec((1, wide), lambda b, gi, t: (0, gi)),
                  pl.BlockSpec((1, ng, n, gw), lambda b, gi, t: (b, gi, 0, 0))],
        out_specs=[pl.BlockSpec((tb, wide), lambda b, gi, t: (b * nt + t, gi)),
                   pl.BlockSpec((1, ng, n, gw), lambda b, gi, t: (b, gi, 0, 0))],
        out_shape=[jax.ShapeDtypeStruct((rows, SSD_INNER), BF16),
                   jax.ShapeDtypeStruct((bt, g, n, gw), F32)],
        scratch_shapes=[pltpu.VMEM((ng, n, gw), F32)],
        compiler_params=_params("parallel", "parallel", "arbitrary"),
        name="ssd_scan",
    )(xconv, xconv, xconv, proj, dtx, dtt, alx, alt, dsk, ssd_norm_w.reshape(1, SSD_INNER), s0)
    state = sout.reshape(bt, g, n, r, p).transpose(0, 1, 3, 4, 2).reshape(bt, SSD_HEADS, p, n)
    return y, state


def _cumsum_rows(tril_bf16, x):
    hi = x.astype(BF16)
    rest = x - hi.astype(F32)
    mid = rest.astype(BF16)
    lo = (rest - mid.astype(F32)).astype(BF16)
    return (jnp.dot(tril_bf16, hi, preferred_element_type=F32)
            + jnp.dot(tril_bf16, mid, preferred_element_type=F32)
            + jnp.dot(tril_bf16, lo, preferred_element_type=F32))


def _hgrn_kernel(q_ref, f_ref, i_ref, g_ref, lbt_ref, nw_ref, s0_ref, o_ref, sout_ref, st_ref,
                 *, q, nchunk, nh):
    t = pl.program_id(2)

    @pl.when(t == 0)
    def _():
        st_ref[...] = s0_ref[0]

    hq = q // 2
    row = lax.broadcasted_iota(jnp.int32, (q, q), 0)
    col = lax.broadcasted_iota(jnp.int32, (q, q), 1)
    lower = row >= col
    tril = lower.astype(BF16)
    left_half = lax.broadcasted_iota(jnp.int32, (hq, LANES), 1) < hq
    k_pad = jnp.zeros((LANES - 3 * hq, nh * HG_K), BF16)
    tab = lbt_ref[...]
    e = jnp.exp(tab - jnp.max(tab, axis=0, keepdims=True))
    lb = e[0:1, :] / jnp.sum(e, axis=0, keepdims=True)
    norm_w = nw_ref[...]
    nt_dims = (((1,), (1,)), ((), ()))
    tn_dims = (((0,), (0,)), ((), ()))

    def decayed(x, cum_rows, ref_row, sign):
        return (x * jnp.exp(sign * (cum_rows - ref_row))).astype(BF16)

    def chunk(ci, carry):
        rows = pl.ds(pl.multiple_of(ci * q, q), q)
        qq = _silu(q_ref[rows, :]) * (HG_K ** -0.5)
        f = lb + (1.0 - lb) * _sigmoid(f_ref[rows, :])
        kk = 1.0 - f
        v = i_ref[rows, :].astype(BF16)
        gate = _silu(g_ref[rows, :])
        cum = _cumsum_rows(tril, jnp.log(f))
        top, bot = slice(0, hq), slice(hq, q)
        mid_top = cum[hq // 2 - 1:hq // 2, :]
        mid_bot = cum[hq + hq // 2 - 1:hq + hq // 2, :]
        edge = cum[hq - 1:hq, :]
        cum_last = cum[q - 1:q, :]
        q_tt = decayed(qq[top], cum[top], mid_top, 1.0)
        k_tt = decayed(kk[top], cum[top], mid_top, -1.0)
        q_bb = decayed(qq[bot], cum[bot], mid_bot, 1.0)
        k_bb = decayed(kk[bot], cum[bot], mid_bot, -1.0)
        q_bt = decayed(qq[bot], cum[bot], edge, 1.0)
        k_bt = decayed(kk[top], cum[top], edge, -1.0)
        qe = (qq * jnp.exp(cum)).astype(BF16)
        kt = (kk * jnp.exp(cum_last - cum)).astype(BF16)
        dec = jnp.exp(cum_last)
        q_stack = jnp.concatenate([q_tt, q_bb, q_bt], axis=0)
        k_stack = jnp.concatenate([k_tt, k_bb, k_bt, k_pad], axis=0)
        for h in range(nh):
            sl = slice(h * HG_K, (h + 1) * HG_K)
            s_all = lax.dot_general(q_stack[:, sl], k_stack[:, sl], nt_dims,
                                    preferred_element_type=F32)
            bottom_left = pltpu.roll(s_all[2 * hq:3 * hq, :], LANES - 2 * hq, axis=1)
            s_bot = jnp.where(left_half, bottom_left, s_all[hq:2 * hq, :])
            att = jnp.concatenate([s_all[0:hq, 0:q], s_bot[:, 0:q]], axis=0)
            att = jnp.where(lower, att, 0.0).astype(BF16)
            st = st_ref[h]
            o = jnp.dot(att, v[:, sl], preferred_element_type=F32) + lax.dot_general(
                qe[:, sl], st.astype(BF16), nt_dims, preferred_element_type=F32)
            ms = jnp.mean(o * o, axis=-1, keepdims=True)
            o_ref[rows, sl] = (o * lax.rsqrt(ms + NORM_EPS) * norm_w * gate[:, sl]).astype(o_ref.dtype)
            st_ref[h] = st * dec[:, sl] + lax.dot_general(v[:, sl], kt[:, sl], tn_dims,
                                                          preferred_element_type=F32)
        return carry

    lax.fori_loop(0, nchunk, chunk, 0)

    @pl.when(t == pl.num_programs(2) - 1)
    def _():
        sout_ref[0] = st_ref[...]


def _hgrn_scan(proj, row_off, bt, seqlen, lb_table, hg_norm_w, state0):
    q = min(CHUNK, seqlen)
    tb = _largest_divisor(seqlen, (512, 256, 128, 64, 32))
    nchunk = tb // q
    nt = seqlen // tb
    rows = bt * seqlen
    roff = row_off // tb
    nh = HGRN_HEADS_PER_STEP
    hw = nh * HG_K
    s0 = state0.transpose(0, 1, 3, 2)
    nrow = lb_table.shape[0]

    def col_spec(col0):
        return pl.BlockSpec((tb, hw), lambda b, h, t: (roff + b * nt + t, col0 // hw + h))

    o, sout = pl.pallas_call(
        functools.partial(_hgrn_kernel, q=q, nchunk=nchunk, nh=nh),
        grid=(bt, HG_HEADS // nh, nt),
        in_specs=[col_spec(COL_HQ), col_spec(COL_HF), col_spec(COL_HI), col_spec(COL_HG),
                  pl.BlockSpec((nrow, hw), lambda b, h, t: (0, h)),
                  pl.BlockSpec((1, HG_V), lambda b, h, t: (0, 0)),
                  pl.BlockSpec((1, nh, HG_V, HG_K), lambda b, h, t: (b, h, 0, 0))],
        out_specs=[pl.BlockSpec((tb, hw), lambda b, h, t: (b * nt + t, h)),
                   pl.BlockSpec((1, nh, HG_V, HG_K), lambda b, h, t: (b, h, 0, 0))],
        out_shape=[jax.ShapeDtypeStruct((rows, HG_HEADS * HG_V), BF16),
                   jax.ShapeDtypeStruct((bt, HG_HEADS, HG_V, HG_K), F32)],
        scratch_shapes=[pltpu.VMEM((nh, HG_V, HG_K), F32)],
        compiler_params=_params("parallel", "parallel", "arbitrary"),
        name="hgrn_scan",
    )(proj, proj, proj, proj, lb_table, hg_norm_w.reshape(1, HG_V), s0)
    return o, sout.transpose(0, 1, 3, 2)


def _top_values(s, count):
    vals = []
    work = s
    for _ in range(count):
        m = jnp.max(work, axis=0, keepdims=True)
        vals.append(m)
        work = jnp.where(work == m, -jnp.inf, work)
    return jnp.concatenate(vals, axis=0)


def _peer_query_kernel(w_ref, x_ref, sk_ref, thr_ref, e0_ref, s1_ref, e1_ref):
    half = PEER_QDIM // 2
    nh = thr_ref.shape[0]
    qt = jnp.dot(w_ref[...], x_ref[...], preferred_element_type=F32)
    for h in range(nh):
        q0 = qt[h * PEER_QDIM:h * PEER_QDIM + half]
        q1 = qt[h * PEER_QDIM + half:(h + 1) * PEER_QDIM]
        s0 = jnp.dot(sk_ref[0], q0, precision=HIGHEST, preferred_element_type=F32)
        s1 = jnp.dot(sk_ref[1], q1, precision=HIGHEST, preferred_element_type=F32)
        sv0 = _top_values(s0, PEER_TOPK)
        sv1 = _top_values(s1, PEER_TOPK)
        cand = [sv0[0:1] + sv1]
        for a in range(1, 8):
            cand.append(sv0[a:a + 1] + sv1[0:8])
        cand.append(sv0[8:16] + sv1[0:1])
        top = _top_values(jnp.concatenate(cand, axis=0), PEER_TOPK)
        z = jnp.sum(jnp.exp(top - top[0:1]), axis=0, keepdims=True)
        tau = top[PEER_TOPK - 1:PEER_TOPK]
        thr = jnp.full(s0.shape, jnp.inf, F32)
        for b in range(PEER_TOPK):
            thr = jnp.where(s0 + sv1[b:b + 1] >= tau, sv1[b:b + 1], thr)
        thr_ref[h] = thr
        s1_ref[h] = s1
        e0_ref[h] = jnp.exp(s0 - sv0[0:1]) / z
        e1_ref[h] = jnp.exp(s1 - sv1[0:1])


def _peer_query(w_pq_t, xn_t, sub_keys):
    d, t = xn_t.shape
    tb = _largest_divisor(t, (256, 128))
    nh = PEER_QUERY_HEADS_PER_STEP
    keyed = jax.ShapeDtypeStruct((PEER_HEADS, PEER_KEYS, t), F32)
    key_spec = pl.BlockSpec((nh, PEER_KEYS, tb), lambda i, h: (h, 0, i))
    return pl.pallas_call(
        _peer_query_kernel,
        grid=(t // tb, PEER_HEADS // nh),
        in_specs=[pl.BlockSpec((nh * PEER_QDIM, d), lambda i, h: (h, 0)),
                  pl.BlockSpec((d, tb), lambda i, h: (0, i)),
                  pl.BlockSpec((2, PEER_KEYS, PEER_QDIM // 2), lambda i, h: (0, 0, 0))],
        out_specs=[key_spec, key_spec, key_spec, key_spec],
        out_shape=[keyed, keyed, keyed, keyed],
        compiler_params=_params("parallel", "arbitrary"),
        name="peer_query",
    )(w_pq_t, xn_t, sub_keys)


def _gelu_tanh(x):
    return 0.5 * x * (1.0 + jnp.tanh(0.7978845608028654 * (x + 0.044715 * (x * x * x))))


def _peer_expert_kernel(x_ref, u_ref, v_ref, thr_ref, e0_ref, s1_ref, e1_ref, o_ref, w_ref, *, nb, tb):
    e = pl.program_id(1)

    @pl.when(e == 0)
    def _():
        o_ref[...] = jnp.zeros_like(o_ref)

    act = _gelu_tanh(jnp.dot(u_ref[...], x_ref[...], preferred_element_type=F32))
    for i in range(nb):
        for tl in range(tb // LANES):
            lanes = pl.ds(tl * LANES, LANES)
            acc = jnp.zeros((PEER_KEYS, LANES), F32)
            for h in range(PEER_HEADS):
                thr = thr_ref[h, 0, i:i + 1, lanes]
                c0 = e0_ref[h, 0, i:i + 1, lanes]
                acc = acc + jnp.where(s1_ref[h, :, lanes] >= thr, e1_ref[h, :, lanes] * c0, 0.0)
            w_ref[i * PEER_KEYS:(i + 1) * PEER_KEYS, lanes] = acc
    g = (w_ref[...] * act).astype(BF16)
    o_ref[...] += lax.dot_general(g, v_ref[...], (((0,), (0,)), ((), ())),
                                  preferred_element_type=F32)


def _peer_experts(xn_t, u, v, thr, e0, s1, e1):
    d, t = xn_t.shape
    n_exp = u.shape[0]
    tb = _largest_divisor(t, (512, 256, 128))
    nb = 4
    eb = nb * PEER_KEYS
    once = pl.Buffered(1)
    keyed = pl.BlockSpec((PEER_HEADS, PEER_KEYS, tb), lambda i, e: (0, 0, i), pipeline_mode=once)
    rows = pl.BlockSpec((PEER_HEADS, 1, nb, tb), lambda i, e: (0, e, 0, i))
    thr = thr.reshape(PEER_HEADS, PEER_KEYS // nb, nb, t)
    e0 = e0.reshape(PEER_HEADS, PEER_KEYS // nb, nb, t)
    return pl.pallas_call(
        functools.partial(_peer_expert_kernel, nb=nb, tb=tb),
        grid=(t // tb, n_exp // eb),
        in_specs=[pl.BlockSpec((d, tb), lambda i, e: (0, i), pipeline_mode=once),
                  pl.BlockSpec((eb, d), lambda i, e: (e, 0)),
                  pl.BlockSpec((eb, d), lambda i, e: (e, 0)),
                  rows, rows, keyed, keyed],
        out_specs=pl.BlockSpec((tb, d), lambda i, e: (i, 0)),
        out_shape=jax.ShapeDtypeStruct((t, d), F32),
        scratch_shapes=[pltpu.VMEM((eb, tb), F32)],
        compiler_params=_params("parallel", "arbitrary"),
        name="peer_experts",
    )(xn_t, u, v, thr, e0, s1, e1)


def _branches(proj, dtx, dt, row_off, bt, seqlen, conv_buf, s_ssd, s_hg, conv_w, conv_b, a_log, d_skip,
              ssd_norm_w, lb_table, hg_norm_w):
    xconv, new_conv = _conv_silu(proj, row_off, bt, seqlen, conv_buf, conv_w, conv_b)
    y_ssd, s_ssd_new = _ssd_scan(xconv, proj, dtx, dt, row_off, bt, seqlen,
                                 a_log, d_skip, ssd_norm_w, s_ssd)
    o_hg, s_hg_new = _hgrn_scan(proj, row_off, bt, seqlen, lb_table, hg_norm_w, s_hg)
    return y_ssd, o_hg, s_ssd_new, new_conv, s_hg_new


def kernel(x_prompt, x_sample, state_ssd, cache_ssd_conv, state_hgrn, lb_table, norm1_w, w_in, conv_w, conv_b, dt_bias, a_log, d_skip, ssd_norm_w, w_ssd_out, hg_norm_w, w_hg_out, w_o, norm2_w, w_pq, sub_keys, u_experts, v_experts, final_norm_w):
    assert w_in.shape[0] == 1 and lb_table.shape[0] == 2, "single-layer stack"
    bp, lp, d = x_prompt.shape
    bs, ls, _ = x_sample.shape
    tp, ts = bp * lp, bs * ls
    x_parts = (x_prompt.reshape(tp, d), x_sample.reshape(ts, d))

    w_in_t = w_in[0].T
    w_dt = jnp.pad(w_in_t[DT_COL:DT_COL + SSD_HEADS], ((0, LANES - SSD_HEADS), (0, 0))).astype(BF16)
    b_dt = jnp.pad(dt_bias[0], (0, LANES - SSD_HEADS)).reshape(1, LANES)

    xn = _rmsnorm_stacked(*x_parts, norm1_w[0], BF16)
    proj, u_bf16, v_bf16 = _in_proj(xn, w_in_t, u_experts[0], v_experts[0])
    dt, dtx = _dt_proj(xn, w_dt, b_dt)

    zeros = functools.partial(jnp.zeros, dtype=F32)
    common = (conv_w[0], conv_b[0], a_log[0], d_skip[0], ssd_norm_w[0], lb_table, hg_norm_w[0])
    ya_p, yb_p, ssd_p, conv_p, hg_p = _branches(
        proj, dtx, dt, 0, bp, lp,
        zeros((bp, SSD_CONV - 1, SSD_CONV_DIM)), zeros((bp, SSD_HEADS, SSD_HEADDIM, SSD_STATE)),
        zeros((bp, HG_HEADS, HG_K, HG_V)), *common)
    ya_s, yb_s, ssd_s, conv_s, hg_s = _branches(
        proj, dtx, dt, tp, bs, ls, cache_ssd_conv[0], state_ssd[0], state_hgrn[0], *common)

    merged = _gated_merge((ya_p, ya_s), (yb_p, yb_s), w_ssd_out[0].astype(BF16),
                          w_hg_out[0].astype(BF16), proj)
    h = _residual_matmul(merged, w_o[0].astype(BF16), x_parts)

    hn_t = _rmsnorm_t(h, norm2_w[0], BF16)
    thr, e0, s1, e1 = _peer_query(w_pq[0].T.astype(BF16), hn_t, sub_keys[0])
    peer = _peer_experts(hn_t, u_bf16, v_bf16, thr, e0, s1, e1)
    y_prompt = _add_rmsnorm(h, peer, final_norm_w, 0, tp).reshape(bp, lp, d)
    y_sample = _add_rmsnorm(h, peer, final_norm_w, tp, ts).reshape(bs, ls, d)
    return (y_prompt, y_sample, ssd_p[None], conv_p[None], hg_p[None],
            ssd_s[None], conv_s[None], hg_s[None])
```

```python
import functools

import jax
import jax.numpy as jnp
from jax import lax
from jax.experimental import pallas as pl
from jax.experimental.pallas import tpu as pltpu

F32 = jnp.float32
BF16 = jnp.bfloat16

NORM_EPS = 1e-6
LANES = 128
VMEM_LIMIT_BYTES = 56 * 1024 * 1024
IN_PROJ_VMEM_LIMIT_BYTES = 62 * 1024 * 1024

D_MODEL = 4096
SSD_GROUPS = 8
SSD_HEADS_PER_GROUP = 8
SSD_HEADDIM = 64
SSD_STATE = 128
SSD_HEADS = SSD_GROUPS * SSD_HEADS_PER_GROUP
SSD_INNER = SSD_HEADS * SSD_HEADDIM
SSD_GROUP_WIDTH = SSD_HEADS_PER_GROUP * SSD_HEADDIM
SSD_CONV = 4
SSD_CONV_DIM = SSD_INNER + 2 * SSD_GROUPS * SSD_STATE
HG_HEADS = 32
HG_K = 128
HG_V = 128
CHUNK = 64
PEER_HEADS = 8
PEER_KEYS = 128
PEER_TOPK = 16
PEER_QDIM = 256
PEER_QUERY_HEADS_PER_STEP = 4
SSD_GROUPS_PER_STEP = 4
HGRN_HEADS_PER_STEP = 8

COL_Z = 0
COL_XBC = SSD_INNER
COL_HQ = COL_XBC + SSD_CONV_DIM
COL_HF = COL_HQ + HG_HEADS * HG_K
COL_HI = COL_HF + HG_HEADS * HG_K
COL_HG = COL_HI + HG_HEADS * HG_V
COL_GA = COL_HG + HG_HEADS * HG_V
COL_GB = COL_GA + D_MODEL
PROJ_WIDTH = COL_GB + D_MODEL
DT_COL = SSD_INNER + SSD_CONV_DIM


def _largest_divisor(n, candidates):
    for c in candidates:
        if n % c == 0:
            return c
    raise ValueError(f"no block size among {candidates} divides {n}")


def _params(*semantics):
    return pltpu.CompilerParams(dimension_semantics=semantics,
                                vmem_limit_bytes=VMEM_LIMIT_BYTES)


def _sigmoid(x):
    return 1.0 / (1.0 + jnp.exp(-x))


def _silu(x):
    return x * _sigmoid(x)


def _stacked_specs(block, n_first, row_pos, col_of=None):
    def first(*idx):
        return (jnp.minimum(idx[row_pos], n_first - 1), 0 if col_of is None else col_of(*idx))

    def second(*idx):
        return (jnp.maximum(idx[row_pos] - n_first, 0), 0 if col_of is None else col_of(*idx))

    return pl.BlockSpec(block, first), pl.BlockSpec(block, second)


def _stacked_block(i, n_first, first_ref, second_ref):
    return jnp.where(i < n_first, first_ref[...], second_ref[...])


def _row_block(n_first_rows, n_second_rows, candidates):
    for c in candidates:
        if n_first_rows % c == 0 and n_second_rows % c == 0:
            return c
    raise ValueError(f"no block size among {candidates} divides {n_first_rows} and {n_second_rows}")


def _rmsnorm_stacked_kernel(xa_ref, xb_ref, w_ref, o_ref, *, n_first):
    x = _stacked_block(pl.program_id(0), n_first, xa_ref, xb_ref)
    ms = jnp.mean(x * x, axis=-1, keepdims=True)
    o_ref[...] = (x * lax.rsqrt(ms + NORM_EPS) * w_ref[...]).astype(o_ref.dtype)


def _rmsnorm_stacked(xa, xb, w, out_dtype):
    (ta, d), tb_rows = xa.shape, xb.shape[0]
    bm = _row_block(ta, tb_rows, (256, 128, 64, 32, 16, 8))
    n_first = ta // bm
    spec_a, spec_b = _stacked_specs((bm, d), n_first, 0)
    return pl.pallas_call(
        functools.partial(_rmsnorm_stacked_kernel, n_first=n_first),
        grid=((ta + tb_rows) // bm,),
        in_specs=[spec_a, spec_b, pl.BlockSpec((1, d), lambda i: (0, 0))],
        out_specs=pl.BlockSpec((bm, d), lambda i: (i, 0)),
        out_shape=jax.ShapeDtypeStruct((ta + tb_rows, d), out_dtype),
        compiler_params=_params("parallel"),
        name="rmsnorm_in",
    )(xa, xb, w.reshape(1, d))


def _rmsnorm_t_kernel(x_ref, w_ref, o_ref):
    x = x_ref[...]
    ms = jnp.mean(x * x, axis=-1, keepdims=True)
    o_ref[...] = jnp.transpose(x * lax.rsqrt(ms + NORM_EPS) * w_ref[...]).astype(o_ref.dtype)


def _rmsnorm_t(x, w, out_dtype):
    t, d = x.shape
    bm = _largest_divisor(t, (256, 128))
    return pl.pallas_call(
        _rmsnorm_t_kernel,
        grid=(t // bm,),
        in_specs=[pl.BlockSpec((bm, d), lambda i: (i, 0)),
                  pl.BlockSpec((1, d), lambda i: (0, 0))],
        out_specs=pl.BlockSpec((d, bm), lambda i: (0, i)),
        out_shape=jax.ShapeDtypeStruct((d, t), out_dtype),
        compiler_params=_params("parallel"),
        name="rmsnorm_t",
    )(x, w.reshape(1, d))


def _add_rmsnorm_kernel(a_ref, b_ref, w_ref, o_ref):
    x = a_ref[...] + b_ref[...]
    ms = jnp.mean(x * x, axis=-1, keepdims=True)
    o_ref[...] = x * lax.rsqrt(ms + NORM_EPS) * w_ref[...]


def _add_rmsnorm(a, b, w, row_off, rows):
    d = a.shape[1]
    bm = _row_block(rows, row_off, (256, 128, 64, 32, 16, 8))
    off = row_off // bm
    return pl.pallas_call(
        _add_rmsnorm_kernel,
        grid=(rows // bm,),
        in_specs=[pl.BlockSpec((bm, d), lambda i: (off + i, 0)),
                  pl.BlockSpec((bm, d), lambda i: (off + i, 0)),
                  pl.BlockSpec((1, d), lambda i: (0, 0))],
        out_specs=pl.BlockSpec((bm, d), lambda i: (i, 0)),
        out_shape=jax.ShapeDtypeStruct((rows, d), F32),
        compiler_params=_params("parallel"),
        name="add_rmsnorm",
    )(a, b, w.reshape(1, d))


def _in_proj_kernel(x_ref, wa_ref, wb_ref, ta_ref, tb_ref, o_ref, ta_out_ref, tb_out_ref, wbuf_ref,
                    *, n_plain, shift, chunk):
    ta_out_ref[...] = ta_ref[...].astype(ta_out_ref.dtype)
    tb_out_ref[...] = tb_ref[...].astype(tb_out_ref.dtype)
    j = pl.program_id(0)
    i = pl.program_id(1)
    bn = wa_ref.shape[0]

    @pl.when(jnp.logical_and(i == 0, j < n_plain))
    def _():
        def body(c, carry):
            rows = pl.ds(pl.multiple_of(c * chunk, chunk), chunk)
            wbuf_ref[rows, :] = wa_ref[rows, :].astype(BF16)
            return carry
        lax.fori_loop(0, bn // chunk, body, 0)

    @pl.when(jnp.logical_and(i == 0, j >= n_plain))
    def _():
        def body(c, carry):
            dst = pl.ds(pl.multiple_of(c * chunk, chunk), chunk)
            src = pl.ds(pl.multiple_of(c * chunk + shift, shift), chunk)
            wbuf_ref[dst, :] = wa_ref[src, :].astype(BF16)
            return carry
        lax.fori_loop(0, (bn - shift) // chunk, body, 0)
        wbuf_ref[bn - shift:bn, :] = wb_ref[...].astype(BF16)

    o_ref[...] = lax.dot_general(x_ref[...], wbuf_ref[...], (((1,), (1,)), ((), ())),
                                 preferred_element_type=F32)


def _in_proj(x, w_t, table_a, table_b):
    m, k = x.shape
    bm = _largest_divisor(m, (512, 256, 128, 64, 32, 16, 8))
    bn = 1024
    shift = SSD_HEADS
    assert DT_COL % bn == 0 and PROJ_WIDTH % bn == 0 and bn % shift == 0
    n_i = m // bm
    steps = (PROJ_WIDTH // bn) * n_i
    rows, width = table_a.shape
    assert table_b.shape == table_a.shape
    slab = next(r for r in range(16, rows + 1, 16) if rows % r == 0 and rows // r <= steps)
    last = rows // slab - 1
    slab_spec = pl.BlockSpec((slab, width), lambda j, i: (jnp.minimum(j * n_i + i, last), 0))
    table_bf16 = jax.ShapeDtypeStruct((rows, width), BF16)
    return pl.pallas_call(
        functools.partial(_in_proj_kernel, n_plain=DT_COL // bn, shift=shift, chunk=shift),
        grid=(PROJ_WIDTH // bn, n_i),
        in_specs=[pl.BlockSpec((bm, k), lambda j, i: (i, 0)),
                  pl.BlockSpec((bn, k), lambda j, i: (j, 0)),
                  pl.BlockSpec((shift, k), lambda j, i: ((bn // shift) * (j + 1), 0)),
                  slab_spec, slab_spec],
        out_specs=[pl.BlockSpec((bm, bn), lambda j, i: (i, j)), slab_spec, slab_spec],
        out_shape=[jax.ShapeDtypeStruct((m, PROJ_WIDTH), F32), table_bf16, table_bf16],
        scratch_shapes=[pltpu.VMEM((bn, k), BF16)],
        compiler_params=pltpu.CompilerParams(dimension_semantics=("arbitrary", "arbitrary"),
                                             vmem_limit_bytes=IN_PROJ_VMEM_LIMIT_BYTES),
        name="in_proj",
    )(x, w_t, w_t, table_a, table_b)


def _split3_dot(x, rhs_bf16):
    hi = x.astype(BF16)
    rest = x - hi.astype(F32)
    mid = rest.astype(BF16)
    lo = (rest - mid.astype(F32)).astype(BF16)
    return (jnp.dot(hi, rhs_bf16, preferred_element_type=F32)
            + jnp.dot(mid, rhs_bf16, preferred_element_type=F32)
            + jnp.dot(lo, rhs_bf16, preferred_element_type=F32))


def _dt_kernel(x_ref, w_ref, b_ref, ex_ref, dt_ref, dtx_ref):
    a = lax.dot_general(x_ref[...], w_ref[...], (((1,), (1,)), ((), ())),
                        preferred_element_type=F32) + b_ref[...]
    dt = jnp.maximum(a, 0.0) + jnp.log1p(jnp.exp(-jnp.abs(a)))
    dt_ref[...] = dt
    dtx_ref[...] = _split3_dot(dt, ex_ref[...])


def _dt_proj(x, w_t, b):
    m, k = x.shape
    n = w_t.shape[0]
    bm = _largest_divisor(m, (512, 256, 128, 64, 32, 16, 8))
    expand = (jnp.arange(SSD_INNER)[None, :] // SSD_HEADDIM == jnp.arange(n)[:, None]).astype(BF16)
    return pl.pallas_call(
        _dt_kernel,
        grid=(m // bm,),
        in_specs=[pl.BlockSpec((bm, k), lambda i: (i, 0)),
                  pl.BlockSpec((n, k), lambda i: (0, 0)),
                  pl.BlockSpec((1, n), lambda i: (0, 0)),
                  pl.BlockSpec((n, SSD_INNER), lambda i: (0, 0))],
        out_specs=[pl.BlockSpec((bm, n), lambda i: (i, 0)),
                   pl.BlockSpec((bm, SSD_INNER), lambda i: (i, 0))],
        out_shape=[jax.ShapeDtypeStruct((m, n), F32),
                   jax.ShapeDtypeStruct((m, SSD_INNER), F32)],
        compiler_params=_params("parallel"),
        name="dt_proj",
    )(x, w_t, b, expand)


def _merge_kernel(ya1_ref, ya2_ref, yb1_ref, yb2_ref, wa_ref, wb_ref, ga_ref, gb_ref, o_ref, *, n_first):
    i = pl.program_id(1)
    ya = _stacked_block(i, n_first, ya1_ref, ya2_ref)
    yb = _stacked_block(i, n_first, yb1_ref, yb2_ref)
    pa = jnp.dot(ya, wa_ref[...], preferred_element_type=F32)
    pb = jnp.dot(yb, wb_ref[...], preferred_element_type=F32)
    o_ref[...] = (_sigmoid(ga_ref[...]) * pa + _sigmoid(gb_ref[...]) * pb).astype(o_ref.dtype)


def _gated_merge(ya_parts, yb_parts, wa, wb, proj):
    (m1, k), m2 = ya_parts[0].shape, ya_parts[1].shape[0]
    n = wa.shape[1]
    bm = _row_block(m1, m2, (512, 256, 128, 64, 32, 16, 8))
    bn = 512
    n_first = m1 // bm
    y_first, y_second = _stacked_specs((bm, k), n_first, 1)
    return pl.pallas_call(
        functools.partial(_merge_kernel, n_first=n_first),
        grid=(n // bn, (m1 + m2) // bm),
        in_specs=[y_first, y_second, y_first, y_second,
                  pl.BlockSpec((k, bn), lambda j, i: (0, j)),
                  pl.BlockSpec((k, bn), lambda j, i: (0, j)),
                  pl.BlockSpec((bm, bn), lambda j, i: (i, COL_GA // bn + j)),
                  pl.BlockSpec((bm, bn), lambda j, i: (i, COL_GB // bn + j))],
        out_specs=pl.BlockSpec((bm, bn), lambda j, i: (i, j)),
        out_shape=jax.ShapeDtypeStruct((m1 + m2, n), BF16),
        compiler_params=_params("parallel", "parallel"),
        name="gated_merge",
    )(ya_parts[0], ya_parts[1], yb_parts[0], yb_parts[1], wa, wb, proj, proj)


def _residual_matmul_kernel(x_ref, w_ref, r1_ref, r2_ref, o_ref, *, n_first):
    r = _stacked_block(pl.program_id(1), n_first, r1_ref, r2_ref)
    o_ref[...] = r + jnp.dot(x_ref[...], w_ref[...], preferred_element_type=F32)


def _residual_matmul(x, w, r_parts):
    m, k = x.shape
    n = w.shape[1]
    m1, m2 = r_parts[0].shape[0], r_parts[1].shape[0]
    bm = _row_block(m1, m2, (512, 256, 128, 64, 32, 16, 8))
    bn = 1024
    n_first = m1 // bm
    r_first, r_second = _stacked_specs((bm, bn), n_first, 1, col_of=lambda j, i: j)
    return pl.pallas_call(
        functools.partial(_residual_matmul_kernel, n_first=n_first),
        grid=(n // bn, m // bm),
        in_specs=[pl.BlockSpec((bm, k), lambda j, i: (i, 0)),
                  pl.BlockSpec((k, bn), lambda j, i: (0, j)),
                  r_first, r_second],
        out_specs=pl.BlockSpec((bm, bn), lambda j, i: (i, j)),
        out_shape=jax.ShapeDtypeStruct((m, n), F32),
        compiler_params=_params("parallel", "parallel"),
        name="residual_matmul",
    )(x, w, r_parts[0], r_parts[1])


CONV_PAD = 8


def _conv_kernel(u_ref, buf_ref, w_ref, b_ref, o_ref, cache_ref, prev_ref, *, tb):
    t = pl.program_id(2)
    lo = CONV_PAD - (SSD_CONV - 1)

    @pl.when(t == 0)
    def _():
        prev_ref[lo:CONV_PAD, :] = buf_ref[0]

    u = u_ref[...]
    first_row = lax.broadcasted_iota(jnp.int32, (CONV_PAD, u.shape[1]), 0) == 0
    acc = b_ref[...] + u * w_ref[SSD_CONV - 1:SSD_CONV, :]
    shifted = u
    for k in range(1, SSD_CONV):
        shifted = pltpu.roll(shifted, 1, axis=0)
        head = jnp.where(first_row, prev_ref[CONV_PAD - k:CONV_PAD - k + 1, :], shifted[0:CONV_PAD])
        shifted = jnp.concatenate([head, shifted[CONV_PAD:]], axis=0)
        acc = acc + shifted * w_ref[SSD_CONV - 1 - k:SSD_CONV - k, :]
    o_ref[...] = _silu(acc)
    prev_ref[...] = u[tb - CONV_PAD:tb]

    @pl.when(t == pl.num_programs(2) - 1)
    def _():
        cache_ref[0] = prev_ref[lo:CONV_PAD, :]


def _conv_silu(proj, row_off, bt, seqlen, buf, conv_w, conv_b):
    tb = _largest_divisor(seqlen, (512, 256, 128, 64, 32))
    cb = 2048
    nt = seqlen // tb
    roff = row_off // tb
    coff = COL_XBC // cb
    return pl.pallas_call(
        functools.partial(_conv_kernel, tb=tb),
        grid=(SSD_CONV_DIM // cb, bt, nt),
        in_specs=[pl.BlockSpec((tb, cb), lambda c, b, t: (roff + b * nt + t, coff + c)),
                  pl.BlockSpec((1, SSD_CONV - 1, cb), lambda c, b, t: (b, 0, c)),
                  pl.BlockSpec((SSD_CONV, cb), lambda c, b, t: (0, c)),
                  pl.BlockSpec((1, cb), lambda c, b, t: (0, c))],
        out_specs=[pl.BlockSpec((tb, cb), lambda c, b, t: (b * nt + t, c)),
                   pl.BlockSpec((1, SSD_CONV - 1, cb), lambda c, b, t: (b, 0, c))],
        out_shape=[jax.ShapeDtypeStruct((bt * seqlen, SSD_CONV_DIM), F32),
                   jax.ShapeDtypeStruct((bt, SSD_CONV - 1, SSD_CONV_DIM), F32)],
        scratch_shapes=[pltpu.VMEM((CONV_PAD, cb), F32)],
        compiler_params=_params("parallel", "parallel", "arbitrary"),
        name="conv_silu",
    )(proj, buf, conv_w, conv_b.reshape(1, SSD_CONV_DIM))


def _ssd_kernel(x_ref, b_ref, c_ref, z_ref, dtx_ref, dtt_ref, alx_ref, alt_ref, dsk_ref, nw_ref,
                s0_ref, y_ref, sout_ref, st_ref, *, q, nchunk, ng):
    t = pl.program_id(2)
    p, r, gw, n = SSD_HEADDIM, SSD_HEADS_PER_GROUP, SSD_GROUP_WIDTH, SSD_STATE

    @pl.when(t == 0)
    def _():
        st_ref[...] = s0_ref[0]

    row = lax.broadcasted_iota(jnp.int32, (q, q), 0)
    col = lax.broadcasted_iota(jnp.int32, (q, q), 1)
    lower = row >= col
    tril = lower.astype(BF16)
    triu = (row <= col).astype(BF16)
    a_x = -jnp.exp(alx_ref[...])
    a_t = -jnp.exp(alt_ref[...]).reshape(ng * r, 1)
    d_skip = dsk_ref[...]
    norm_w = nw_ref[...]
    nt_dims = (((1,), (1,)), ((), ()))
    tn_dims = (((0,), (0,)), ((), ()))

    def chunk(ci, carry):
        rows = pl.ds(pl.multiple_of(ci * q, q), q)
        x = x_ref[rows, :]
        dtx = dtx_ref[rows, :]
        cum = _cumsum_rows(tril, dtx * a_x)
        cum_t = _split3_dot(dtt_ref[:, ci].reshape(ng * r, q) * a_t, triu)
        cum_last = cum[q - 1:q, :]
        decay_in = jnp.exp(cum)
        xdt = (x * dtx).astype(BF16)
        xt = (x * (jnp.exp(cum_last - cum) * dtx)).astype(BF16)
        decay_st = jnp.exp(cum_last)
        gate = _silu(z_ref[rows, :])
        skip = d_skip * x
        for gi in range(ng):
            cols = slice(gi * gw, (gi + 1) * gw)
            bm = b_ref[rows, gi * n:(gi + 1) * n].astype(BF16)
            cm = c_ref[rows, gi * n:(gi + 1) * n].astype(BF16)
            cb = lax.dot_general(cm, bm, nt_dims, preferred_element_type=F32)
            st = st_ref[gi]
            y = jnp.dot(cm, st.astype(BF16), preferred_element_type=F32) * decay_in[:, cols]
            parts = []
            for h in range(r):
                c0 = gi * gw + h * p
                seg = cum[:, c0:c0 + q] - cum_t[gi * r + h:gi * r + h + 1, :]
                mix = (cb * jnp.where(lower, jnp.exp(seg), 0.0)).astype(BF16)
                parts.append(jnp.dot(mix, xdt[:, c0:c0 + p], preferred_element_type=F32))
            y = (y + jnp.concatenate(parts, axis=1) + skip[:, cols]) * gate[:, cols]
            ms = jnp.mean(y * y, axis=-1, keepdims=True)
            y_ref[rows, cols] = (y * lax.rsqrt(ms + NORM_EPS) * norm_w[:, cols]).astype(y_ref.dtype)
            st_ref[gi] = st * decay_st[:, cols] + lax.dot_general(
                bm, xt[:, cols], tn_dims, preferred_element_type=F32)
        return carry

    lax.fori_loop(0, nchunk, chunk, 0)

    @pl.when(t == pl.num_programs(2) - 1)
    def _():
        sout_ref[0] = st_ref[...]


def _ssd_scan(xconv, proj, dtx, dt, row_off, bt, seqlen, a_log, d_skip, ssd_norm_w, state0):
    g, r, p, n, gw = SSD_GROUPS, SSD_HEADS_PER_GROUP, SSD_HEADDIM, SSD_STATE, SSD_GROUP_WIDTH
    ng = SSD_GROUPS_PER_STEP
    q = min(CHUNK, seqlen)
    tb = _largest_divisor(seqlen, (512, 256, 128, 64, 32))
    nchunk = tb // q
    nt = seqlen // tb
    rows = bt * seqlen
    roff = row_off // tb
    dtt = dt[row_off:row_off + rows, :SSD_HEADS].reshape(rows // q, q, g, r).transpose(2, 0, 3, 1)
    alx = jnp.repeat(a_log, p).reshape(1, SSD_INNER)
    alt = a_log.reshape(g, r, 1)
    dsk = jnp.repeat(d_skip, p).reshape(1, SSD_INNER)
    s0 = state0.reshape(bt, g, r, p, n).transpose(0, 1, 4, 2, 3).reshape(bt, g, n, gw)
    bcol = SSD_INNER // (ng * n)
    ccol = bcol + g // ng
    wide = ng * gw
    y, sout = pl.pallas_call(
        functools.partial(_ssd_kernel, q=q, nchunk=nchunk, ng=ng),
        grid=(bt, g // ng, nt),
        in_specs=[pl.BlockSpec((tb, wide), lambda b, gi, t: (b * nt + t, gi)),
                  pl.BlockSpec((tb, ng * n), lambda b, gi, t: (b * nt + t, bcol + gi)),
                  pl.BlockSpec((tb, ng * n), lambda b, gi, t: (b * nt + t, ccol + gi)),
                  pl.BlockSpec((tb, wide), lambda b, gi, t: (roff + b * nt + t, COL_Z // wide + gi)),
                  pl.BlockSpec((tb, wide), lambda b, gi, t: (roff + b * nt + t, gi)),
                  pl.BlockSpec((ng, nchunk, r, q), lambda b, gi, t: (gi, b * nt + t, 0, 0)),
                  pl.BlockSpec((1, wide), lambda b, gi, t: (0, gi)),
                  pl.BlockSpec((ng, r, 1), lambda b, gi, t: (gi, 0, 0)),
                  pl.BlockSpec((1, wide), lambda b, gi, t: (0, gi)),
                  pl.BlockSpec((1, wide), lambda b, gi, t: (0, gi)),
                  pl.BlockSpec((1, ng, n, gw), lambda b, gi, t: (b, gi, 0, 0))],
        out_specs=[pl.BlockSpec((tb, wide), lambda b, gi, t: (b * nt + t, gi)),
                   pl.BlockSpec((1, ng, n, gw), lambda b, gi, t: (b, gi, 0, 0))],
        out_shape=[jax.ShapeDtypeStruct((rows, SSD_INNER), BF16),
                   jax.ShapeDtypeStruct((bt, g, n, gw), F32)],
        scratch_shapes=[pltpu.VMEM((ng, n, gw), F32)],
        compiler_params=_params("parallel", "parallel", "arbitrary"),
        name="ssd_scan",
    )(xconv, xconv, xconv, proj, dtx, dtt, alx, alt, dsk, ssd_norm_w.reshape(1, SSD_INNER), s0)
    state = sout.reshape(bt, g, n, r, p).transpose(0, 1, 3, 4, 2).reshape(bt, SSD_HEADS, p, n)
    return y, state


def _cumsum_rows(tril_bf16, x):
    hi = x.astype(BF16)
    rest = x - hi.astype(F32)
    mid = rest.astype(BF16)
    lo = (rest - mid.astype(F32)).astype(BF16)
    return (jnp.dot(tril_bf16, hi, preferred_element_type=F32)
            + jnp.dot(tril_bf16, mid, preferred_element_type=F32)
            + jnp.dot(tril_bf16, lo, preferred_element_type=F32))


def _hgrn_kernel(q_ref, f_ref, i_ref, g_ref, lbt_ref, nw_ref, s0_ref, o_ref, sout_ref, st_ref,
                 *, q, nchunk, nh):
    t = pl.program_id(2)

    @pl.when(t == 0)
    def _():
        st_ref[...] = s0_ref[0]

    hq = q // 2
    row = lax.broadcasted_iota(jnp.int32, (q, q), 0)
    col = lax.broadcasted_iota(jnp.int32, (q, q), 1)
    lower = row >= col
    tril = lower.astype(BF16)
    left_half = lax.broadcasted_iota(jnp.int32, (hq, LANES), 1) < hq
    k_pad = jnp.zeros((LANES - 3 * hq, nh * HG_K), BF16)
    tab = lbt_ref[...]
    e = jnp.exp(tab - jnp.max(tab, axis=0, keepdims=True))
    lb = e[0:1, :] / jnp.sum(e, axis=0, keepdims=True)
    norm_w = nw_ref[...]
    nt_dims = (((1,), (1,)), ((), ()))
    tn_dims = (((0,), (0,)), ((), ()))

    def decayed(x, cum_rows, ref_row, sign):
        return (x * jnp.exp(sign * (cum_rows - ref_row))).astype(BF16)

    def chunk(ci, carry):
        rows = pl.ds(pl.multiple_of(ci * q, q), q)
        qq = _silu(q_ref[rows, :]) * (HG_K ** -0.5)
        f = lb + (1.0 - lb) * _sigmoid(f_ref[rows, :])
        kk = 1.0 - f
        v = i_ref[rows, :].astype(BF16)
        gate = _silu(g_ref[rows, :])
        cum = _cumsum_rows(tril, jnp.log(f))
        top, bot = slice(0, hq), slice(hq, q)
        mid_top = cum[hq // 2 - 1:hq // 2, :]
        mid_bot = cum[hq + hq // 2 - 1:hq + hq // 2, :]
        edge = cum[hq - 1:hq, :]
        cum_last = cum[q - 1:q, :]
        q_tt = decayed(qq[top], cum[top], mid_top, 1.0)
        k_tt = decayed(kk[top], cum[top], mid_top, -1.0)
        q_bb = decayed(qq[bot], cum[bot], mid_bot, 1.0)
        k_bb = decayed(kk[bot], cum[bot], mid_bot, -1.0)
        q_bt = decayed(qq[bot], cum[bot], edge, 1.0)
        k_bt = decayed(kk[top], cum[top], edge, -1.0)
        qe = (qq * jnp.exp(cum)).astype(BF16)
        kt = (kk * jnp.exp(cum_last - cum)).astype(BF16)
        dec = jnp.exp(cum_last)
        q_stack = jnp.concatenate([q_tt, q_bb, q_bt], axis=0)
        k_stack = jnp.concatenate([k_tt, k_bb, k_bt, k_pad], axis=0)
        for h in range(nh):
            sl = slice(h * HG_K, (h + 1) * HG_K)
            s_all = lax.dot_general(q_stack[:, sl], k_stack[:, sl], nt_dims,
                                    preferred_element_type=F32)
            bottom_left = pltpu.roll(s_all[2 * hq:3 * hq, :], LANES - 2 * hq, axis=1)
            s_bot = jnp.where(left_half, bottom_left, s_all[hq:2 * hq, :])
            att = jnp.concatenate([s_all[0:hq, 0:q], s_bot[:, 0:q]], axis=0)
            att = jnp.where(lower, att, 0.0).astype(BF16)
            st = st_ref[h]
            o = jnp.dot(att, v[:, sl], preferred_element_type=F32) + lax.dot_general(
                qe[:, sl], st.astype(BF16), nt_dims, preferred_element_type=F32)
            ms = jnp.mean(o * o, axis=-1, keepdims=True)
            o_ref[rows, sl] = (o * lax.rsqrt(ms + NORM_EPS) * norm_w * gate[:, sl]).astype(o_ref.dtype)
            st_ref[h] = st * dec[:, sl] + lax.dot_general(v[:, sl], kt[:, sl], tn_dims,
                                                          preferred_element_type=F32)
        return carry

    lax.fori_loop(0, nchunk, chunk, 0)

    @pl.when(t == pl.num_programs(2) - 1)
    def _():
        sout_ref[0] = st_ref[...]


def _hgrn_scan(proj, row_off, bt, seqlen, lb_table, hg_norm_w, state0):
    q = min(CHUNK, seqlen)
    tb = _largest_divisor(seqlen, (512, 256, 128, 64, 32))
    nchunk = tb // q
    nt = seqlen // tb
    rows = bt * seqlen
    roff = row_off // tb
    nh = HGRN_HEADS_PER_STEP
    hw = nh * HG_K
    s0 = state0.transpose(0, 1, 3, 2)
    nrow = lb_table.shape[0]

    def col_spec(col0):
        return pl.BlockSpec((tb, hw), lambda b, h, t: (roff + b * nt + t, col0 // hw + h))

    o, sout = pl.pallas_call(
        functools.partial(_hgrn_kernel, q=q, nchunk=nchunk, nh=nh),
        grid=(bt, HG_HEADS // nh, nt),
        in_specs=[col_spec(COL_HQ), col_spec(COL_HF), col_spec(COL_HI), col_spec(COL_HG),
                  pl.BlockSpec((nrow, hw), lambda b, h, t: (0, h)),
                  pl.BlockSpec((1, HG_V), lambda b, h, t: (0, 0)),
                  pl.BlockSpec((1, nh, HG_V, HG_K), lambda b, h, t: (b, h, 0, 0))],
        out_specs=[pl.BlockSpec((tb, hw), lambda b, h, t: (b * nt + t, h)),
                   pl.BlockSpec((1, nh, HG_V, HG_K), lambda b, h, t: (b, h, 0, 0))],
        out_shape=[jax.ShapeDtypeStruct((rows, HG_HEADS * HG_V), BF16),
                   jax.ShapeDtypeStruct((bt, HG_HEADS, HG_V, HG_K), F32)],
        scratch_shapes=[pltpu.VMEM((nh, HG_V, HG_K), F32)],
        compiler_params=_params("parallel", "parallel", "arbitrary"),
        name="hgrn_scan",
    )(proj, proj, proj, proj, lb_table, hg_norm_w.reshape(1, HG_V), s0)
    return o, sout.transpose(0, 1, 3, 2)


def _top_values(s, count):
    vals = []
    work = s
    for _ in range(count):
        m = jnp.max(work, axis=0, keepdims=True)
        vals.append(m)
        work = jnp.where(work == m, -jnp.inf, work)
    return jnp.concatenate(vals, axis=0)


def _oddeven_merge(lo, hi, r):
    step = r * 2
    if step < hi - lo:
        yield from _oddeven_merge(lo, hi, step)
        yield from _oddeven_merge(lo + r, hi, step)
        yield from [(i, i + r) for i in range(lo + r, hi - r, step)]
    else:
        yield (lo, lo + r)


def _oddeven_merge_sort(lo, hi):
    if hi - lo >= 1:
        mid = lo + (hi - lo) // 2
        yield from _oddeven_merge_sort(lo, mid)
        yield from _oddeven_merge_sort(mid + 1, hi)
        yield from _oddeven_merge(lo, hi, 1)


SUBLANES = 8
_SORT_NETWORK = tuple(_oddeven_merge_sort(0, PEER_KEYS // SUBLANES - 1))


def _compare_exchange(v, i, j):
    v[i], v[j] = jnp.maximum(v[i], v[j]), jnp.minimum(v[i], v[j])


def _top_sorted(s):
    n = PEER_KEYS // SUBLANES
    assert n == PEER_TOPK
    x = s.reshape(n, SUBLANES, s.shape[-1])
    v = [x[k] for k in range(n)]
    for i, j in _SORT_NETWORK:
        _compare_exchange(v, i, j)
    shift = 1
    while shift < SUBLANES:
        w = [pltpu.roll(a, shift, axis=0) for a in v]
        v = [jnp.maximum(v[k], w[n - 1 - k]) for k in range(n)]
        d = n // 2
        while d >= 1:
            for k in range(n):
                if not k & d:
                    _compare_exchange(v, k, k + d)
            d //= 2
        shift *= 2
    return jnp.concatenate([a[0:1] for a in v], axis=0)


def _peer_query_kernel(w_ref, x_ref, sk_ref, thr_ref, e0_ref, s1_ref, e1_ref):
    half = PEER_QDIM // 2
    nh = thr_ref.shape[0]
    qt = jnp.dot(w_ref[...], x_ref[...], preferred_element_type=F32)

    def split(a):
        hi = a.astype(BF16)
        return hi, (a - hi.astype(F32)).astype(BF16)

    def scores(keys, q):
        (k_hi, k_lo), (q_hi, q_lo) = keys, split(q)
        return (jnp.dot(k_hi, q_hi, preferred_element_type=F32)
                + jnp.dot(k_hi, q_lo, preferred_element_type=F32)
                + jnp.dot(k_lo, q_hi, preferred_element_type=F32))

    keys0, keys1 = split(sk_ref[0]), split(sk_ref[1])
    for h in range(nh):
        s0 = scores(keys0, qt[h * PEER_QDIM:h * PEER_QDIM + half])
        s1 = scores(keys1, qt[h * PEER_QDIM + half:(h + 1) * PEER_QDIM])
        sv0 = _top_sorted(s0)
        sv1 = _top_sorted(s1)
        cand = [sv0[0:1] + sv1]
        for a in range(1, 8):
            cand.append(sv0[a:a + 1] + sv1[0:8])
        cand.append(sv0[8:16] + sv1[0:1])
        top = _top_values(jnp.concatenate(cand, axis=0), PEER_TOPK)
        z = jnp.sum(jnp.exp(top - top[0:1]), axis=0, keepdims=True)
        tau = top[PEER_TOPK - 1:PEER_TOPK]
        thr = jnp.full(s0.shape, jnp.inf, F32)
        for b in range(PEER_TOPK):
            thr = jnp.where(s0 + sv1[b:b + 1] >= tau, sv1[b:b + 1], thr)
        thr_ref[h] = thr
        s1_ref[h] = s1
        e0_ref[h] = jnp.exp(s0 - sv0[0:1]) / z
        e1_ref[h] = jnp.exp(s1 - sv1[0:1])


def _peer_query(w_pq_t, xn_t, sub_keys):
    d, t = xn_t.shape
    tb = _largest_divisor(t, (256, 128))
    nh = PEER_QUERY_HEADS_PER_STEP
    keyed = jax.ShapeDtypeStruct((PEER_HEADS, PEER_KEYS, t), F32)
    key_spec = pl.BlockSpec((nh, PEER_KEYS, tb), lambda i, h: (h, 0, i))
    return pl.pallas_call(
        _peer_query_kernel,
        grid=(t // tb, PEER_HEADS // nh),
        in_specs=[pl.BlockSpec((nh * PEER_QDIM, d), lambda i, h: (h, 0)),
                  pl.BlockSpec((d, tb), lambda i, h: (0, i)),
                  pl.BlockSpec((2, PEER_KEYS, PEER_QDIM // 2), lambda i, h: (0, 0, 0))],
        out_specs=[key_spec, key_spec, key_spec, key_spec],
        out_shape=[keyed, keyed, keyed, keyed],
        compiler_params=_params("parallel", "arbitrary"),
        name="peer_query",
    )(w_pq_t, xn_t, sub_keys)


def _gelu_tanh(x):
    return 0.5 * x * (1.0 + jnp.tanh(0.7978845608028654 * (x + 0.044715 * (x * x * x))))


def _peer_expert_kernel(x_ref, u_ref, v_ref, thr_ref, e0_ref, s1_ref, e1_ref, o_ref, w_ref, *, nb, tb):
    e = pl.program_id(1)

    @pl.when(e == 0)
    def _():
        o_ref[...] = jnp.zeros_like(o_ref)

    act = _gelu_tanh(jnp.dot(u_ref[...], x_ref[...], preferred_element_type=F32))
    for i in range(nb):
        for tl in range(tb // LANES):
            lanes = pl.ds(tl * LANES, LANES)
            acc = jnp.zeros((PEER_KEYS, LANES), F32)
            for h in range(PEER_HEADS):
                thr = thr_ref[h, 0, i:i + 1, lanes]
                c0 = e0_ref[h, 0, i:i + 1, lanes]
                acc = acc + jnp.where(s1_ref[h, :, lanes] >= thr, e1_ref[h, :, lanes] * c0, 0.0)
            w_ref[i * PEER_KEYS:(i + 1) * PEER_KEYS, lanes] = acc
    g = (w_ref[...] * act).astype(BF16)
    o_ref[...] += lax.dot_general(g, v_ref[...], (((0,), (0,)), ((), ())),
                                  preferred_element_type=F32)


def _peer_experts(xn_t, u, v, thr, e0, s1, e1):
    d, t = xn_t.shape
    n_exp = u.shape[0]
    tb = _largest_divisor(t, (512, 256, 128))
    nb = 4
    eb = nb * PEER_KEYS
    once = pl.Buffered(1)
    keyed = pl.BlockSpec((PEER_HEADS, PEER_KEYS, tb), lambda i, e: (0, 0, i), pipeline_mode=once)
    rows = pl.BlockSpec((PEER_HEADS, 1, nb, tb), lambda i, e: (0, e, 0, i))
    thr = thr.reshape(PEER_HEADS, PEER_KEYS // nb, nb, t)
    e0 = e0.reshape(PEER_HEADS, PEER_KEYS // nb, nb, t)
    return pl.pallas_call(
        functools.partial(_peer_expert_kernel, nb=nb, tb=tb),
        grid=(t // tb, n_exp // eb),
        in_specs=[pl.BlockSpec((d, tb), lambda i, e: (0, i), pipeline_mode=once),
                  pl.BlockSpec((eb, d), lambda i, e: (e, 0)),
                  pl.BlockSpec((eb, d), lambda i, e: (e, 0)),
                  rows, rows, keyed, keyed],
        out_specs=pl.BlockSpec((tb, d), lambda i, e: (i, 0)),
        out_shape=jax.ShapeDtypeStruct((t, d), F32),
        scratch_shapes=[pltpu.VMEM((eb, tb), F32)],
        compiler_params=_params("parallel", "arbitrary"),
        name="peer_experts",
    )(xn_t, u, v, thr, e0, s1, e1)


def _branches(proj, dtx, dt, row_off, bt, seqlen, conv_buf, s_ssd, s_hg, conv_w, conv_b, a_log, d_skip,
              ssd_norm_w, lb_table, hg_norm_w):
    xconv, new_conv = _conv_silu(proj, row_off, bt, seqlen, conv_buf, conv_w, conv_b)
    y_ssd, s_ssd_new = _ssd_scan(xconv, proj, dtx, dt, row_off, bt, seqlen,
                                 a_log, d_skip, ssd_norm_w, s_ssd)
    o_hg, s_hg_new = _hgrn_scan(proj, row_off, bt, seqlen, lb_table, hg_norm_w, s_hg)
    return y_ssd, o_hg, s_ssd_new, new_conv, s_hg_new


def kernel(x_prompt, x_sample, state_ssd, cache_ssd_conv, state_hgrn, lb_table, norm1_w, w_in, conv_w, conv_b, dt_bias, a_log, d_skip, ssd_norm_w, w_ssd_out, hg_norm_w, w_hg_out, w_o, norm2_w, w_pq, sub_keys, u_experts, v_experts, final_norm_w):
    assert w_in.shape[0] == 1 and lb_table.shape[0] == 2, "single-layer stack"
    bp, lp, d = x_prompt.shape
    bs, ls, _ = x_sample.shape
    tp, ts = bp * lp, bs * ls
    x_parts = (x_prompt.reshape(tp, d), x_sample.reshape(ts, d))

    w_in_t = w_in[0].T
    w_dt = jnp.pad(w_in_t[DT_COL:DT_COL + SSD_HEADS], ((0, LANES - SSD_HEADS), (0, 0))).astype(BF16)
    b_dt = jnp.pad(dt_bias[0], (0, LANES - SSD_HEADS)).reshape(1, LANES)

    xn = _rmsnorm_stacked(*x_parts, norm1_w[0], BF16)
    proj, u_bf16, v_bf16 = _in_proj(xn, w_in_t, u_experts[0], v_experts[0])
    dt, dtx = _dt_proj(xn, w_dt, b_dt)

    zeros = functools.partial(jnp.zeros, dtype=F32)
    common = (conv_w[0], conv_b[0], a_log[0], d_skip[0], ssd_norm_w[0], lb_table, hg_norm_w[0])
    ya_p, yb_p, ssd_p, conv_p, hg_p = _branches(
        proj, dtx, dt, 0, bp, lp,
        zeros((bp, SSD_CONV - 1, SSD_CONV_DIM)), zeros((bp, SSD_HEADS, SSD_HEADDIM, SSD_STATE)),
        zeros((bp, HG_HEADS, HG_K, HG_V)), *common)
    ya_s, yb_s, ssd_s, conv_s, hg_s = _branches(
        proj, dtx, dt, tp, bs, ls, cache_ssd_conv[0], state_ssd[0], state_hgrn[0], *common)

    merged = _gated_merge((ya_p, ya_s), (yb_p, yb_s), w_ssd_out[0].astype(BF16),
                          w_hg_out[0].astype(BF16), proj)
    h = _residual_matmul(merged, w_o[0].astype(BF16), x_parts)

    hn_t = _rmsnorm_t(h, norm2_w[0], BF16)
    thr, e0, s1, e1 = _peer_query(w_pq[0].T.astype(BF16), hn_t, sub_keys[0])
    peer = _peer_experts(hn_t, u_bf16, v_bf16, thr, e0, s1, e1)
    y_prompt = _add_rmsnorm(h, peer, final_norm_w, 0, tp).reshape(bp, lp, d)
    y_sample = _add_rmsnorm(h, peer, final_norm_w, tp, ts).reshape(bs, ls, d)
    return (y_prompt, y_sample, ssd_p[None], conv_p[None], hg_p[None],
            ssd_s[None], conv_s[None], hg_s[None])
```

```python
import functools

import jax
import jax.numpy as jnp
from jax import lax
from jax.experimental import pallas as pl
from jax.experimental.pallas import tpu as pltpu

F32 = jnp.float32
BF16 = jnp.bfloat16

NORM_EPS = 1e-6
LANES = 128
VMEM_LIMIT_BYTES = 56 * 1024 * 1024
IN_PROJ_VMEM_LIMIT_BYTES = 62 * 1024 * 1024

D_MODEL = 4096
SSD_GROUPS = 8
SSD_HEADS_PER_GROUP = 8
SSD_HEADDIM = 64
SSD_STATE = 128
SSD_HEADS = SSD_GROUPS * SSD_HEADS_PER_GROUP
SSD_INNER = SSD_HEADS * SSD_HEADDIM
SSD_GROUP_WIDTH = SSD_HEADS_PER_GROUP * SSD_HEADDIM
SSD_CONV = 4
SSD_CONV_DIM = SSD_INNER + 2 * SSD_GROUPS * SSD_STATE
HG_HEADS = 32
HG_K = 128
HG_V = 128
CHUNK = 64
PEER_HEADS = 8
PEER_KEYS = 128
PEER_TOPK = 16
PEER_QDIM = 256
PEER_EXPERT_KEY_ROWS = 4
PEER_QUERY_HEADS_PER_STEP = 4
SSD_GROUPS_PER_STEP = 4
HGRN_HEADS_PER_STEP = 8

COL_Z = 0
COL_XBC = SSD_INNER
COL_HQ = COL_XBC + SSD_CONV_DIM
COL_HF = COL_HQ + HG_HEADS * HG_K
COL_HI = COL_HF + HG_HEADS * HG_K
COL_HG = COL_HI + HG_HEADS * HG_V
COL_GA = COL_HG + HG_HEADS * HG_V
COL_GB = COL_GA + D_MODEL
PROJ_WIDTH = COL_GB + D_MODEL
DT_COL = SSD_INNER + SSD_CONV_DIM


def _largest_divisor(n, candidates):
    for c in candidates:
        if n % c == 0:
            return c
    raise ValueError(f"no block size among {candidates} divides {n}")


def _params(*semantics):
    return pltpu.CompilerParams(dimension_semantics=semantics,
                                vmem_limit_bytes=VMEM_LIMIT_BYTES)


def _sigmoid(x):
    return 1.0 / (1.0 + jnp.exp(-x))


def _silu(x):
    return x * _sigmoid(x)


def _stacked_specs(block, n_first, row_pos, col_of=None):
    def first(*idx):
        return (jnp.minimum(idx[row_pos], n_first - 1), 0 if col_of is None else col_of(*idx))

    def second(*idx):
        return (jnp.maximum(idx[row_pos] - n_first, 0), 0 if col_of is None else col_of(*idx))

    return pl.BlockSpec(block, first), pl.BlockSpec(block, second)


def _stacked_block(i, n_first, first_ref, second_ref):
    return jnp.where(i < n_first, first_ref[...], second_ref[...])


def _row_block(n_first_rows, n_second_rows, candidates):
    for c in candidates:
        if n_first_rows % c == 0 and n_second_rows % c == 0:
            return c
    raise ValueError(f"no block size among {candidates} divides {n_first_rows} and {n_second_rows}")


def _rmsnorm_stacked_kernel(xa_ref, xb_ref, w_ref, o_ref, *, n_first):
    x = _stacked_block(pl.program_id(0), n_first, xa_ref, xb_ref)
    ms = jnp.mean(x * x, axis=-1, keepdims=True)
    o_ref[...] = (x * lax.rsqrt(ms + NORM_EPS) * w_ref[...]).astype(o_ref.dtype)


def _rmsnorm_stacked(xa, xb, w, out_dtype):
    (ta, d), tb_rows = xa.shape, xb.shape[0]
    bm = _row_block(ta, tb_rows, (256, 128, 64, 32, 16, 8))
    n_first = ta // bm
    spec_a, spec_b = _stacked_specs((bm, d), n_first, 0)
    return pl.pallas_call(
        functools.partial(_rmsnorm_stacked_kernel, n_first=n_first),
        grid=((ta + tb_rows) // bm,),
        in_specs=[spec_a, spec_b, pl.BlockSpec((1, d), lambda i: (0, 0))],
        out_specs=pl.BlockSpec((bm, d), lambda i: (i, 0)),
        out_shape=jax.ShapeDtypeStruct((ta + tb_rows, d), out_dtype),
        compiler_params=_params("parallel"),
        name="rmsnorm_in",
    )(xa, xb, w.reshape(1, d))


def _rmsnorm_t_kernel(x_ref, w_ref, o_ref):
    x = x_ref[...]
    ms = jnp.mean(x * x, axis=-1, keepdims=True)
    o_ref[...] = jnp.transpose(x * lax.rsqrt(ms + NORM_EPS) * w_ref[...]).astype(o_ref.dtype)


def _rmsnorm_t(x, w, out_dtype):
    t, d = x.shape
    bm = _largest_divisor(t, (256, 128))
    return pl.pallas_call(
        _rmsnorm_t_kernel,
        grid=(t // bm,),
        in_specs=[pl.BlockSpec((bm, d), lambda i: (i, 0)),
                  pl.BlockSpec((1, d), lambda i: (0, 0))],
        out_specs=pl.BlockSpec((d, bm), lambda i: (0, i)),
        out_shape=jax.ShapeDtypeStruct((d, t), out_dtype),
        compiler_params=_params("parallel"),
        name="rmsnorm_t",
    )(x, w.reshape(1, d))


def _add_rmsnorm_kernel(a_ref, b_ref, w_ref, o_ref):
    x = a_ref[...] + b_ref[...]
    ms = jnp.mean(x * x, axis=-1, keepdims=True)
    o_ref[...] = x * lax.rsqrt(ms + NORM_EPS) * w_ref[...]


def _add_rmsnorm(a, b, w, row_off, rows):
    d = a.shape[1]
    bm = _row_block(rows, row_off, (256, 128, 64, 32, 16, 8))
    off = row_off // bm
    return pl.pallas_call(
        _add_rmsnorm_kernel,
        grid=(rows // bm,),
        in_specs=[pl.BlockSpec((bm, d), lambda i: (off + i, 0)),
                  pl.BlockSpec((bm, d), lambda i: (off + i, 0)),
                  pl.BlockSpec((1, d), lambda i: (0, 0))],
        out_specs=pl.BlockSpec((bm, d), lambda i: (i, 0)),
        out_shape=jax.ShapeDtypeStruct((rows, d), F32),
        compiler_params=_params("parallel"),
        name="add_rmsnorm",
    )(a, b, w.reshape(1, d))


def _in_proj_kernel(x_ref, wa_ref, wb_ref, ta_ref, tb_ref, o_ref, ta_out_ref, tb_out_ref, wbuf_ref,
                    *, n_plain, shift, chunk):
    ta_out_ref[...] = ta_ref[...].astype(ta_out_ref.dtype)
    tb_out_ref[...] = tb_ref[...].astype(tb_out_ref.dtype)
    j = pl.program_id(0)
    i = pl.program_id(1)
    bn = wa_ref.shape[0]

    @pl.when(jnp.logical_and(i == 0, j < n_plain))
    def _():
        def body(c, carry):
            rows = pl.ds(pl.multiple_of(c * chunk, chunk), chunk)
            wbuf_ref[rows, :] = wa_ref[rows, :].astype(BF16)
            return carry
        lax.fori_loop(0, bn // chunk, body, 0)

    @pl.when(jnp.logical_and(i == 0, j >= n_plain))
    def _():
        def body(c, carry):
            dst = pl.ds(pl.multiple_of(c * chunk, chunk), chunk)
            src = pl.ds(pl.multiple_of(c * chunk + shift, shift), chunk)
            wbuf_ref[dst, :] = wa_ref[src, :].astype(BF16)
            return carry
        lax.fori_loop(0, (bn - shift) // chunk, body, 0)
        wbuf_ref[bn - shift:bn, :] = wb_ref[...].astype(BF16)

    o_ref[...] = lax.dot_general(x_ref[...], wbuf_ref[...], (((1,), (1,)), ((), ())),
                                 preferred_element_type=F32)


def _in_proj(x, w_t, table_a, table_b):
    m, k = x.shape
    bm = _largest_divisor(m, (512, 256, 128, 64, 32, 16, 8))
    bn = 1024
    shift = SSD_HEADS
    assert DT_COL % bn == 0 and PROJ_WIDTH % bn == 0 and bn % shift == 0
    n_i = m // bm
    steps = (PROJ_WIDTH // bn) * n_i
    rows, width = table_a.shape
    assert table_b.shape == table_a.shape
    slab = next(r for r in range(16, rows + 1, 16) if rows % r == 0 and rows // r <= steps)
    last = rows // slab - 1
    slab_spec = pl.BlockSpec((slab, width), lambda j, i: (jnp.minimum(j * n_i + i, last), 0))
    table_bf16 = jax.ShapeDtypeStruct((rows, width), BF16)
    return pl.pallas_call(
        functools.partial(_in_proj_kernel, n_plain=DT_COL // bn, shift=shift, chunk=shift),
        grid=(PROJ_WIDTH // bn, n_i),
        in_specs=[pl.BlockSpec((bm, k), lambda j, i: (i, 0)),
                  pl.BlockSpec((bn, k), lambda j, i: (j, 0)),
                  pl.BlockSpec((shift, k), lambda j, i: ((bn // shift) * (j + 1), 0)),
                  slab_spec, slab_spec],
        out_specs=[pl.BlockSpec((bm, bn), lambda j, i: (i, j)), slab_spec, slab_spec],
        out_shape=[jax.ShapeDtypeStruct((m, PROJ_WIDTH), F32), table_bf16, table_bf16],
        scratch_shapes=[pltpu.VMEM((bn, k), BF16)],
        compiler_params=pltpu.CompilerParams(dimension_semantics=("arbitrary", "arbitrary"),
                                             vmem_limit_bytes=IN_PROJ_VMEM_LIMIT_BYTES),
        name="in_proj",
    )(x, w_t, w_t, table_a, table_b)


def _split3_dot(x, rhs_bf16):
    hi = x.astype(BF16)
    rest = x - hi.astype(F32)
    mid = rest.astype(BF16)
    lo = (rest - mid.astype(F32)).astype(BF16)
    return (jnp.dot(hi, rhs_bf16, preferred_element_type=F32)
            + jnp.dot(mid, rhs_bf16, preferred_element_type=F32)
            + jnp.dot(lo, rhs_bf16, preferred_element_type=F32))


def _dt_kernel(x_ref, w_ref, b_ref, ex_ref, dt_ref, dtx_ref):
    a = lax.dot_general(x_ref[...], w_ref[...], (((1,), (1,)), ((), ())),
                        preferred_element_type=F32) + b_ref[...]
    dt = jnp.maximum(a, 0.0) + jnp.log1p(jnp.exp(-jnp.abs(a)))
    dt_ref[...] = dt
    dtx_ref[...] = _split3_dot(dt, ex_ref[...])


def _dt_proj(x, w_t, b):
    m, k = x.shape
    n = w_t.shape[0]
    bm = _largest_divisor(m, (512, 256, 128, 64, 32, 16, 8))
    expand = (jnp.arange(SSD_INNER)[None, :] // SSD_HEADDIM == jnp.arange(n)[:, None]).astype(BF16)
    return pl.pallas_call(
        _dt_kernel,
        grid=(m // bm,),
        in_specs=[pl.BlockSpec((bm, k), lambda i: (i, 0)),
                  pl.BlockSpec((n, k), lambda i: (0, 0)),
                  pl.BlockSpec((1, n), lambda i: (0, 0)),
                  pl.BlockSpec((n, SSD_INNER), lambda i: (0, 0))],
        out_specs=[pl.BlockSpec((bm, n), lambda i: (i, 0)),
                   pl.BlockSpec((bm, SSD_INNER), lambda i: (i, 0))],
        out_shape=[jax.ShapeDtypeStruct((m, n), F32),
                   jax.ShapeDtypeStruct((m, SSD_INNER), F32)],
        compiler_params=_params("parallel"),
        name="dt_proj",
    )(x, w_t, b, expand)


def _merge_kernel(ya1_ref, ya2_ref, yb1_ref, yb2_ref, wa_ref, wb_ref, ga_ref, gb_ref, o_ref, *, n_first):
    i = pl.program_id(1)
    ya = _stacked_block(i, n_first, ya1_ref, ya2_ref)
    yb = _stacked_block(i, n_first, yb1_ref, yb2_ref)
    pa = jnp.dot(ya, wa_ref[...], preferred_element_type=F32)
    pb = jnp.dot(yb, wb_ref[...], preferred_element_type=F32)
    o_ref[...] = (_sigmoid(ga_ref[...]) * pa + _sigmoid(gb_ref[...]) * pb).astype(o_ref.dtype)


def _gated_merge(ya_parts, yb_parts, wa, wb, proj):
    (m1, k), m2 = ya_parts[0].shape, ya_parts[1].shape[0]
    n = wa.shape[1]
    bm = _row_block(m1, m2, (512, 256, 128, 64, 32, 16, 8))
    bn = 512
    n_first = m1 // bm
    y_first, y_second = _stacked_specs((bm, k), n_first, 1)
    return pl.pallas_call(
        functools.partial(_merge_kernel, n_first=n_first),
        grid=(n // bn, (m1 + m2) // bm),
        in_specs=[y_first, y_second, y_first, y_second,
                  pl.BlockSpec((k, bn), lambda j, i: (0, j)),
                  pl.BlockSpec((k, bn), lambda j, i: (0, j)),
                  pl.BlockSpec((bm, bn), lambda j, i: (i, COL_GA // bn + j)),
                  pl.BlockSpec((bm, bn), lambda j, i: (i, COL_GB // bn + j))],
        out_specs=pl.BlockSpec((bm, bn), lambda j, i: (i, j)),
        out_shape=jax.ShapeDtypeStruct((m1 + m2, n), BF16),
        compiler_params=_params("parallel", "parallel"),
        name="gated_merge",
    )(ya_parts[0], ya_parts[1], yb_parts[0], yb_parts[1], wa, wb, proj, proj)


def _residual_matmul_kernel(x_ref, w_ref, r1_ref, r2_ref, o_ref, *, n_first):
    r = _stacked_block(pl.program_id(1), n_first, r1_ref, r2_ref)
    o_ref[...] = r + jnp.dot(x_ref[...], w_ref[...], preferred_element_type=F32)


def _residual_matmul(x, w, r_parts):
    m, k = x.shape
    n = w.shape[1]
    m1, m2 = r_parts[0].shape[0], r_parts[1].shape[0]
    bm = _row_block(m1, m2, (512, 256, 128, 64, 32, 16, 8))
    bn = 1024
    n_first = m1 // bm
    r_first, r_second = _stacked_specs((bm, bn), n_first, 1, col_of=lambda j, i: j)
    return pl.pallas_call(
        functools.partial(_residual_matmul_kernel, n_first=n_first),
        grid=(n // bn, m // bm),
        in_specs=[pl.BlockSpec((bm, k), lambda j, i: (i, 0)),
                  pl.BlockSpec((k, bn), lambda j, i: (0, j)),
                  r_first, r_second],
        out_specs=pl.BlockSpec((bm, bn), lambda j, i: (i, j)),
        out_shape=jax.ShapeDtypeStruct((m, n), F32),
        compiler_params=_params("parallel", "parallel"),
        name="residual_matmul",
    )(x, w, r_parts[0], r_parts[1])


CONV_PAD = 8


def _conv_kernel(u_ref, buf_ref, w_ref, b_ref, o_ref, cache_ref, prev_ref, *, tb):
    t = pl.program_id(2)
    lo = CONV_PAD - (SSD_CONV - 1)

    @pl.when(t == 0)
    def _():
        prev_ref[lo:CONV_PAD, :] = buf_ref[0]

    u = u_ref[...]
    first_row = lax.broadcasted_iota(jnp.int32, (CONV_PAD, u.shape[1]), 0) == 0
    acc = b_ref[...] + u * w_ref[SSD_CONV - 1:SSD_CONV, :]
    shifted = u
    for k in range(1, SSD_CONV):
        shifted = pltpu.roll(shifted, 1, axis=0)
        head = jnp.where(first_row, prev_ref[CONV_PAD - k:CONV_PAD - k + 1, :], shifted[0:CONV_PAD])
        shifted = jnp.concatenate([head, shifted[CONV_PAD:]], axis=0)
        acc = acc + shifted * w_ref[SSD_CONV - 1 - k:SSD_CONV - k, :]
    o_ref[...] = _silu(acc)
    prev_ref[...] = u[tb - CONV_PAD:tb]

    @pl.when(t == pl.num_programs(2) - 1)
    def _():
        cache_ref[0] = prev_ref[lo:CONV_PAD, :]


def _conv_silu(proj, row_off, bt, seqlen, buf, conv_w, conv_b):
    tb = _largest_divisor(seqlen, (512, 256, 128, 64, 32))
    cb = 2048
    nt = seqlen // tb
    roff = row_off // tb
    coff = COL_XBC // cb
    return pl.pallas_call(
        functools.partial(_conv_kernel, tb=tb),
        grid=(SSD_CONV_DIM // cb, bt, nt),
        in_specs=[pl.BlockSpec((tb, cb), lambda c, b, t: (roff + b * nt + t, coff + c)),
                  pl.BlockSpec((1, SSD_CONV - 1, cb), lambda c, b, t: (b, 0, c)),
                  pl.BlockSpec((SSD_CONV, cb), lambda c, b, t: (0, c)),
                  pl.BlockSpec((1, cb), lambda c, b, t: (0, c))],
        out_specs=[pl.BlockSpec((tb, cb), lambda c, b, t: (b * nt + t, c)),
                   pl.BlockSpec((1, SSD_CONV - 1, cb), lambda c, b, t: (b, 0, c))],
        out_shape=[jax.ShapeDtypeStruct((bt * seqlen, SSD_CONV_DIM), F32),
                   jax.ShapeDtypeStruct((bt, SSD_CONV - 1, SSD_CONV_DIM), F32)],
        scratch_shapes=[pltpu.VMEM((CONV_PAD, cb), F32)],
        compiler_params=_params("parallel", "parallel", "arbitrary"),
        name="conv_silu",
    )(proj, buf, conv_w, conv_b.reshape(1, SSD_CONV_DIM))


def _ssd_kernel(x_ref, b_ref, c_ref, z_ref, dtx_ref, dtt_ref, alx_ref, alt_ref, dsk_ref, nw_ref,
                s0_ref, y_ref, sout_ref, st_ref, *, q, nchunk, ng):
    t = pl.program_id(2)
    p, r, gw, n = SSD_HEADDIM, SSD_HEADS_PER_GROUP, SSD_GROUP_WIDTH, SSD_STATE

    @pl.when(t == 0)
    def _():
        st_ref[...] = s0_ref[0]

    row = lax.broadcasted_iota(jnp.int32, (q, q), 0)
    col = lax.broadcasted_iota(jnp.int32, (q, q), 1)
    lower = row >= col
    tril = lower.astype(BF16)
    triu = (row <= col).astype(BF16)
    a_x = -jnp.exp(alx_ref[...])
    a_t = -jnp.exp(alt_ref[...]).reshape(ng * r, 1)
    d_skip = dsk_ref[...]
    norm_w = nw_ref[...]
    nt_dims = (((1,), (1,)), ((), ()))
    tn_dims = (((0,), (0,)), ((), ()))

    def chunk(ci, carry):
        rows = pl.ds(pl.multiple_of(ci * q, q), q)
        x = x_ref[rows, :]
        dtx = dtx_ref[rows, :]
        cum = _cumsum_rows(tril, dtx * a_x)
        cum_t = _split3_dot(dtt_ref[:, ci].reshape(ng * r, q) * a_t, triu)
        cum_last = cum[q - 1:q, :]
        decay_in = jnp.exp(cum)
        xdt = (x * dtx).astype(BF16)
        xt = (x * (jnp.exp(cum_last - cum) * dtx)).astype(BF16)
        decay_st = jnp.exp(cum_last)
        gate = _silu(z_ref[rows, :])
        skip = d_skip * x
        for gi in range(ng):
            cols = slice(gi * gw, (gi + 1) * gw)
            bm = b_ref[rows, gi * n:(gi + 1) * n].astype(BF16)
            cm = c_ref[rows, gi * n:(gi + 1) * n].astype(BF16)
            cb = lax.dot_general(cm, bm, nt_dims, preferred_element_type=F32)
            st = st_ref[gi]
            y = jnp.dot(cm, st.astype(BF16), preferred_element_type=F32) * decay_in[:, cols]
            parts = []
            for h in range(r):
                c0 = gi * gw + h * p
                seg = cum[:, c0:c0 + q] - cum_t[gi * r + h:gi * r + h + 1, :]
                mix = (cb * jnp.where(lower, jnp.exp(seg), 0.0)).astype(BF16)
                parts.append(jnp.dot(mix, xdt[:, c0:c0 + p], preferred_element_type=F32))
            y = (y + jnp.concatenate(parts, axis=1) + skip[:, cols]) * gate[:, cols]
            ms = jnp.mean(y * y, axis=-1, keepdims=True)
            y_ref[rows, cols] = (y * lax.rsqrt(ms + NORM_EPS) * norm_w[:, cols]).astype(y_ref.dtype)
            st_ref[gi] = st * decay_st[:, cols] + lax.dot_general(
                bm, xt[:, cols], tn_dims, preferred_element_type=F32)
        return carry

    lax.fori_loop(0, nchunk, chunk, 0)

    @pl.when(t == pl.num_programs(2) - 1)
    def _():
        sout_ref[0] = st_ref[...]


def _ssd_scan(xconv, proj, dtx, dt, row_off, bt, seqlen, a_log, d_skip, ssd_norm_w, state0):
    g, r, p, n, gw = SSD_GROUPS, SSD_HEADS_PER_GROUP, SSD_HEADDIM, SSD_STATE, SSD_GROUP_WIDTH
    ng = SSD_GROUPS_PER_STEP
    q = min(CHUNK, seqlen)
    tb = _largest_divisor(seqlen, (512, 256, 128, 64, 32))
    nchunk = tb // q
    nt = seqlen // tb
    rows = bt * seqlen
    roff = row_off // tb
    dtt = dt[row_off:row_off + rows, :SSD_HEADS].reshape(rows // q, q, g, r).transpose(2, 0, 3, 1)
    alx = jnp.repeat(a_log, p).reshape(1, SSD_INNER)
    alt = a_log.reshape(g, r, 1)
    dsk = jnp.repeat(d_skip, p).reshape(1, SSD_INNER)
    s0 = state0.reshape(bt, g, r, p, n).transpose(0, 1, 4, 2, 3).reshape(bt, g, n, gw)
    bcol = SSD_INNER // (ng * n)
    ccol = bcol + g // ng
    wide = ng * gw
    y, sout = pl.pallas_call(
        functools.partial(_ssd_kernel, q=q, nchunk=nchunk, ng=ng),
        grid=(bt, g // ng, nt),
        in_specs=[pl.BlockSpec((tb, wide), lambda b, gi, t: (b * nt + t, gi)),
                  pl.BlockSpec((tb, ng * n), lambda b, gi, t: (b * nt + t, bcol + gi)),
                  pl.BlockSpec((tb, ng * n), lambda b, gi, t: (b * nt + t, ccol + gi)),
                  pl.BlockSpec((tb, wide), lambda b, gi, t: (roff + b * nt + t, COL_Z // wide + gi)),
                  pl.BlockSpec((tb, wide), lambda b, gi, t: (roff + b * nt + t, gi)),
                  pl.BlockSpec((ng, nchunk, r, q), lambda b, gi, t: (gi, b * nt + t, 0, 0)),
                  pl.BlockSpec((1, wide), lambda b, gi, t: (0, gi)),
                  pl.BlockSpec((ng, r, 1), lambda b, gi, t: (gi, 0, 0)),
                  pl.BlockSpec((1, wide), lambda b, gi, t: (0, gi)),
                  pl.BlockSpec((1, wide), lambda b, gi, t: (0, gi)),
                  pl.BlockSpec((1, ng, n, gw), lambda b, gi, t: (b, gi, 0, 0))],
        out_specs=[pl.BlockSpec((tb, wide), lambda b, gi, t: (b * nt + t, gi)),
                   pl.BlockSpec((1, ng, n, gw), lambda b, gi, t: (b, gi, 0, 0))],
        out_shape=[jax.ShapeDtypeStruct((rows, SSD_INNER), BF16),
                   jax.ShapeDtypeStruct((bt, g, n, gw), F32)],
        scratch_shapes=[pltpu.VMEM((ng, n, gw), F32)],
        compiler_params=_params("parallel", "parallel", "arbitrary"),
        name="ssd_scan",
    )(xconv, xconv, xconv, proj, dtx, dtt, alx, alt, dsk, ssd_norm_w.reshape(1, SSD_INNER), s0)
    state = sout.reshape(bt, g, n, r, p).transpose(0, 1, 3, 4, 2).reshape(bt, SSD_HEADS, p, n)
    return y, state


def _cumsum_rows(tril_bf16, x):
    hi = x.astype(BF16)
    rest = x - hi.astype(F32)
    mid = rest.astype(BF16)
    lo = (rest - mid.astype(F32)).astype(BF16)
    return (jnp.dot(tril_bf16, hi, preferred_element_type=F32)
            + jnp.dot(tril_bf16, mid, preferred_element_type=F32)
            + jnp.dot(tril_bf16, lo, preferred_element_type=F32))


def _hgrn_kernel(q_ref, f_ref, i_ref, g_ref, lbt_ref, nw_ref, s0_ref, o_ref, sout_ref, st_ref,
                 *, q, nchunk, nh):
    t = pl.program_id(2)

    @pl.when(t == 0)
    def _():
        st_ref[...] = s0_ref[0]

    hq = q // 2
    row = lax.broadcasted_iota(jnp.int32, (q, q), 0)
    col = lax.broadcasted_iota(jnp.int32, (q, q), 1)
    lower = row >= col
    tril = lower.astype(BF16)
    left_half = lax.broadcasted_iota(jnp.int32, (hq, LANES), 1) < hq
    k_pad = jnp.zeros((LANES - 3 * hq, nh * HG_K), BF16)
    tab = lbt_ref[...]
    e = jnp.exp(tab - jnp.max(tab, axis=0, keepdims=True))
    lb = e[0:1, :] / jnp.sum(e, axis=0, keepdims=True)
    norm_w = nw_ref[...]
    nt_dims = (((1,), (1,)), ((), ()))
    tn_dims = (((0,), (0,)), ((), ()))

    def decayed(x, cum_rows, ref_row, sign):
        return (x * jnp.exp(sign * (cum_rows - ref_row))).astype(BF16)

    def chunk(ci, carry):
        rows = pl.ds(pl.multiple_of(ci * q, q), q)
        qq = _silu(q_ref[rows, :]) * (HG_K ** -0.5)
        f = lb + (1.0 - lb) * _sigmoid(f_ref[rows, :])
        kk = 1.0 - f
        v = i_ref[rows, :].astype(BF16)
        gate = _silu(g_ref[rows, :])
        cum = _cumsum_rows(tril, jnp.log(f))
        top, bot = slice(0, hq), slice(hq, q)
        mid_top = cum[hq // 2 - 1:hq // 2, :]
        mid_bot = cum[hq + hq // 2 - 1:hq + hq // 2, :]
        edge = cum[hq - 1:hq, :]
        cum_last = cum[q - 1:q, :]
        q_tt = decayed(qq[top], cum[top], mid_top, 1.0)
        k_tt = decayed(kk[top], cum[top], mid_top, -1.0)
        q_bb = decayed(qq[bot], cum[bot], mid_bot, 1.0)
        k_bb = decayed(kk[bot], cum[bot], mid_bot, -1.0)
        q_bt = decayed(qq[bot], cum[bot], edge, 1.0)
        k_bt = decayed(kk[top], cum[top], edge, -1.0)
        qe = (qq * jnp.exp(cum)).astype(BF16)
        kt = (kk * jnp.exp(cum_last - cum)).astype(BF16)
        dec = jnp.exp(cum_last)
        q_stack = jnp.concatenate([q_tt, q_bb, q_bt], axis=0)
        k_stack = jnp.concatenate([k_tt, k_bb, k_bt, k_pad], axis=0)
        for h in range(nh):
            sl = slice(h * HG_K, (h + 1) * HG_K)
            s_all = lax.dot_general(q_stack[:, sl], k_stack[:, sl], nt_dims,
                                    preferred_element_type=F32)
            bottom_left = pltpu.roll(s_all[2 * hq:3 * hq, :], LANES - 2 * hq, axis=1)
            s_bot = jnp.where(left_half, bottom_left, s_all[hq:2 * hq, :])
            att = jnp.concatenate([s_all[0:hq, 0:q], s_bot[:, 0:q]], axis=0)
            att = jnp.where(lower, att, 0.0).astype(BF16)
            st = st_ref[h]
            o = jnp.dot(att, v[:, sl], preferred_element_type=F32) + lax.dot_general(
                qe[:, sl], st.astype(BF16), nt_dims, preferred_element_type=F32)
            ms = jnp.mean(o * o, axis=-1, keepdims=True)
            o_ref[rows, sl] = (o * lax.rsqrt(ms + NORM_EPS) * norm_w * gate[:, sl]).astype(o_ref.dtype)
            st_ref[h] = st * dec[:, sl] + lax.dot_general(v[:, sl], kt[:, sl], tn_dims,
                                                          preferred_element_type=F32)
        return carry

    lax.fori_loop(0, nchunk, chunk, 0)

    @pl.when(t == pl.num_programs(2) - 1)
    def _():
        sout_ref[0] = st_ref[...]


def _hgrn_scan(proj, row_off, bt, seqlen, lb_table, hg_norm_w, state0):
    q = min(CHUNK, seqlen)
    tb = _largest_divisor(seqlen, (512, 256, 128, 64, 32))
    nchunk = tb // q
    nt = seqlen // tb
    rows = bt * seqlen
    roff = row_off // tb
    nh = HGRN_HEADS_PER_STEP
    hw = nh * HG_K
    s0 = state0.transpose(0, 1, 3, 2)
    nrow = lb_table.shape[0]

    def col_spec(col0):
        return pl.BlockSpec((tb, hw), lambda b, h, t: (roff + b * nt + t, col0 // hw + h))

    o, sout = pl.pallas_call(
        functools.partial(_hgrn_kernel, q=q, nchunk=nchunk, nh=nh),
        grid=(bt, HG_HEADS // nh, nt),
        in_specs=[col_spec(COL_HQ), col_spec(COL_HF), col_spec(COL_HI), col_spec(COL_HG),
                  pl.BlockSpec((nrow, hw), lambda b, h, t: (0, h)),
                  pl.BlockSpec((1, HG_V), lambda b, h, t: (0, 0)),
                  pl.BlockSpec((1, nh, HG_V, HG_K), lambda b, h, t: (b, h, 0, 0))],
        out_specs=[pl.BlockSpec((tb, hw), lambda b, h, t: (b * nt + t, h)),
                   pl.BlockSpec((1, nh, HG_V, HG_K), lambda b, h, t: (b, h, 0, 0))],
        out_shape=[jax.ShapeDtypeStruct((rows, HG_HEADS * HG_V), BF16),
                   jax.ShapeDtypeStruct((bt, HG_HEADS, HG_V, HG_K), F32)],
        scratch_shapes=[pltpu.VMEM((nh, HG_V, HG_K), F32)],
        compiler_params=_params("parallel", "parallel", "arbitrary"),
        name="hgrn_scan",
    )(proj, proj, proj, proj, lb_table, hg_norm_w.reshape(1, HG_V), s0)
    return o, sout.transpose(0, 1, 3, 2)


def _top_values(s, count):
    vals = []
    work = s
    for _ in range(count):
        m = jnp.max(work, axis=0, keepdims=True)
        vals.append(m)
        work = jnp.where(work == m, -jnp.inf, work)
    return jnp.concatenate(vals, axis=0)


def _oddeven_merge(lo, hi, r):
    step = r * 2
    if step < hi - lo:
        yield from _oddeven_merge(lo, hi, step)
        yield from _oddeven_merge(lo + r, hi, step)
        yield from [(i, i + r) for i in range(lo + r, hi - r, step)]
    else:
        yield (lo, lo + r)


def _oddeven_merge_sort(lo, hi):
    if hi - lo >= 1:
        mid = lo + (hi - lo) // 2
        yield from _oddeven_merge_sort(lo, mid)
        yield from _oddeven_merge_sort(mid + 1, hi)
        yield from _oddeven_merge(lo, hi, 1)


SUBLANES = 8
_SORT_NETWORK = tuple(_oddeven_merge_sort(0, PEER_KEYS // SUBLANES - 1))


def _compare_exchange(v, i, j):
    v[i], v[j] = jnp.maximum(v[i], v[j]), jnp.minimum(v[i], v[j])


def _top_sorted(s):
    n = PEER_KEYS // SUBLANES
    assert n == PEER_TOPK
    x = s.reshape(n, SUBLANES, s.shape[-1])
    v = [x[k] for k in range(n)]
    for i, j in _SORT_NETWORK:
        _compare_exchange(v, i, j)
    shift = 1
    while shift < SUBLANES:
        w = [pltpu.roll(a, shift, axis=0) for a in v]
        v = [jnp.maximum(v[k], w[n - 1 - k]) for k in range(n)]
        d = n // 2
        while d >= 1:
            for k in range(n):
                if not k & d:
                    _compare_exchange(v, k, k + d)
            d //= 2
        shift *= 2
    return jnp.concatenate([a[0:1] for a in v], axis=0)


def _peer_query_kernel(w_ref, x_ref, sk_ref, thr_ref, e0_ref, s1_ref, e1_ref):
    half = PEER_QDIM // 2
    nh = thr_ref.shape[0]
    qt = jnp.dot(w_ref[...], x_ref[...], preferred_element_type=F32)

    def split(a):
        hi = a.astype(BF16)
        return hi, (a - hi.astype(F32)).astype(BF16)

    def scores(keys, q):
        (k_hi, k_lo), (q_hi, q_lo) = keys, split(q)
        return (jnp.dot(k_hi, q_hi, preferred_element_type=F32)
                + jnp.dot(k_hi, q_lo, preferred_element_type=F32)
                + jnp.dot(k_lo, q_hi, preferred_element_type=F32))

    keys0, keys1 = split(sk_ref[0]), split(sk_ref[1])
    for h in range(nh):
        s0 = scores(keys0, qt[h * PEER_QDIM:h * PEER_QDIM + half])
        s1 = scores(keys1, qt[h * PEER_QDIM + half:(h + 1) * PEER_QDIM])
        sv0 = _top_sorted(s0)
        sv1 = _top_sorted(s1)
        cand = [sv0[0:1] + sv1]
        for a in range(1, 8):
            cand.append(sv0[a:a + 1] + sv1[0:8])
        cand.append(sv0[8:16] + sv1[0:1])
        top = _top_values(jnp.concatenate(cand, axis=0), PEER_TOPK)
        z = jnp.sum(jnp.exp(top - top[0:1]), axis=0, keepdims=True)
        tau = top[PEER_TOPK - 1:PEER_TOPK]
        thr = jnp.full(s0.shape, jnp.inf, F32)
        for b in range(PEER_TOPK):
            thr = jnp.where(s0 + sv1[b:b + 1] >= tau, sv1[b:b + 1], thr)
        thr_ref[h] = thr.reshape(thr_ref.shape[1:])
        s1_ref[h] = s1
        e0_ref[h] = (jnp.exp(s0 - sv0[0:1]) / z).reshape(e0_ref.shape[1:])
        e1_ref[h] = jnp.exp(s1 - sv1[0:1])


def _peer_query(w_pq_t, xn_t, sub_keys):
    d, t = xn_t.shape
    tb = _largest_divisor(t, (256, 128))
    nh = PEER_QUERY_HEADS_PER_STEP
    keyed = jax.ShapeDtypeStruct((PEER_HEADS, PEER_KEYS, t), F32)
    key_spec = pl.BlockSpec((nh, PEER_KEYS, tb), lambda i, h: (h, 0, i))
    nb = PEER_EXPERT_KEY_ROWS
    grouped = jax.ShapeDtypeStruct((PEER_HEADS, PEER_KEYS // nb, nb, t), F32)
    group_spec = pl.BlockSpec((nh, PEER_KEYS // nb, nb, tb), lambda i, h: (h, 0, 0, i))
    return pl.pallas_call(
        _peer_query_kernel,
        grid=(t // tb, PEER_HEADS // nh),
        in_specs=[pl.BlockSpec((nh * PEER_QDIM, d), lambda i, h: (h, 0)),
                  pl.BlockSpec((d, tb), lambda i, h: (0, i)),
                  pl.BlockSpec((2, PEER_KEYS, PEER_QDIM // 2), lambda i, h: (0, 0, 0))],
        out_specs=[group_spec, group_spec, key_spec, key_spec],
        out_shape=[grouped, grouped, keyed, keyed],
        compiler_params=_params("parallel", "arbitrary"),
        name="peer_query",
    )(w_pq_t, xn_t, sub_keys)


def _gelu_tanh(x):
    return 0.5 * x * (1.0 + jnp.tanh(0.7978845608028654 * (x + 0.044715 * (x * x * x))))


def _peer_expert_kernel(x_ref, u_ref, v_ref, thr_ref, e0_ref, s1_ref, e1_ref, o_ref, w_ref, *, nb, tb):
    e = pl.program_id(1)

    @pl.when(e == 0)
    def _():
        o_ref[...] = jnp.zeros_like(o_ref)

    act = _gelu_tanh(jnp.dot(u_ref[...], x_ref[...], preferred_element_type=F32))
    for i in range(nb):
        for tl in range(tb // LANES):
            lanes = pl.ds(tl * LANES, LANES)
            acc = jnp.zeros((PEER_KEYS, LANES), F32)
            for h in range(PEER_HEADS):
                thr = thr_ref[h, 0, i:i + 1, lanes]
                c0 = e0_ref[h, 0, i:i + 1, lanes]
                acc = acc + jnp.where(s1_ref[h, :, lanes] >= thr, e1_ref[h, :, lanes] * c0, 0.0)
            w_ref[i * PEER_KEYS:(i + 1) * PEER_KEYS, lanes] = acc
    g = (w_ref[...] * act).astype(BF16)
    o_ref[...] += lax.dot_general(g, v_ref[...], (((0,), (0,)), ((), ())),
                                  preferred_element_type=F32)


def _peer_experts(xn_t, u, v, thr, e0, s1, e1):
    d, t = xn_t.shape
    n_exp = u.shape[0]
    tb = _largest_divisor(t, (512, 256, 128))
    nb = PEER_EXPERT_KEY_ROWS
    eb = nb * PEER_KEYS
    once = pl.Buffered(1)
    keyed = pl.BlockSpec((PEER_HEADS, PEER_KEYS, tb), lambda i, e: (0, 0, i), pipeline_mode=once)
    rows = pl.BlockSpec((PEER_HEADS, 1, nb, tb), lambda i, e: (0, e, 0, i))
    return pl.pallas_call(
        functools.partial(_peer_expert_kernel, nb=nb, tb=tb),
        grid=(t // tb, n_exp // eb),
        in_specs=[pl.BlockSpec((d, tb), lambda i, e: (0, i), pipeline_mode=once),
                  pl.BlockSpec((eb, d), lambda i, e: (e, 0)),
                  pl.BlockSpec((eb, d), lambda i, e: (e, 0)),
                  rows, rows, keyed, keyed],
        out_specs=pl.BlockSpec((tb, d), lambda i, e: (i, 0)),
        out_shape=jax.ShapeDtypeStruct((t, d), F32),
        scratch_shapes=[pltpu.VMEM((eb, tb), F32)],
        compiler_params=_params("parallel", "arbitrary"),
        name="peer_experts",
    )(xn_t, u, v, thr, e0, s1, e1)


def _branches(proj, dtx, dt, row_off, bt, seqlen, conv_buf, s_ssd, s_hg, conv_w, conv_b, a_log, d_skip,
              ssd_norm_w, lb_table, hg_norm_w):
    xconv, new_conv = _conv_silu(proj, row_off, bt, seqlen, conv_buf, conv_w, conv_b)
    y_ssd, s_ssd_new = _ssd_scan(xconv, proj, dtx, dt, row_off, bt, seqlen,
                                 a_log, d_skip, ssd_norm_w, s_ssd)
    o_hg, s_hg_new = _hgrn_scan(proj, row_off, bt, seqlen, lb_table, hg_norm_w, s_hg)
    return y_ssd, o_hg, s_ssd_new, new_conv, s_hg_new


def kernel(x_prompt, x_sample, state_ssd, cache_ssd_conv, state_hgrn, lb_table, norm1_w, w_in, conv_w, conv_b, dt_bias, a_log, d_skip, ssd_norm_w, w_ssd_out, hg_norm_w, w_hg_out, w_o, norm2_w, w_pq, sub_keys, u_experts, v_experts, final_norm_w):
    assert w_in.shape[0] == 1 and lb_table.shape[0] == 2, "single-layer stack"
    bp, lp, d = x_prompt.shape
    bs, ls, _ = x_sample.shape
    tp, ts = bp * lp, bs * ls
    x_parts = (x_prompt.reshape(tp, d), x_sample.reshape(ts, d))

    w_in_t = w_in[0].T
    w_dt = jnp.pad(w_in_t[DT_COL:DT_COL + SSD_HEADS], ((0, LANES - SSD_HEADS), (0, 0))).astype(BF16)
    b_dt = jnp.pad(dt_bias[0], (0, LANES - SSD_HEADS)).reshape(1, LANES)

    xn = _rmsnorm_stacked(*x_parts, norm1_w[0], BF16)
    proj, u_bf16, v_bf16 = _in_proj(xn, w_in_t, u_experts[0], v_experts[0])
    dt, dtx = _dt_proj(xn, w_dt, b_dt)

    zeros = functools.partial(jnp.zeros, dtype=F32)
    common = (conv_w[0], conv_b[0], a_log[0], d_skip[0], ssd_norm_w[0], lb_table, hg_norm_w[0])
    ya_p, yb_p, ssd_p, conv_p, hg_p = _branches(
        proj, dtx, dt, 0, bp, lp,
        zeros((bp, SSD_CONV - 1, SSD_CONV_DIM)), zeros((bp, SSD_HEADS, SSD_HEADDIM, SSD_STATE)),
        zeros((bp, HG_HEADS, HG_K, HG_V)), *common)
    ya_s, yb_s, ssd_s, conv_s, hg_s = _branches(
        proj, dtx, dt, tp, bs, ls, cache_ssd_conv[0], state_ssd[0], state_hgrn[0], *common)

    merged = _gated_merge((ya_p, ya_s), (yb_p, yb_s), w_ssd_out[0].astype(BF16),
                          w_hg_out[0].astype(BF16), proj)
    h = _residual_matmul(merged, w_o[0].astype(BF16), x_parts)

    hn_t = _rmsnorm_t(h, norm2_w[0], BF16)
    thr, e0, s1, e1 = _peer_query(w_pq[0].T.astype(BF16), hn_t, sub_keys[0])
    peer = _peer_experts(hn_t, u_bf16, v_bf16, thr, e0, s1, e1)
    y_prompt = _add_rmsnorm(h, peer, final_norm_w, 0, tp).reshape(bp, lp, d)
    y_sample = _add_rmsnorm(h, peer, final_norm_w, tp, ts).reshape(bs, ls, d)
    return (y_prompt, y_sample, ssd_p[None], conv_p[None], hg_p[None],
            ssd_s[None], conv_s[None], hg_s[None])
```

```python
import functools

import jax
import jax.numpy as jnp
from jax import lax
from jax.experimental import pallas as pl
from jax.experimental.pallas import tpu as pltpu

F32 = jnp.float32
BF16 = jnp.bfloat16

NORM_EPS = 1e-6
LANES = 128
VMEM_LIMIT_BYTES = 56 * 1024 * 1024
IN_PROJ_VMEM_LIMIT_BYTES = 62 * 1024 * 1024

D_MODEL = 4096
SSD_GROUPS = 8
SSD_HEADS_PER_GROUP = 8
SSD_HEADDIM = 64
SSD_STATE = 128
SSD_HEADS = SSD_GROUPS * SSD_HEADS_PER_GROUP
SSD_INNER = SSD_HEADS * SSD_HEADDIM
SSD_GROUP_WIDTH = SSD_HEADS_PER_GROUP * SSD_HEADDIM
SSD_CONV = 4
SSD_CONV_DIM = SSD_INNER + 2 * SSD_GROUPS * SSD_STATE
HG_HEADS = 32
HG_K = 128
HG_V = 128
CHUNK = 64
PEER_HEADS = 8
PEER_KEYS = 128
PEER_TOPK = 16
PEER_QDIM = 256
PEER_EXPERT_KEY_ROWS = 4
PEER_QUERY_HEADS_PER_STEP = 4
SSD_GROUPS_PER_STEP = 4
HGRN_HEADS_PER_STEP = 8

COL_Z = 0
COL_XBC = SSD_INNER
COL_HQ = COL_XBC + SSD_CONV_DIM
COL_HF = COL_HQ + HG_HEADS * HG_K
COL_HI = COL_HF + HG_HEADS * HG_K
COL_HG = COL_HI + HG_HEADS * HG_V
COL_GA = COL_HG + HG_HEADS * HG_V
COL_GB = COL_GA + D_MODEL
PROJ_WIDTH = COL_GB + D_MODEL
DT_COL = SSD_INNER + SSD_CONV_DIM


def _largest_divisor(n, candidates):
    for c in candidates:
        if n % c == 0:
            return c
    raise ValueError(f"no block size among {candidates} divides {n}")


def _params(*semantics):
    return pltpu.CompilerParams(dimension_semantics=semantics,
                                vmem_limit_bytes=VMEM_LIMIT_BYTES)


def _sigmoid(x):
    return 1.0 / (1.0 + jnp.exp(-x))


def _silu(x):
    return x * _sigmoid(x)


def _stacked_specs(block, n_first, row_pos, col_of=None):
    def first(*idx):
        return (jnp.minimum(idx[row_pos], n_first - 1), 0 if col_of is None else col_of(*idx))

    def second(*idx):
        return (jnp.maximum(idx[row_pos] - n_first, 0), 0 if col_of is None else col_of(*idx))

    return pl.BlockSpec(block, first), pl.BlockSpec(block, second)


def _stacked_block(i, n_first, first_ref, second_ref):
    return jnp.where(i < n_first, first_ref[...], second_ref[...])


def _row_block(n_first_rows, n_second_rows, candidates):
    for c in candidates:
        if n_first_rows % c == 0 and n_second_rows % c == 0:
            return c
    raise ValueError(f"no block size among {candidates} divides {n_first_rows} and {n_second_rows}")


def _rmsnorm_stacked_kernel(xa_ref, xb_ref, w_ref, o_ref, *, n_first):
    x = _stacked_block(pl.program_id(0), n_first, xa_ref, xb_ref)
    ms = jnp.mean(x * x, axis=-1, keepdims=True)
    o_ref[...] = (x * lax.rsqrt(ms + NORM_EPS) * w_ref[...]).astype(o_ref.dtype)


def _rmsnorm_stacked(xa, xb, w, out_dtype):
    (ta, d), tb_rows = xa.shape, xb.shape[0]
    bm = _row_block(ta, tb_rows, (256, 128, 64, 32, 16, 8))
    n_first = ta // bm
    spec_a, spec_b = _stacked_specs((bm, d), n_first, 0)
    return pl.pallas_call(
        functools.partial(_rmsnorm_stacked_kernel, n_first=n_first),
        grid=((ta + tb_rows) // bm,),
        in_specs=[spec_a, spec_b, pl.BlockSpec((1, d), lambda i: (0, 0))],
        out_specs=pl.BlockSpec((bm, d), lambda i: (i, 0)),
        out_shape=jax.ShapeDtypeStruct((ta + tb_rows, d), out_dtype),
        compiler_params=_params("parallel"),
        name="rmsnorm_in",
    )(xa, xb, w.reshape(1, d))


def _rmsnorm_t_kernel(x_ref, w_ref, o_ref):
    x = x_ref[...]
    ms = jnp.mean(x * x, axis=-1, keepdims=True)
    o_ref[...] = jnp.transpose(x * lax.rsqrt(ms + NORM_EPS) * w_ref[...]).astype(o_ref.dtype)


def _rmsnorm_t(x, w, out_dtype):
    t, d = x.shape
    bm = _largest_divisor(t, (256, 128))
    return pl.pallas_call(
        _rmsnorm_t_kernel,
        grid=(t // bm,),
        in_specs=[pl.BlockSpec((bm, d), lambda i: (i, 0)),
                  pl.BlockSpec((1, d), lambda i: (0, 0))],
        out_specs=pl.BlockSpec((d, bm), lambda i: (0, i)),
        out_shape=jax.ShapeDtypeStruct((d, t), out_dtype),
        compiler_params=_params("parallel"),
        name="rmsnorm_t",
    )(x, w.reshape(1, d))


def _add_rmsnorm_kernel(a_ref, b_ref, w_ref, o_ref):
    x = a_ref[...] + b_ref[...]
    ms = jnp.mean(x * x, axis=-1, keepdims=True)
    o_ref[...] = x * lax.rsqrt(ms + NORM_EPS) * w_ref[...]


def _add_rmsnorm(a, b, w, row_off, rows):
    d = a.shape[1]
    bm = _row_block(rows, row_off, (256, 128, 64, 32, 16, 8))
    off = row_off // bm
    return pl.pallas_call(
        _add_rmsnorm_kernel,
        grid=(rows // bm,),
        in_specs=[pl.BlockSpec((bm, d), lambda i: (off + i, 0)),
                  pl.BlockSpec((bm, d), lambda i: (off + i, 0)),
                  pl.BlockSpec((1, d), lambda i: (0, 0))],
        out_specs=pl.BlockSpec((bm, d), lambda i: (i, 0)),
        out_shape=jax.ShapeDtypeStruct((rows, d), F32),
        compiler_params=_params("parallel"),
        name="add_rmsnorm",
    )(a, b, w.reshape(1, d))


def _in_proj_kernel(x_ref, wa_ref, wb_ref, ta_ref, tb_ref, o_ref, ta_out_ref, tb_out_ref, wbuf_ref,
                    *, n_plain, shift, chunk):
    ta_out_ref[...] = ta_ref[...].astype(ta_out_ref.dtype)
    tb_out_ref[...] = tb_ref[...].astype(tb_out_ref.dtype)
    j = pl.program_id(0)
    i = pl.program_id(1)
    bn = wa_ref.shape[0]

    @pl.when(jnp.logical_and(i == 0, j < n_plain))
    def _():
        def body(c, carry):
            rows = pl.ds(pl.multiple_of(c * chunk, chunk), chunk)
            wbuf_ref[rows, :] = wa_ref[rows, :].astype(BF16)
            return carry
        lax.fori_loop(0, bn // chunk, body, 0)

    @pl.when(jnp.logical_and(i == 0, j >= n_plain))
    def _():
        def body(c, carry):
            dst = pl.ds(pl.multiple_of(c * chunk, chunk), chunk)
            src = pl.ds(pl.multiple_of(c * chunk + shift, shift), chunk)
            wbuf_ref[dst, :] = wa_ref[src, :].astype(BF16)
            return carry
        lax.fori_loop(0, (bn - shift) // chunk, body, 0)
        wbuf_ref[bn - shift:bn, :] = wb_ref[...].astype(BF16)

    o_ref[...] = lax.dot_general(x_ref[...], wbuf_ref[...], (((1,), (1,)), ((), ())),
                                 preferred_element_type=F32)


def _slab_spec(rows, width, steps, step_of):
    slab = next(r for r in range(16, rows + 1, 16) if rows % r == 0 and rows // r <= steps)
    last = rows // slab - 1
    return pl.BlockSpec((slab, width), lambda *idx: (jnp.minimum(step_of(*idx), last), 0))


def _in_proj(x, w_t, table_a, table_b):
    m, k = x.shape
    bm = _largest_divisor(m, (512, 256, 128, 64, 32, 16, 8))
    bn = 1024
    shift = SSD_HEADS
    assert DT_COL % bn == 0 and PROJ_WIDTH % bn == 0 and bn % shift == 0
    n_i = m // bm
    assert table_b.shape == table_a.shape
    slab_spec = _slab_spec(*table_a.shape, (PROJ_WIDTH // bn) * n_i, lambda j, i: j * n_i + i)
    table_bf16 = jax.ShapeDtypeStruct(table_a.shape, BF16)
    return pl.pallas_call(
        functools.partial(_in_proj_kernel, n_plain=DT_COL // bn, shift=shift, chunk=shift),
        grid=(PROJ_WIDTH // bn, n_i),
        in_specs=[pl.BlockSpec((bm, k), lambda j, i: (i, 0)),
                  pl.BlockSpec((bn, k), lambda j, i: (j, 0)),
                  pl.BlockSpec((shift, k), lambda j, i: ((bn // shift) * (j + 1), 0)),
                  slab_spec, slab_spec],
        out_specs=[pl.BlockSpec((bm, bn), lambda j, i: (i, j)), slab_spec, slab_spec],
        out_shape=[jax.ShapeDtypeStruct((m, PROJ_WIDTH), F32), table_bf16, table_bf16],
        scratch_shapes=[pltpu.VMEM((bn, k), BF16)],
        compiler_params=pltpu.CompilerParams(dimension_semantics=("arbitrary", "arbitrary"),
                                             vmem_limit_bytes=IN_PROJ_VMEM_LIMIT_BYTES),
        name="in_proj",
    )(x, w_t, w_t, table_a, table_b)


def _split3_dot(x, rhs_bf16):
    hi = x.astype(BF16)
    rest = x - hi.astype(F32)
    mid = rest.astype(BF16)
    lo = (rest - mid.astype(F32)).astype(BF16)
    return (jnp.dot(hi, rhs_bf16, preferred_element_type=F32)
            + jnp.dot(mid, rhs_bf16, preferred_element_type=F32)
            + jnp.dot(lo, rhs_bf16, preferred_element_type=F32))


def _dt_kernel(x_ref, w_ref, b_ref, ex_ref, dt_ref, dtx_ref):
    a = lax.dot_general(x_ref[...], w_ref[...], (((1,), (1,)), ((), ())),
                        preferred_element_type=F32) + b_ref[...]
    dt = jnp.maximum(a, 0.0) + jnp.log1p(jnp.exp(-jnp.abs(a)))
    dt_ref[...] = dt
    n = dt.shape[1]
    hi = dt.astype(BF16)
    rest = dt - hi.astype(F32)
    mid = rest.astype(BF16)
    lo = (rest - mid.astype(F32)).astype(BF16)
    dtx_ref[...] = (jnp.dot(jnp.concatenate([hi, mid], axis=1), ex_ref[...], preferred_element_type=F32)
                    + jnp.dot(lo, ex_ref[0:n, :], preferred_element_type=F32))


def _dt_proj(x, w_t, b):
    m, k = x.shape
    n = w_t.shape[0]
    bm = _largest_divisor(m, (512, 256, 128, 64, 32, 16, 8))
    expand = (jnp.arange(SSD_INNER)[None, :] // SSD_HEADDIM == jnp.arange(n)[:, None]).astype(BF16)
    expand = jnp.concatenate([expand, expand], axis=0)
    return pl.pallas_call(
        _dt_kernel,
        grid=(m // bm,),
        in_specs=[pl.BlockSpec((bm, k), lambda i: (i, 0)),
                  pl.BlockSpec((n, k), lambda i: (0, 0)),
                  pl.BlockSpec((1, n), lambda i: (0, 0)),
                  pl.BlockSpec((2 * n, SSD_INNER), lambda i: (0, 0))],
        out_specs=[pl.BlockSpec((bm, n), lambda i: (i, 0)),
                   pl.BlockSpec((bm, SSD_INNER), lambda i: (i, 0))],
        out_shape=[jax.ShapeDtypeStruct((m, n), F32),
                   jax.ShapeDtypeStruct((m, SSD_INNER), F32)],
        compiler_params=_params("parallel"),
        name="dt_proj",
    )(x, w_t, b, expand)


def _merge_kernel(ya1_ref, ya2_ref, yb1_ref, yb2_ref, wa_ref, wb_ref, ga_ref, gb_ref, o_ref, *, n_first):
    i = pl.program_id(1)
    ya = _stacked_block(i, n_first, ya1_ref, ya2_ref)
    yb = _stacked_block(i, n_first, yb1_ref, yb2_ref)
    pa = jnp.dot(ya, wa_ref[...], preferred_element_type=F32)
    pb = jnp.dot(yb, wb_ref[...], preferred_element_type=F32)
    o_ref[...] = (_sigmoid(ga_ref[...]) * pa + _sigmoid(gb_ref[...]) * pb).astype(o_ref.dtype)


def _gated_merge(ya_parts, yb_parts, wa, wb, proj):
    (m1, k), m2 = ya_parts[0].shape, ya_parts[1].shape[0]
    n = wa.shape[1]
    bm = _row_block(m1, m2, (512, 256, 128, 64, 32, 16, 8))
    bn = 512
    n_first = m1 // bm
    y_first, y_second = _stacked_specs((bm, k), n_first, 1)
    return pl.pallas_call(
        functools.partial(_merge_kernel, n_first=n_first),
        grid=(n // bn, (m1 + m2) // bm),
        in_specs=[y_first, y_second, y_first, y_second,
                  pl.BlockSpec((k, bn), lambda j, i: (0, j)),
                  pl.BlockSpec((k, bn), lambda j, i: (0, j)),
                  pl.BlockSpec((bm, bn), lambda j, i: (i, COL_GA // bn + j)),
                  pl.BlockSpec((bm, bn), lambda j, i: (i, COL_GB // bn + j))],
        out_specs=pl.BlockSpec((bm, bn), lambda j, i: (i, j)),
        out_shape=jax.ShapeDtypeStruct((m1 + m2, n), BF16),
        compiler_params=_params("parallel", "parallel"),
        name="gated_merge",
    )(ya_parts[0], ya_parts[1], yb_parts[0], yb_parts[1], wa, wb, proj, proj)


def _residual_matmul_kernel(x_ref, w_ref, r1_ref, r2_ref, o_ref, *, n_first):
    r = _stacked_block(pl.program_id(1), n_first, r1_ref, r2_ref)
    o_ref[...] = r + jnp.dot(x_ref[...], w_ref[...], preferred_element_type=F32)


def _residual_matmul(x, w, r_parts):
    m, k = x.shape
    n = w.shape[1]
    m1, m2 = r_parts[0].shape[0], r_parts[1].shape[0]
    bm = _row_block(m1, m2, (512, 256, 128, 64, 32, 16, 8))
    bn = 1024
    n_first = m1 // bm
    r_first, r_second = _stacked_specs((bm, bn), n_first, 1, col_of=lambda j, i: j)
    return pl.pallas_call(
        functools.partial(_residual_matmul_kernel, n_first=n_first),
        grid=(n // bn, m // bm),
        in_specs=[pl.BlockSpec((bm, k), lambda j, i: (i, 0)),
                  pl.BlockSpec((k, bn), lambda j, i: (0, j)),
                  r_first, r_second],
        out_specs=pl.BlockSpec((bm, bn), lambda j, i: (i, j)),
        out_shape=jax.ShapeDtypeStruct((m, n), F32),
        compiler_params=_params("parallel", "parallel"),
        name="residual_matmul",
    )(x, w, r_parts[0], r_parts[1])


CONV_PAD = 8


def _conv_kernel(u_ref, buf_ref, w_ref, b_ref, o_ref, cache_ref, prev_ref, *, tb):
    t = pl.program_id(2)
    lo = CONV_PAD - (SSD_CONV - 1)

    @pl.when(t == 0)
    def _():
        prev_ref[lo:CONV_PAD, :] = buf_ref[0]

    u = u_ref[...]
    first_row = lax.broadcasted_iota(jnp.int32, (CONV_PAD, u.shape[1]), 0) == 0
    acc = b_ref[...] + u * w_ref[SSD_CONV - 1:SSD_CONV, :]
    shifted = u
    for k in range(1, SSD_CONV):
        shifted = pltpu.roll(shifted, 1, axis=0)
        head = jnp.where(first_row, prev_ref[CONV_PAD - k:CONV_PAD - k + 1, :], shifted[0:CONV_PAD])
        shifted = jnp.concatenate([head, shifted[CONV_PAD:]], axis=0)
        acc = acc + shifted * w_ref[SSD_CONV - 1 - k:SSD_CONV - k, :]
    o_ref[...] = _silu(acc)
    prev_ref[...] = u[tb - CONV_PAD:tb]

    @pl.when(t == pl.num_programs(2) - 1)
    def _():
        cache_ref[0] = prev_ref[lo:CONV_PAD, :]


def _conv_silu(proj, row_off, bt, seqlen, buf, conv_w, conv_b):
    tb = _largest_divisor(seqlen, (512, 256, 128, 64, 32))
    cb = 2048
    nt = seqlen // tb
    roff = row_off // tb
    coff = COL_XBC // cb
    return pl.pallas_call(
        functools.partial(_conv_kernel, tb=tb),
        grid=(SSD_CONV_DIM // cb, bt, nt),
        in_specs=[pl.BlockSpec((tb, cb), lambda c, b, t: (roff + b * nt + t, coff + c)),
                  pl.BlockSpec((1, SSD_CONV - 1, cb), lambda c, b, t: (b, 0, c)),
                  pl.BlockSpec((SSD_CONV, cb), lambda c, b, t: (0, c)),
                  pl.BlockSpec((1, cb), lambda c, b, t: (0, c))],
        out_specs=[pl.BlockSpec((tb, cb), lambda c, b, t: (b * nt + t, c)),
                   pl.BlockSpec((1, SSD_CONV - 1, cb), lambda c, b, t: (b, 0, c))],
        out_shape=[jax.ShapeDtypeStruct((bt * seqlen, SSD_CONV_DIM), F32),
                   jax.ShapeDtypeStruct((bt, SSD_CONV - 1, SSD_CONV_DIM), F32)],
        scratch_shapes=[pltpu.VMEM((CONV_PAD, cb), F32)],
        compiler_params=_params("parallel", "parallel", "arbitrary"),
        name="conv_silu",
    )(proj, buf, conv_w, conv_b.reshape(1, SSD_CONV_DIM))


def _ssd_kernel(x_ref, b_ref, c_ref, z_ref, dtx_ref, dtt_ref, alx_ref, alt_ref, dsk_ref, nw_ref,
                s0_ref, y_ref, sout_ref, st_ref, *, q, nchunk, ng):
    t = pl.program_id(2)
    p, r, gw, n = SSD_HEADDIM, SSD_HEADS_PER_GROUP, SSD_GROUP_WIDTH, SSD_STATE

    @pl.when(t == 0)
    def _():
        st_ref[...] = s0_ref[0]

    row = lax.broadcasted_iota(jnp.int32, (q, q), 0)
    col = lax.broadcasted_iota(jnp.int32, (q, q), 1)
    lower = row >= col
    tril = _tril3(q)
    triu = (row <= col).astype(BF16)
    a_x = -jnp.exp(alx_ref[...])
    a_t = -jnp.exp(alt_ref[...]).reshape(ng * r, 1)
    d_skip = dsk_ref[...]
    norm_w = nw_ref[...]
    nt_dims = (((1,), (1,)), ((), ()))
    tn_dims = (((0,), (0,)), ((), ()))

    def chunk(ci, carry):
        rows = pl.ds(pl.multiple_of(ci * q, q), q)
        x = x_ref[rows, :]
        dtx = dtx_ref[rows, :]
        cum = _cumsum_rows(tril, dtx * a_x)
        cum_t = _split3_dot(dtt_ref[:, ci].reshape(ng * r, q) * a_t, triu)
        cum_last = cum[q - 1:q, :]
        decay_in = jnp.exp(cum)
        xdt = (x * dtx).astype(BF16)
        xt = (x * (jnp.exp(cum_last - cum) * dtx)).astype(BF16)
        decay_st = jnp.exp(cum_last)
        gate = _silu(z_ref[rows, :])
        skip = d_skip * x
        for gi in range(ng):
            cols = slice(gi * gw, (gi + 1) * gw)
            bm = b_ref[rows, gi * n:(gi + 1) * n].astype(BF16)
            cm = c_ref[rows, gi * n:(gi + 1) * n].astype(BF16)
            cb = lax.dot_general(cm, bm, nt_dims, preferred_element_type=F32)
            st = st_ref[gi]
            y = jnp.dot(cm, st.astype(BF16), preferred_element_type=F32) * decay_in[:, cols]
            parts = []
            for h in range(r):
                c0 = gi * gw + h * p
                seg = cum[:, c0:c0 + q] - cum_t[gi * r + h:gi * r + h + 1, :]
                mix = (cb * jnp.where(lower, jnp.exp(seg), 0.0)).astype(BF16)
                parts.append(jnp.dot(mix, xdt[:, c0:c0 + p], preferred_element_type=F32))
            y = (y + jnp.concatenate(parts, axis=1) + skip[:, cols]) * gate[:, cols]
            ms = jnp.mean(y * y, axis=-1, keepdims=True)
            y_ref[rows, cols] = (y * lax.rsqrt(ms + NORM_EPS) * norm_w[:, cols]).astype(y_ref.dtype)
            st_ref[gi] = st * decay_st[:, cols] + lax.dot_general(
                bm, xt[:, cols], tn_dims, preferred_element_type=F32)
        return carry

    lax.fori_loop(0, nchunk, chunk, 0)

    @pl.when(t == pl.num_programs(2) - 1)
    def _():
        sout_ref[0] = st_ref[...]


def _ssd_scan(xconv, proj, dtx, dt, row_off, bt, seqlen, a_log, d_skip, ssd_norm_w, state0):
    g, r, p, n, gw = SSD_GROUPS, SSD_HEADS_PER_GROUP, SSD_HEADDIM, SSD_STATE, SSD_GROUP_WIDTH
    q = min(CHUNK, seqlen)
    ng = SSD_GROUPS_PER_STEP if q == CHUNK else 2 * SSD_GROUPS_PER_STEP
    tb = _largest_divisor(seqlen, (512, 256, 128, 64, 32))
    nchunk = tb // q
    nt = seqlen // tb
    rows = bt * seqlen
    roff = row_off // tb
    dtt = dt[row_off:row_off + rows, :SSD_HEADS].reshape(rows // q, q, g, r).transpose(2, 0, 3, 1)
    alx = jnp.repeat(a_log, p).reshape(1, SSD_INNER)
    alt = a_log.reshape(g, r, 1)
    dsk = jnp.repeat(d_skip, p).reshape(1, SSD_INNER)
    s0 = state0.reshape(bt, g, r, p, n).transpose(0, 1, 4, 2, 3).reshape(bt, g, n, gw)
    bcol = SSD_INNER // (ng * n)
    ccol = bcol + g // ng
    wide = ng * gw
    y, sout = pl.pallas_call(
        functools.partial(_ssd_kernel, q=q, nchunk=nchunk, ng=ng),
        grid=(bt, g // ng, nt),
        in_specs=[pl.BlockSpec((tb, wide), lambda b, gi, t: (b * nt + t, gi)),
                  pl.BlockSpec((tb, ng * n), lambda b, gi, t: (b * nt + t, bcol + gi)),
                  pl.BlockSpec((tb, ng * n), lambda b, gi, t: (b * nt + t, ccol + gi)),
                  pl.BlockSpec((tb, wide), lambda b, gi, t: (roff + b * nt + t, COL_Z // wide + gi)),
                  pl.BlockSpec((tb, wide), lambda b, gi, t: (roff + b * nt + t, gi)),
                  pl.BlockSpec((ng, nchunk, r, q), lambda b, gi, t: (gi, b * nt + t, 0, 0)),
                  pl.BlockSpec((1, wide), lambda b, gi, t: (0, gi)),
                  pl.BlockSpec((ng, r, 1), lambda b, gi, t: (gi, 0, 0)),
                  pl.BlockSpec((1, wide), lambda b, gi, t: (0, gi)),
                  pl.BlockSpec((1, wide), lambda b, gi, t: (0, gi)),
                  pl.BlockSpec((1, ng, n, gw), lambda b, gi, t: (b, gi, 0, 0))],
        out_specs=[pl.BlockSpec((tb, wide), lambda b, gi, t: (b * nt + t, gi)),
                   pl.BlockSpec((1, ng, n, gw), lambda b, gi, t: (b, gi, 0, 0))],
        out_shape=[jax.ShapeDtypeStruct((rows, SSD_INNER), BF16),
                   jax.ShapeDtypeStruct((bt, g, n, gw), F32)],
        scratch_shapes=[pltpu.VMEM((ng, n, gw), F32)],
        compiler_params=_params("parallel", "parallel", "arbitrary"),
        name="ssd_scan",
    )(xconv, xconv, xconv, proj, dtx, dtt, alx, alt, dsk, ssd_norm_w.reshape(1, SSD_INNER), s0)
    state = sout.reshape(bt, g, n, r, p).transpose(0, 1, 3, 4, 2).reshape(bt, SSD_HEADS, p, n)
    return y, state


def _tril3(q):
    row = lax.broadcasted_iota(jnp.int32, (q, 3 * q), 0)
    col = lax.broadcasted_iota(jnp.int32, (q, 3 * q), 1)
    col = jnp.where(col >= 2 * q, col - 2 * q, jnp.where(col >= q, col - q, col))
    return (row >= col).astype(BF16)


def _cumsum_rows(tril3_bf16, x):
    hi = x.astype(BF16)
    rest = x - hi.astype(F32)
    mid = rest.astype(BF16)
    lo = (rest - mid.astype(F32)).astype(BF16)
    return jnp.dot(tril3_bf16, jnp.concatenate([hi, mid, lo], axis=0), preferred_element_type=F32)


def _hgrn_kernel(q_ref, f_ref, i_ref, g_ref, lbt_ref, nw_ref, s0_ref, o_ref, sout_ref, st_ref,
                 *, q, nchunk, nh):
    t = pl.program_id(2)

    @pl.when(t == 0)
    def _():
        st_ref[...] = s0_ref[0]

    hq = q // 2
    row = lax.broadcasted_iota(jnp.int32, (q, q), 0)
    col = lax.broadcasted_iota(jnp.int32, (q, q), 1)
    lower = row >= col
    tril = _tril3(q)
    left_half = lax.broadcasted_iota(jnp.int32, (hq, LANES), 1) < hq
    k_pad = jnp.zeros((LANES - 3 * hq, nh * HG_K), BF16)
    tab = lbt_ref[...]
    e = jnp.exp(tab - jnp.max(tab, axis=0, keepdims=True))
    lb = e[0:1, :] / jnp.sum(e, axis=0, keepdims=True)
    norm_w = nw_ref[...]
    nt_dims = (((1,), (1,)), ((), ()))
    tn_dims = (((0,), (0,)), ((), ()))

    def decayed(x, cum_rows, ref_row, sign):
        return (x * jnp.exp(sign * (cum_rows - ref_row))).astype(BF16)

    def chunk(ci, carry):
        rows = pl.ds(pl.multiple_of(ci * q, q), q)
        qq = _silu(q_ref[rows, :]) * (HG_K ** -0.5)
        f = lb + (1.0 - lb) * _sigmoid(f_ref[rows, :])
        kk = 1.0 - f
        v = i_ref[rows, :].astype(BF16)
        gate = _silu(g_ref[rows, :])
        cum = _cumsum_rows(tril, jnp.log(f))
        top, bot = slice(0, hq), slice(hq, q)
        mid_top = cum[hq // 2 - 1:hq // 2, :]
        mid_bot = cum[hq + hq // 2 - 1:hq + hq // 2, :]
        edge = cum[hq - 1:hq, :]
        cum_last = cum[q - 1:q, :]
        q_tt = decayed(qq[top], cum[top], mid_top, 1.0)
        k_tt = decayed(kk[top], cum[top], mid_top, -1.0)
        q_bb = decayed(qq[bot], cum[bot], mid_bot, 1.0)
        k_bb = decayed(kk[bot], cum[bot], mid_bot, -1.0)
        q_bt = decayed(qq[bot], cum[bot], edge, 1.0)
        k_bt = decayed(kk[top], cum[top], edge, -1.0)
        qe = (qq * jnp.exp(cum)).astype(BF16)
        kt = (kk * jnp.exp(cum_last - cum)).astype(BF16)
        dec = jnp.exp(cum_last)
        q_stack = jnp.concatenate([q_tt, q_bb, q_bt], axis=0)
        k_stack = jnp.concatenate([k_tt, k_bb, k_bt, k_pad], axis=0)
        for h in range(nh):
            sl = slice(h * HG_K, (h + 1) * HG_K)
            s_all = lax.dot_general(q_stack[:, sl], k_stack[:, sl], nt_dims,
                                    preferred_element_type=F32)
            bottom_left = pltpu.roll(s_all[2 * hq:3 * hq, :], LANES - 2 * hq, axis=1)
            s_bot = jnp.where(left_half, bottom_left, s_all[hq:2 * hq, :])
            att = jnp.concatenate([s_all[0:hq, 0:q], s_bot[:, 0:q]], axis=0)
            att = jnp.where(lower, att, 0.0).astype(BF16)
            st = st_ref[h]
            o = jnp.dot(att, v[:, sl], preferred_element_type=F32) + lax.dot_general(
                qe[:, sl], st.astype(BF16), nt_dims, preferred_element_type=F32)
            ms = jnp.mean(o * o, axis=-1, keepdims=True)
            o_ref[rows, sl] = (o * lax.rsqrt(ms + NORM_EPS) * norm_w * gate[:, sl]).astype(o_ref.dtype)
            st_ref[h] = st * dec[:, sl] + lax.dot_general(v[:, sl], kt[:, sl], tn_dims,
                                                          preferred_element_type=F32)
        return carry

    lax.fori_loop(0, nchunk, chunk, 0)

    @pl.when(t == pl.num_programs(2) - 1)
    def _():
        sout_ref[0] = st_ref[...]


def _hgrn_scan(proj, row_off, bt, seqlen, lb_table, hg_norm_w, state0):
    q = min(CHUNK, seqlen)
    tb = _largest_divisor(seqlen, (512, 256, 128, 64, 32))
    nchunk = tb // q
    nt = seqlen // tb
    rows = bt * seqlen
    roff = row_off // tb
    nh = HGRN_HEADS_PER_STEP if q == CHUNK else 2 * HGRN_HEADS_PER_STEP
    hw = nh * HG_K
    s0 = state0.transpose(0, 1, 3, 2)
    nrow = lb_table.shape[0]

    def col_spec(col0):
        return pl.BlockSpec((tb, hw), lambda b, h, t: (roff + b * nt + t, col0 // hw + h))

    o, sout = pl.pallas_call(
        functools.partial(_hgrn_kernel, q=q, nchunk=nchunk, nh=nh),
        grid=(bt, HG_HEADS // nh, nt),
        in_specs=[col_spec(COL_HQ), col_spec(COL_HF), col_spec(COL_HI), col_spec(COL_HG),
                  pl.BlockSpec((nrow, hw), lambda b, h, t: (0, h)),
                  pl.BlockSpec((1, HG_V), lambda b, h, t: (0, 0)),
                  pl.BlockSpec((1, nh, HG_V, HG_K), lambda b, h, t: (b, h, 0, 0))],
        out_specs=[pl.BlockSpec((tb, hw), lambda b, h, t: (b * nt + t, h)),
                   pl.BlockSpec((1, nh, HG_V, HG_K), lambda b, h, t: (b, h, 0, 0))],
        out_shape=[jax.ShapeDtypeStruct((rows, HG_HEADS * HG_V), BF16),
                   jax.ShapeDtypeStruct((bt, HG_HEADS, HG_V, HG_K), F32)],
        scratch_shapes=[pltpu.VMEM((nh, HG_V, HG_K), F32)],
        compiler_params=_params("parallel", "parallel", "arbitrary"),
        name="hgrn_scan",
    )(proj, proj, proj, proj, lb_table, hg_norm_w.reshape(1, HG_V), s0)
    return o, sout.transpose(0, 1, 3, 2)


def _top_values(s, count):
    vals = []
    work = s
    for _ in range(count):
        m = jnp.max(work, axis=0, keepdims=True)
        vals.append(m)
        work = jnp.where(work == m, -jnp.inf, work)
    return jnp.concatenate(vals, axis=0)


def _oddeven_merge(lo, hi, r):
    step = r * 2
    if step < hi - lo:
        yield from _oddeven_merge(lo, hi, step)
        yield from _oddeven_merge(lo + r, hi, step)
        yield from [(i, i + r) for i in range(lo + r, hi - r, step)]
    else:
        yield (lo, lo + r)


def _oddeven_merge_sort(lo, hi):
    if hi - lo >= 1:
        mid = lo + (hi - lo) // 2
        yield from _oddeven_merge_sort(lo, mid)
        yield from _oddeven_merge_sort(mid + 1, hi)
        yield from _oddeven_merge(lo, hi, 1)


SUBLANES = 8
_SORT_NETWORK = tuple(_oddeven_merge_sort(0, PEER_KEYS // SUBLANES - 1))


def _compare_exchange(v, i, j):
    v[i], v[j] = jnp.maximum(v[i], v[j]), jnp.minimum(v[i], v[j])


def _top_sorted(s):
    n = PEER_KEYS // SUBLANES
    assert n == PEER_TOPK
    x = s.reshape(n, SUBLANES, s.shape[-1])
    v = [x[k] for k in range(n)]
    for i, j in _SORT_NETWORK:
        _compare_exchange(v, i, j)
    shift = 1
    while shift < SUBLANES:
        w = [pltpu.roll(a, shift, axis=0) for a in v]
        v = [jnp.maximum(v[k], w[n - 1 - k]) for k in range(n)]
        d = n // 2
        while d >= 1:
            for k in range(n):
                if not k & d:
                    _compare_exchange(v, k, k + d)
            d //= 2
        shift *= 2
    return jnp.concatenate([a[0:1] for a in v], axis=0)


def _peer_query_kernel(w_ref, x_ref, sk_ref, thr_ref, e0_ref, s1_ref, e1_ref):
    half = PEER_QDIM // 2
    nh = thr_ref.shape[0]
    qt = jnp.dot(w_ref[...], x_ref[...], preferred_element_type=F32)

    def split(a):
        hi = a.astype(BF16)
        return hi, (a - hi.astype(F32)).astype(BF16)

    def scores(keys, q):
        (k_hi, k_lo), (q_hi, q_lo) = keys, split(q)
        return (jnp.dot(k_hi, q_hi, preferred_element_type=F32)
                + jnp.dot(k_hi, q_lo, preferred_element_type=F32)
                + jnp.dot(k_lo, q_hi, preferred_element_type=F32))

    keys0, keys1 = split(sk_ref[0]), split(sk_ref[1])
    for h in range(nh):
        s0 = scores(keys0, qt[h * PEER_QDIM:h * PEER_QDIM + half])
        s1 = scores(keys1, qt[h * PEER_QDIM + half:(h + 1) * PEER_QDIM])
        sv0 = _top_sorted(s0)
        sv1 = _top_sorted(s1)
        cand = [sv0[0:1] + sv1]
        for a in range(1, 8):
            cand.append(sv0[a:a + 1] + sv1[0:8])
        cand.append(sv0[8:16] + sv1[0:1])
        top = _top_values(jnp.concatenate(cand, axis=0), PEER_TOPK)
        z = jnp.sum(jnp.exp(top - top[0:1]), axis=0, keepdims=True)
        tau = top[PEER_TOPK - 1:PEER_TOPK]
        thr = jnp.full(s0.shape, jnp.inf, F32)
        for b in range(PEER_TOPK):
            thr = jnp.where(s0 + sv1[b:b + 1] >= tau, sv1[b:b + 1], thr)
        thr_ref[h] = thr.reshape(thr_ref.shape[1:])
        s1_ref[h] = s1
        e0_ref[h] = (jnp.exp(s0 - sv0[0:1]) / z).reshape(e0_ref.shape[1:])
        e1_ref[h] = jnp.exp(s1 - sv1[0:1])


def _peer_query(w_pq_t, xn_t, sub_keys):
    d, t = xn_t.shape
    tb = _largest_divisor(t, (256, 128))
    nh = PEER_QUERY_HEADS_PER_STEP
    keyed = jax.ShapeDtypeStruct((PEER_HEADS, PEER_KEYS, t), F32)
    key_spec = pl.BlockSpec((nh, PEER_KEYS, tb), lambda i, h: (h, 0, i))
    nb = PEER_EXPERT_KEY_ROWS
    grouped = jax.ShapeDtypeStruct((PEER_HEADS, PEER_KEYS // nb, nb, t), F32)
    group_spec = pl.BlockSpec((nh, PEER_KEYS // nb, nb, tb), lambda i, h: (h, 0, 0, i))
    return pl.pallas_call(
        _peer_query_kernel,
        grid=(t // tb, PEER_HEADS // nh),
        in_specs=[pl.BlockSpec((nh * PEER_QDIM, d), lambda i, h: (h, 0)),
                  pl.BlockSpec((d, tb), lambda i, h: (0, i)),
                  pl.BlockSpec((2, PEER_KEYS, PEER_QDIM // 2), lambda i, h: (0, 0, 0))],
        out_specs=[group_spec, group_spec, key_spec, key_spec],
        out_shape=[grouped, grouped, keyed, keyed],
        compiler_params=_params("parallel", "arbitrary"),
        name="peer_query",
    )(w_pq_t, xn_t, sub_keys)


def _gelu_tanh(x):
    return 0.5 * x * (1.0 + jnp.tanh(0.7978845608028654 * (x + 0.044715 * (x * x * x))))


def _peer_expert_kernel(x_ref, u_ref, v_ref, thr_ref, e0_ref, s1_ref, e1_ref, o_ref, w_ref, *, nb, tb):
    e = pl.program_id(1)

    @pl.when(e == 0)
    def _():
        o_ref[...] = jnp.zeros_like(o_ref)

    act = _gelu_tanh(jnp.dot(u_ref[...], x_ref[...], preferred_element_type=F32))
    for i in range(nb):
        for tl in range(tb // LANES):
            lanes = pl.ds(tl * LANES, LANES)
            acc = jnp.zeros((PEER_KEYS, LANES), F32)
            for h in range(PEER_HEADS):
                thr = thr_ref[h, 0, i:i + 1, lanes]
                c0 = e0_ref[h, 0, i:i + 1, lanes]
                acc = acc + jnp.where(s1_ref[h, :, lanes] >= thr, e1_ref[h, :, lanes] * c0, 0.0)
            w_ref[i * PEER_KEYS:(i + 1) * PEER_KEYS, lanes] = acc
    g = (w_ref[...] * act).astype(BF16)
    o_ref[...] += lax.dot_general(g, v_ref[...], (((0,), (0,)), ((), ())),
                                  preferred_element_type=F32)


def _peer_experts(xn_t, u, v, thr, e0, s1, e1):
    d, t = xn_t.shape
    n_exp = u.shape[0]
    tb = _largest_divisor(t, (512, 256, 128))
    nb = PEER_EXPERT_KEY_ROWS
    eb = nb * PEER_KEYS
    once = pl.Buffered(1)
    keyed = pl.BlockSpec((PEER_HEADS, PEER_KEYS, tb), lambda i, e: (0, 0, i), pipeline_mode=once)
    rows = pl.BlockSpec((PEER_HEADS, 1, nb, tb), lambda i, e: (0, e, 0, i))
    return pl.pallas_call(
        functools.partial(_peer_expert_kernel, nb=nb, tb=tb),
        grid=(t // tb, n_exp // eb),
        in_specs=[pl.BlockSpec((d, tb), lambda i, e: (0, i), pipeline_mode=once),
                  pl.BlockSpec((eb, d), lambda i, e: (e, 0)),
                  pl.BlockSpec((eb, d), lambda i, e: (e, 0)),
                  rows, rows, keyed, keyed],
        out_specs=pl.BlockSpec((tb, d), lambda i, e: (i, 0)),
        out_shape=jax.ShapeDtypeStruct((t, d), F32),
        scratch_shapes=[pltpu.VMEM((eb, tb), F32)],
        compiler_params=_params("parallel", "arbitrary"),
        name="peer_experts",
    )(xn_t, u, v, thr, e0, s1, e1)


def _branches(proj, dtx, dt, row_off, bt, seqlen, conv_buf, s_ssd, s_hg, conv_w, conv_b, a_log, d_skip,
              ssd_norm_w, lb_table, hg_norm_w):
    xconv, new_conv = _conv_silu(proj, row_off, bt, seqlen, conv_buf, conv_w, conv_b)
    y_ssd, s_ssd_new = _ssd_scan(xconv, proj, dtx, dt, row_off, bt, seqlen,
                                 a_log, d_skip, ssd_norm_w, s_ssd)
    o_hg, s_hg_new = _hgrn_scan(proj, row_off, bt, seqlen, lb_table, hg_norm_w, s_hg)
    return y_ssd, o_hg, s_ssd_new, new_conv, s_hg_new


def kernel(x_prompt, x_sample, state_ssd, cache_ssd_conv, state_hgrn, lb_table, norm1_w, w_in, conv_w, conv_b, dt_bias, a_log, d_skip, ssd_norm_w, w_ssd_out, hg_norm_w, w_hg_out, w_o, norm2_w, w_pq, sub_keys, u_experts, v_experts, final_norm_w):
    assert w_in.shape[0] == 1 and lb_table.shape[0] == 2, "single-layer stack"
    bp, lp, d = x_prompt.shape
    bs, ls, _ = x_sample.shape
    tp, ts = bp * lp, bs * ls
    x_parts = (x_prompt.reshape(tp, d), x_sample.reshape(ts, d))

    w_in_t = w_in[0].T
    w_dt = jnp.pad(w_in_t[DT_COL:DT_COL + SSD_HEADS], ((0, LANES - SSD_HEADS), (0, 0))).astype(BF16)
    b_dt = jnp.pad(dt_bias[0], (0, LANES - SSD_HEADS)).reshape(1, LANES)

    xn = _rmsnorm_stacked(*x_parts, norm1_w[0], BF16)
    proj, u_bf16, v_bf16 = _in_proj(xn, w_in_t, u_experts[0], v_experts[0])
    dt, dtx = _dt_proj(xn, w_dt, b_dt)

    zeros = functools.partial(jnp.zeros, dtype=F32)
    common = (conv_w[0], conv_b[0], a_log[0], d_skip[0], ssd_norm_w[0], lb_table, hg_norm_w[0])
    ya_p, yb_p, ssd_p, conv_p, hg_p = _branches(
        proj, dtx, dt, 0, bp, lp,
        zeros((bp, SSD_CONV - 1, SSD_CONV_DIM)), zeros((bp, SSD_HEADS, SSD_HEADDIM, SSD_STATE)),
        zeros((bp, HG_HEADS, HG_K, HG_V)), *common)
    ya_s, yb_s, ssd_s, conv_s, hg_s = _branches(
        proj, dtx, dt, tp, bs, ls, cache_ssd_conv[0], state_ssd[0], state_hgrn[0], *common)

    merged = _gated_merge((ya_p, ya_s), (yb_p, yb_s), w_ssd_out[0].astype(BF16),
                          w_hg_out[0].astype(BF16), proj)
    h = _residual_matmul(merged, w_o[0].astype(BF16), x_parts)

    hn_t = _rmsnorm_t(h, norm2_w[0], BF16)
    thr, e0, s1, e1 = _peer_query(w_pq[0].T.astype(BF16), hn_t, sub_keys[0])
    peer = _peer_experts(hn_t, u_bf16, v_bf16, thr, e0, s1, e1)
    y_prompt = _add_rmsnorm(h, peer, final_norm_w, 0, tp).reshape(bp, lp, d)
    y_sample = _add_rmsnorm(h, peer, final_norm_w, tp, ts).reshape(bs, ls, d)
    return (y_prompt, y_sample, ssd_p[None], conv_p[None], hg_p[None],
            ssd_s[None], conv_s[None], hg_s[None])
```

```python
import functools

import jax
import jax.numpy as jnp
from jax import lax
from jax.experimental import pallas as pl
from jax.experimental.pallas import tpu as pltpu

F32 = jnp.float32
BF16 = jnp.bfloat16

NORM_EPS = 1e-6
LANES = 128
VMEM_LIMIT_BYTES = 56 * 1024 * 1024
IN_PROJ_VMEM_LIMIT_BYTES = 62 * 1024 * 1024

D_MODEL = 4096
SSD_GROUPS = 8
SSD_HEADS_PER_GROUP = 8
SSD_HEADDIM = 64
SSD_STATE = 128
SSD_HEADS = SSD_GROUPS * SSD_HEADS_PER_GROUP
SSD_INNER = SSD_HEADS * SSD_HEADDIM
SSD_GROUP_WIDTH = SSD_HEADS_PER_GROUP * SSD_HEADDIM
SSD_CONV = 4
SSD_CONV_DIM = SSD_INNER + 2 * SSD_GROUPS * SSD_STATE
HG_HEADS = 32
HG_K = 128
HG_V = 128
CHUNK = 64
PEER_HEADS = 8
PEER_KEYS = 128
PEER_TOPK = 16
PEER_QDIM = 256
PEER_EXPERT_KEY_ROWS = 4
PEER_QUERY_HEADS_PER_STEP = 4
SCAN_UNROLL = 2
SSD_GROUPS_PER_STEP = 4
HGRN_HEADS_PER_STEP = 8

COL_Z = 0
COL_XBC = SSD_INNER
COL_HQ = COL_XBC + SSD_CONV_DIM
COL_HF = COL_HQ + HG_HEADS * HG_K
COL_HI = COL_HF + HG_HEADS * HG_K
COL_HG = COL_HI + HG_HEADS * HG_V
COL_GA = COL_HG + HG_HEADS * HG_V
COL_GB = COL_GA + D_MODEL
PROJ_WIDTH = COL_GB + D_MODEL
DT_COL = SSD_INNER + SSD_CONV_DIM


def _largest_divisor(n, candidates):
    for c in candidates:
        if n % c == 0:
            return c
    raise ValueError(f"no block size among {candidates} divides {n}")


def _params(*semantics):
    return pltpu.CompilerParams(dimension_semantics=semantics,
                                vmem_limit_bytes=VMEM_LIMIT_BYTES)


def _sigmoid(x):
    return 1.0 / (1.0 + jnp.exp(-x))


def _silu(x):
    return x * _sigmoid(x)


def _stacked_specs(block, n_first, row_pos, col_of=None):
    def first(*idx):
        return (jnp.minimum(idx[row_pos], n_first - 1), 0 if col_of is None else col_of(*idx))

    def second(*idx):
        return (jnp.maximum(idx[row_pos] - n_first, 0), 0 if col_of is None else col_of(*idx))

    return pl.BlockSpec(block, first), pl.BlockSpec(block, second)


def _stacked_block(i, n_first, first_ref, second_ref):
    return jnp.where(i < n_first, first_ref[...], second_ref[...])


def _row_block(n_first_rows, n_second_rows, candidates):
    for c in candidates:
        if n_first_rows % c == 0 and n_second_rows % c == 0:
            return c
    raise ValueError(f"no block size among {candidates} divides {n_first_rows} and {n_second_rows}")


def _rmsnorm_stacked_kernel(xa_ref, xb_ref, w_ref, o_ref, *, n_first):
    x = _stacked_block(pl.program_id(0), n_first, xa_ref, xb_ref)
    ms = jnp.mean(x * x, axis=-1, keepdims=True)
    o_ref[...] = (x * lax.rsqrt(ms + NORM_EPS) * w_ref[...]).astype(o_ref.dtype)


def _rmsnorm_stacked(xa, xb, w, out_dtype):
    (ta, d), tb_rows = xa.shape, xb.shape[0]
    bm = _row_block(ta, tb_rows, (256, 128, 64, 32, 16, 8))
    n_first = ta // bm
    spec_a, spec_b = _stacked_specs((bm, d), n_first, 0)
    return pl.pallas_call(
        functools.partial(_rmsnorm_stacked_kernel, n_first=n_first),
        grid=((ta + tb_rows) // bm,),
        in_specs=[spec_a, spec_b, pl.BlockSpec((1, d), lambda i: (0, 0))],
        out_specs=pl.BlockSpec((bm, d), lambda i: (i, 0)),
        out_shape=jax.ShapeDtypeStruct((ta + tb_rows, d), out_dtype),
        compiler_params=_params("parallel"),
        name="rmsnorm_in",
    )(xa, xb, w.reshape(1, d))


def _rmsnorm_t_kernel(x_ref, w_ref, o_ref):
    x = x_ref[...]
    ms = jnp.mean(x * x, axis=-1, keepdims=True)
    o_ref[...] = jnp.transpose(x * lax.rsqrt(ms + NORM_EPS) * w_ref[...]).astype(o_ref.dtype)


def _rmsnorm_t(x, w, out_dtype):
    t, d = x.shape
    bm = _largest_divisor(t, (256, 128))
    return pl.pallas_call(
        _rmsnorm_t_kernel,
        grid=(t // bm,),
        in_specs=[pl.BlockSpec((bm, d), lambda i: (i, 0)),
                  pl.BlockSpec((1, d), lambda i: (0, 0))],
        out_specs=pl.BlockSpec((d, bm), lambda i: (0, i)),
        out_shape=jax.ShapeDtypeStruct((d, t), out_dtype),
        compiler_params=_params("parallel"),
        name="rmsnorm_t",
    )(x, w.reshape(1, d))


def _add_rmsnorm_kernel(a_ref, b_ref, w_ref, o_ref):
    x = a_ref[...] + b_ref[...]
    ms = jnp.mean(x * x, axis=-1, keepdims=True)
    o_ref[...] = x * lax.rsqrt(ms + NORM_EPS) * w_ref[...]


def _add_rmsnorm(a, b, w, row_off, rows):
    d = a.shape[1]
    bm = _row_block(rows, row_off, (256, 128, 64, 32, 16, 8))
    off = row_off // bm
    return pl.pallas_call(
        _add_rmsnorm_kernel,
        grid=(rows // bm,),
        in_specs=[pl.BlockSpec((bm, d), lambda i: (off + i, 0)),
                  pl.BlockSpec((bm, d), lambda i: (off + i, 0)),
                  pl.BlockSpec((1, d), lambda i: (0, 0))],
        out_specs=pl.BlockSpec((bm, d), lambda i: (i, 0)),
        out_shape=jax.ShapeDtypeStruct((rows, d), F32),
        compiler_params=_params("parallel"),
        name="add_rmsnorm",
    )(a, b, w.reshape(1, d))


def _in_proj_kernel(x_ref, wa_ref, wb_ref, ta_ref, tb_ref, o_ref, ta_out_ref, tb_out_ref, wbuf_ref,
                    *, n_plain, shift, chunk):
    ta_out_ref[...] = ta_ref[...].astype(ta_out_ref.dtype)
    tb_out_ref[...] = tb_ref[...].astype(tb_out_ref.dtype)
    j = pl.program_id(0)
    i = pl.program_id(1)
    bn = wa_ref.shape[0]

    @pl.when(jnp.logical_and(i == 0, j < n_plain))
    def _():
        def body(c, carry):
            rows = pl.ds(pl.multiple_of(c * chunk, chunk), chunk)
            wbuf_ref[rows, :] = wa_ref[rows, :].astype(BF16)
            return carry
        lax.fori_loop(0, bn // chunk, body, 0)

    @pl.when(jnp.logical_and(i == 0, j >= n_plain))
    def _():
        def body(c, carry):
            dst = pl.ds(pl.multiple_of(c * chunk, chunk), chunk)
            src = pl.ds(pl.multiple_of(c * chunk + shift, shift), chunk)
            wbuf_ref[dst, :] = wa_ref[src, :].astype(BF16)
            return carry
        lax.fori_loop(0, (bn - shift) // chunk, body, 0)
        wbuf_ref[bn - shift:bn, :] = wb_ref[...].astype(BF16)

    o_ref[...] = lax.dot_general(x_ref[...], wbuf_ref[...], (((1,), (1,)), ((), ())),
                                 preferred_element_type=F32)


def _slab_spec(rows, width, steps, step_of):
    slab = next(r for r in range(16, rows + 1, 16) if rows % r == 0 and rows // r <= steps)
    last = rows // slab - 1
    return pl.BlockSpec((slab, width), lambda *idx: (jnp.minimum(step_of(*idx), last), 0))


def _in_proj(x, w_t, table_a, table_b):
    m, k = x.shape
    bm = _largest_divisor(m, (512, 256, 128, 64, 32, 16, 8))
    bn = 1024
    shift = SSD_HEADS
    assert DT_COL % bn == 0 and PROJ_WIDTH % bn == 0 and bn % shift == 0
    n_i = m // bm
    assert table_b.shape == table_a.shape
    slab_spec = _slab_spec(*table_a.shape, (PROJ_WIDTH // bn) * n_i, lambda j, i: j * n_i + i)
    table_bf16 = jax.ShapeDtypeStruct(table_a.shape, BF16)
    return pl.pallas_call(
        functools.partial(_in_proj_kernel, n_plain=DT_COL // bn, shift=shift, chunk=shift),
        grid=(PROJ_WIDTH // bn, n_i),
        in_specs=[pl.BlockSpec((bm, k), lambda j, i: (i, 0)),
                  pl.BlockSpec((bn, k), lambda j, i: (j, 0)),
                  pl.BlockSpec((shift, k), lambda j, i: ((bn // shift) * (j + 1), 0)),
                  slab_spec, slab_spec],
        out_specs=[pl.BlockSpec((bm, bn), lambda j, i: (i, j)), slab_spec, slab_spec],
        out_shape=[jax.ShapeDtypeStruct((m, PROJ_WIDTH), F32), table_bf16, table_bf16],
        scratch_shapes=[pltpu.VMEM((bn, k), BF16)],
        compiler_params=pltpu.CompilerParams(dimension_semantics=("arbitrary", "arbitrary"),
                                             vmem_limit_bytes=IN_PROJ_VMEM_LIMIT_BYTES),
        name="in_proj",
    )(x, w_t, w_t, table_a, table_b)


def _split3_dot(x, rhs_bf16):
    hi = x.astype(BF16)
    rest = x - hi.astype(F32)
    mid = rest.astype(BF16)
    lo = (rest - mid.astype(F32)).astype(BF16)
    return (jnp.dot(hi, rhs_bf16, preferred_element_type=F32)
            + jnp.dot(mid, rhs_bf16, preferred_element_type=F32)
            + jnp.dot(lo, rhs_bf16, preferred_element_type=F32))


def _dt_kernel(x_ref, w_ref, b_ref, ex_ref, dt_ref, dtx_ref):
    a = lax.dot_general(x_ref[...], w_ref[...], (((1,), (1,)), ((), ())),
                        preferred_element_type=F32) + b_ref[...]
    dt = jnp.maximum(a, 0.0) + jnp.log1p(jnp.exp(-jnp.abs(a)))
    dt_ref[...] = dt
    n = dt.shape[1]
    hi = dt.astype(BF16)
    rest = dt - hi.astype(F32)
    mid = rest.astype(BF16)
    lo = (rest - mid.astype(F32)).astype(BF16)
    dtx_ref[...] = (jnp.dot(jnp.concatenate([hi, mid], axis=1), ex_ref[...], preferred_element_type=F32)
                    + jnp.dot(lo, ex_ref[0:n, :], preferred_element_type=F32))


def _dt_proj(x, w_t, b):
    m, k = x.shape
    n = w_t.shape[0]
    bm = _largest_divisor(m, (512, 256, 128, 64, 32, 16, 8))
    expand = (jnp.arange(SSD_INNER)[None, :] // SSD_HEADDIM == jnp.arange(n)[:, None]).astype(BF16)
    expand = jnp.concatenate([expand, expand], axis=0)
    return pl.pallas_call(
        _dt_kernel,
        grid=(m // bm,),
        in_specs=[pl.BlockSpec((bm, k), lambda i: (i, 0)),
                  pl.BlockSpec((n, k), lambda i: (0, 0)),
                  pl.BlockSpec((1, n), lambda i: (0, 0)),
                  pl.BlockSpec((2 * n, SSD_INNER), lambda i: (0, 0))],
        out_specs=[pl.BlockSpec((bm, n), lambda i: (i, 0)),
                   pl.BlockSpec((bm, SSD_INNER), lambda i: (i, 0))],
        out_shape=[jax.ShapeDtypeStruct((m, n), F32),
                   jax.ShapeDtypeStruct((m, SSD_INNER), F32)],
        compiler_params=_params("parallel"),
        name="dt_proj",
    )(x, w_t, b, expand)


def _merge_kernel(ya1_ref, ya2_ref, yb1_ref, yb2_ref, wa_ref, wb_ref, ga_ref, gb_ref, o_ref, *, n_first):
    i = pl.program_id(1)
    ya = _stacked_block(i, n_first, ya1_ref, ya2_ref)
    yb = _stacked_block(i, n_first, yb1_ref, yb2_ref)
    pa = jnp.dot(ya, wa_ref[...], preferred_element_type=F32)
    pb = jnp.dot(yb, wb_ref[...], preferred_element_type=F32)
    o_ref[...] = (_sigmoid(ga_ref[...]) * pa + _sigmoid(gb_ref[...]) * pb).astype(o_ref.dtype)


def _gated_merge(ya_parts, yb_parts, wa, wb, proj):
    (m1, k), m2 = ya_parts[0].shape, ya_parts[1].shape[0]
    n = wa.shape[1]
    bm = _row_block(m1, m2, (512, 256, 128, 64, 32, 16, 8))
    bn = 512
    n_first = m1 // bm
    y_first, y_second = _stacked_specs((bm, k), n_first, 1)
    return pl.pallas_call(
        functools.partial(_merge_kernel, n_first=n_first),
        grid=(n // bn, (m1 + m2) // bm),
        in_specs=[y_first, y_second, y_first, y_second,
                  pl.BlockSpec((k, bn), lambda j, i: (0, j)),
                  pl.BlockSpec((k, bn), lambda j, i: (0, j)),
                  pl.BlockSpec((bm, bn), lambda j, i: (i, COL_GA // bn + j)),
                  pl.BlockSpec((bm, bn), lambda j, i: (i, COL_GB // bn + j))],
        out_specs=pl.BlockSpec((bm, bn), lambda j, i: (i, j)),
        out_shape=jax.ShapeDtypeStruct((m1 + m2, n), BF16),
        compiler_params=_params("parallel", "parallel"),
        name="gated_merge",
    )(ya_parts[0], ya_parts[1], yb_parts[0], yb_parts[1], wa, wb, proj, proj)


def _residual_matmul_kernel(x_ref, w_ref, r1_ref, r2_ref, o_ref, *, n_first):
    r = _stacked_block(pl.program_id(1), n_first, r1_ref, r2_ref)
    o_ref[...] = r + jnp.dot(x_ref[...], w_ref[...], preferred_element_type=F32)


def _residual_matmul(x, w, r_parts):
    m, k = x.shape
    n = w.shape[1]
    m1, m2 = r_parts[0].shape[0], r_parts[1].shape[0]
    bm = _row_block(m1, m2, (512, 256, 128, 64, 32, 16, 8))
    bn = 1024
    n_first = m1 // bm
    r_first, r_second = _stacked_specs((bm, bn), n_first, 1, col_of=lambda j, i: j)
    return pl.pallas_call(
        functools.partial(_residual_matmul_kernel, n_first=n_first),
        grid=(n // bn, m // bm),
        in_specs=[pl.BlockSpec((bm, k), lambda j, i: (i, 0)),
                  pl.BlockSpec((k, bn), lambda j, i: (0, j)),
                  r_first, r_second],
        out_specs=pl.BlockSpec((bm, bn), lambda j, i: (i, j)),
        out_shape=jax.ShapeDtypeStruct((m, n), F32),
        compiler_params=_params("parallel", "parallel"),
        name="residual_matmul",
    )(x, w, r_parts[0], r_parts[1])


CONV_PAD = 8


def _conv_kernel(u_ref, buf_ref, w_ref, b_ref, o_ref, cache_ref, prev_ref, *, tb):
    t = pl.program_id(2)
    lo = CONV_PAD - (SSD_CONV - 1)

    @pl.when(t == 0)
    def _():
        prev_ref[lo:CONV_PAD, :] = buf_ref[0]

    u = u_ref[...]
    first_row = lax.broadcasted_iota(jnp.int32, (CONV_PAD, u.shape[1]), 0) == 0
    acc = b_ref[...] + u * w_ref[SSD_CONV - 1:SSD_CONV, :]
    shifted = u
    for k in range(1, SSD_CONV):
        shifted = pltpu.roll(shifted, 1, axis=0)
        head = jnp.where(first_row, prev_ref[CONV_PAD - k:CONV_PAD - k + 1, :], shifted[0:CONV_PAD])
        shifted = jnp.concatenate([head, shifted[CONV_PAD:]], axis=0)
        acc = acc + shifted * w_ref[SSD_CONV - 1 - k:SSD_CONV - k, :]
    o_ref[...] = _silu(acc)
    prev_ref[...] = u[tb - CONV_PAD:tb]

    @pl.when(t == pl.num_programs(2) - 1)
    def _():
        cache_ref[0] = prev_ref[lo:CONV_PAD, :]


def _conv_silu(proj, row_off, bt, seqlen, buf, conv_w, conv_b):
    tb = _largest_divisor(seqlen, (512, 256, 128, 64, 32))
    cb = 2048
    nt = seqlen // tb
    roff = row_off // tb
    coff = COL_XBC // cb
    return pl.pallas_call(
        functools.partial(_conv_kernel, tb=tb),
        grid=(SSD_CONV_DIM // cb, bt, nt),
        in_specs=[pl.BlockSpec((tb, cb), lambda c, b, t: (roff + b * nt + t, coff + c)),
                  pl.BlockSpec((1, SSD_CONV - 1, cb), lambda c, b, t: (b, 0, c)),
                  pl.BlockSpec((SSD_CONV, cb), lambda c, b, t: (0, c)),
                  pl.BlockSpec((1, cb), lambda c, b, t: (0, c))],
        out_specs=[pl.BlockSpec((tb, cb), lambda c, b, t: (b * nt + t, c)),
                   pl.BlockSpec((1, SSD_CONV - 1, cb), lambda c, b, t: (b, 0, c))],
        out_shape=[jax.ShapeDtypeStruct((bt * seqlen, SSD_CONV_DIM), F32),
                   jax.ShapeDtypeStruct((bt, SSD_CONV - 1, SSD_CONV_DIM), F32)],
        scratch_shapes=[pltpu.VMEM((CONV_PAD, cb), F32)],
        compiler_params=_params("parallel", "parallel", "arbitrary"),
        name="conv_silu",
    )(proj, buf, conv_w, conv_b.reshape(1, SSD_CONV_DIM))


def _ssd_kernel(x_ref, b_ref, c_ref, z_ref, dtx_ref, dtt_ref, alx_ref, alt_ref, dsk_ref, nw_ref,
                s0_ref, y_ref, sout_ref, st_ref, *, q, nchunk, ng):
    t = pl.program_id(2)
    p, r, gw, n = SSD_HEADDIM, SSD_HEADS_PER_GROUP, SSD_GROUP_WIDTH, SSD_STATE

    @pl.when(t == 0)
    def _():
        st_ref[...] = s0_ref[0]

    row = lax.broadcasted_iota(jnp.int32, (q, q), 0)
    col = lax.broadcasted_iota(jnp.int32, (q, q), 1)
    lower = row >= col
    tril = _tril3(q)
    triu = (row <= col).astype(BF16)
    a_x = -jnp.exp(alx_ref[...])
    a_t = -jnp.exp(alt_ref[...]).reshape(ng * r, 1)
    d_skip = dsk_ref[...]
    norm_w = nw_ref[...]
    nt_dims = (((1,), (1,)), ((), ()))
    tn_dims = (((0,), (0,)), ((), ()))

    def chunk(ci, carry):
        rows = pl.ds(pl.multiple_of(ci * q, q), q)
        x = x_ref[rows, :]
        dtx = dtx_ref[rows, :]
        cum = _cumsum_rows(tril, dtx * a_x)
        cum_t = _split3_dot(dtt_ref[:, ci].reshape(ng * r, q) * a_t, triu)
        cum_last = cum[q - 1:q, :]
        decay_in = jnp.exp(cum)
        xdt = (x * dtx).astype(BF16)
        xt = (x * (jnp.exp(cum_last - cum) * dtx)).astype(BF16)
        decay_st = jnp.exp(cum_last)
        gate = _silu(z_ref[rows, :])
        skip = d_skip * x
        for gi in range(ng):
            cols = slice(gi * gw, (gi + 1) * gw)
            bm = b_ref[rows, gi * n:(gi + 1) * n].astype(BF16)
            cm = c_ref[rows, gi * n:(gi + 1) * n].astype(BF16)
            cb = lax.dot_general(cm, bm, nt_dims, preferred_element_type=F32)
            st = st_ref[gi]
            y = jnp.dot(cm, st.astype(BF16), preferred_element_type=F32) * decay_in[:, cols]
            parts = []
            for h in range(r):
                c0 = gi * gw + h * p
                seg = cum[:, c0:c0 + q] - cum_t[gi * r + h:gi * r + h + 1, :]
                mix = (cb * jnp.where(lower, jnp.exp(seg), 0.0)).astype(BF16)
                parts.append(jnp.dot(mix, xdt[:, c0:c0 + p], preferred_element_type=F32))
            y = (y + jnp.concatenate(parts, axis=1) + skip[:, cols]) * gate[:, cols]
            ms = jnp.mean(y * y, axis=-1, keepdims=True)
            y_ref[rows, cols] = (y * lax.rsqrt(ms + NORM_EPS) * norm_w[:, cols]).astype(y_ref.dtype)
            st_ref[gi] = st * decay_st[:, cols] + lax.dot_general(
                bm, xt[:, cols], tn_dims, preferred_element_type=F32)
        return carry

    lax.fori_loop(0, nchunk, chunk, 0, unroll=SCAN_UNROLL if nchunk % SCAN_UNROLL == 0 else 1)

    @pl.when(t == pl.num_programs(2) - 1)
    def _():
        sout_ref[0] = st_ref[...]


def _ssd_scan(xconv, proj, dtx, dt, row_off, bt, seqlen, a_log, d_skip, ssd_norm_w, state0):
    g, r, p, n, gw = SSD_GROUPS, SSD_HEADS_PER_GROUP, SSD_HEADDIM, SSD_STATE, SSD_GROUP_WIDTH
    q = min(CHUNK, seqlen)
    ng = SSD_GROUPS_PER_STEP if q == CHUNK else 2 * SSD_GROUPS_PER_STEP
    tb = _largest_divisor(seqlen, (512, 256, 128, 64, 32))
    nchunk = tb // q
    nt = seqlen // tb
    rows = bt * seqlen
    roff = row_off // tb
    dtt = dt[row_off:row_off + rows, :SSD_HEADS].reshape(rows // q, q, g, r).transpose(2, 0, 3, 1)
    alx = jnp.repeat(a_log, p).reshape(1, SSD_INNER)
    alt = a_log.reshape(g, r, 1)
    dsk = jnp.repeat(d_skip, p).reshape(1, SSD_INNER)
    s0 = state0.reshape(bt, g, r, p, n).transpose(0, 1, 4, 2, 3).reshape(bt, g, n, gw)
    bcol = SSD_INNER // (ng * n)
    ccol = bcol + g // ng
    wide = ng * gw
    y, sout = pl.pallas_call(
        functools.partial(_ssd_kernel, q=q, nchunk=nchunk, ng=ng),
        grid=(bt, g // ng, nt),
        in_specs=[pl.BlockSpec((tb, wide), lambda b, gi, t: (b * nt + t, gi)),
                  pl.BlockSpec((tb, ng * n), lambda b, gi, t: (b * nt + t, bcol + gi)),
                  pl.BlockSpec((tb, ng * n), lambda b, gi, t: (b * nt + t, ccol + gi)),
                  pl.BlockSpec((tb, wide), lambda b, gi, t: (roff + b * nt + t, COL_Z // wide + gi)),
                  pl.BlockSpec((tb, wide), lambda b, gi, t: (roff + b * nt + t, gi)),
                  pl.BlockSpec((ng, nchunk, r, q), lambda b, gi, t: (gi, b * nt + t, 0, 0)),
                  pl.BlockSpec((1, wide), lambda b, gi, t: (0, gi)),
                  pl.BlockSpec((ng, r, 1), lambda b, gi, t: (gi, 0, 0)),
                  pl.BlockSpec((1, wide), lambda b, gi, t: (0, gi)),
                  pl.BlockSpec((1, wide), lambda b, gi, t: (0, gi)),
                  pl.BlockSpec((1, ng, n, gw), lambda b, gi, t: (b, gi, 0, 0))],
        out_specs=[pl.BlockSpec((tb, wide), lambda b, gi, t: (b * nt + t, gi)),
                   pl.BlockSpec((1, ng, n, gw), lambda b, gi, t: (b, gi, 0, 0))],
        out_shape=[jax.ShapeDtypeStruct((rows, SSD_INNER), BF16),
                   jax.ShapeDtypeStruct((bt, g, n, gw), F32)],
        scratch_shapes=[pltpu.VMEM((ng, n, gw), F32)],
        compiler_params=_params("parallel", "parallel", "arbitrary"),
        name="ssd_scan",
    )(xconv, xconv, xconv, proj, dtx, dtt, alx, alt, dsk, ssd_norm_w.reshape(1, SSD_INNER), s0)
    state = sout.reshape(bt, g, n, r, p).transpose(0, 1, 3, 4, 2).reshape(bt, SSD_HEADS, p, n)
    return y, state


def _tril3(q):
    row = lax.broadcasted_iota(jnp.int32, (q, 3 * q), 0)
    col = lax.broadcasted_iota(jnp.int32, (q, 3 * q), 1)
    col = jnp.where(col >= 2 * q, col - 2 * q, jnp.where(col >= q, col - q, col))
    return (row >= col).astype(BF16)


def _cumsum_rows(tril3_bf16, x):
    hi = x.astype(BF16)
    rest = x - hi.astype(F32)
    mid = rest.astype(BF16)
    lo = (rest - mid.astype(F32)).astype(BF16)
    return jnp.dot(tril3_bf16, jnp.concatenate([hi, mid, lo], axis=0), preferred_element_type=F32)


def _hgrn_kernel(q_ref, f_ref, i_ref, g_ref, lbt_ref, nw_ref, s0_ref, o_ref, sout_ref, st_ref,
                 *, q, nchunk, nh):
    t = pl.program_id(2)

    @pl.when(t == 0)
    def _():
        st_ref[...] = s0_ref[0]

    hq = q // 2
    row = lax.broadcasted_iota(jnp.int32, (q, q), 0)
    col = lax.broadcasted_iota(jnp.int32, (q, q), 1)
    lower = row >= col
    tril = _tril3(q)
    left_half = lax.broadcasted_iota(jnp.int32, (hq, LANES), 1) < hq
    k_pad = jnp.zeros((LANES - 3 * hq, nh * HG_K), BF16)
    tab = lbt_ref[...]
    e = jnp.exp(tab - jnp.max(tab, axis=0, keepdims=True))
    lb = e[0:1, :] / jnp.sum(e, axis=0, keepdims=True)
    norm_w = nw_ref[...]
    nt_dims = (((1,), (1,)), ((), ()))
    tn_dims = (((0,), (0,)), ((), ()))

    def decayed(x, cum_rows, ref_row, sign):
        return (x * jnp.exp(sign * (cum_rows - ref_row))).astype(BF16)

    def chunk(ci, carry):
        rows = pl.ds(pl.multiple_of(ci * q, q), q)
        qq = _silu(q_ref[rows, :]) * (HG_K ** -0.5)
        f = lb + (1.0 - lb) * _sigmoid(f_ref[rows, :])
        kk = 1.0 - f
        v = i_ref[rows, :].astype(BF16)
        gate = _silu(g_ref[rows, :])
        cum = _cumsum_rows(tril, jnp.log(f))
        top, bot = slice(0, hq), slice(hq, q)
        mid_top = cum[hq // 2 - 1:hq // 2, :]
        mid_bot = cum[hq + hq // 2 - 1:hq + hq // 2, :]
        edge = cum[hq - 1:hq, :]
        cum_last = cum[q - 1:q, :]
        q_tt = decayed(qq[top], cum[top], mid_top, 1.0)
        k_tt = decayed(kk[top], cum[top], mid_top, -1.0)
        q_bb = decayed(qq[bot], cum[bot], mid_bot, 1.0)
        k_bb = decayed(kk[bot], cum[bot], mid_bot, -1.0)
        q_bt = decayed(qq[bot], cum[bot], edge, 1.0)
        k_bt = decayed(kk[top], cum[top], edge, -1.0)
        qe = (qq * jnp.exp(cum)).astype(BF16)
        kt = (kk * jnp.exp(cum_last - cum)).astype(BF16)
        dec = jnp.exp(cum_last)
        q_stack = jnp.concatenate([q_tt, q_bb, q_bt], axis=0)
        k_stack = jnp.concatenate([k_tt, k_bb, k_bt, k_pad], axis=0)
        for h in range(nh):
            sl = slice(h * HG_K, (h + 1) * HG_K)
            s_all = lax.dot_general(q_stack[:, sl], k_stack[:, sl], nt_dims,
                                    preferred_element_type=F32)
            bottom_left = pltpu.roll(s_all[2 * hq:3 * hq, :], LANES - 2 * hq, axis=1)
            s_bot = jnp.where(left_half, bottom_left, s_all[hq:2 * hq, :])
            att = jnp.concatenate([s_all[0:hq, 0:q], s_bot[:, 0:q]], axis=0)
            att = jnp.where(lower, att, 0.0).astype(BF16)
            st = st_ref[h]
            o = jnp.dot(att, v[:, sl], preferred_element_type=F32) + lax.dot_general(
                qe[:, sl], st.astype(BF16), nt_dims, preferred_element_type=F32)
            ms = jnp.mean(o * o, axis=-1, keepdims=True)
            o_ref[rows, sl] = (o * lax.rsqrt(ms + NORM_EPS) * norm_w * gate[:, sl]).astype(o_ref.dtype)
            st_ref[h] = st * dec[:, sl] + lax.dot_general(v[:, sl], kt[:, sl], tn_dims,
                                                          preferred_element_type=F32)
        return carry

    lax.fori_loop(0, nchunk, chunk, 0, unroll=SCAN_UNROLL if nchunk % SCAN_UNROLL == 0 else 1)

    @pl.when(t == pl.num_programs(2) - 1)
    def _():
        sout_ref[0] = st_ref[...]


def _hgrn_scan(proj, row_off, bt, seqlen, lb_table, hg_norm_w, state0):
    q = min(CHUNK, seqlen)
    tb = _largest_divisor(seqlen, (512, 256, 128, 64, 32))
    nchunk = tb // q
    nt = seqlen // tb
    rows = bt * seqlen
    roff = row_off // tb
    nh = HGRN_HEADS_PER_STEP if q == CHUNK else 2 * HGRN_HEADS_PER_STEP
    hw = nh * HG_K
    s0 = state0.transpose(0, 1, 3, 2)
    nrow = lb_table.shape[0]

    def col_spec(col0):
        return pl.BlockSpec((tb, hw), lambda b, h, t: (roff + b * nt + t, col0 // hw + h))

    o, sout = pl.pallas_call(
        functools.partial(_hgrn_kernel, q=q, nchunk=nchunk, nh=nh),
        grid=(bt, HG_HEADS // nh, nt),
        in_specs=[col_spec(COL_HQ), col_spec(COL_HF), col_spec(COL_HI), col_spec(COL_HG),
                  pl.BlockSpec((nrow, hw), lambda b, h, t: (0, h)),
                  pl.BlockSpec((1, HG_V), lambda b, h, t: (0, 0)),
                  pl.BlockSpec((1, nh, HG_V, HG_K), lambda b, h, t: (b, h, 0, 0))],
        out_specs=[pl.BlockSpec((tb, hw), lambda b, h, t: (b * nt + t, h)),
                   pl.BlockSpec((1, nh, HG_V, HG_K), lambda b, h, t: (b, h, 0, 0))],
        out_shape=[jax.ShapeDtypeStruct((rows, HG_HEADS * HG_V), BF16),
                   jax.ShapeDtypeStruct((bt, HG_HEADS, HG_V, HG_K), F32)],
        scratch_shapes=[pltpu.VMEM((nh, HG_V, HG_K), F32)],
        compiler_params=_params("parallel", "parallel", "arbitrary"),
        name="hgrn_scan",
    )(proj, proj, proj, proj, lb_table, hg_norm_w.reshape(1, HG_V), s0)
    return o, sout.transpose(0, 1, 3, 2)


def _top_values(s, count):
    vals = []
    work = s
    for _ in range(count):
        m = jnp.max(work, axis=0, keepdims=True)
        vals.append(m)
        work = jnp.where(work == m, -jnp.inf, work)
    return jnp.concatenate(vals, axis=0)


def _oddeven_merge(lo, hi, r):
    step = r * 2
    if step < hi - lo:
        yield from _oddeven_merge(lo, hi, step)
        yield from _oddeven_merge(lo + r, hi, step)
        yield from [(i, i + r) for i in range(lo + r, hi - r, step)]
    else:
        yield (lo, lo + r)


def _oddeven_merge_sort(lo, hi):
    if hi - lo >= 1:
        mid = lo + (hi - lo) // 2
        yield from _oddeven_merge_sort(lo, mid)
        yield from _oddeven_merge_sort(mid + 1, hi)
        yield from _oddeven_merge(lo, hi, 1)


SUBLANES = 8
_SORT_NETWORK = tuple(_oddeven_merge_sort(0, PEER_KEYS // SUBLANES - 1))


def _compare_exchange(v, i, j):
    v[i], v[j] = jnp.maximum(v[i], v[j]), jnp.minimum(v[i], v[j])


def _top_sorted(s):
    n = PEER_KEYS // SUBLANES
    assert n == PEER_TOPK
    x = s.reshape(n, SUBLANES, s.shape[-1])
    v = [x[k] for k in range(n)]
    for i, j in _SORT_NETWORK:
        _compare_exchange(v, i, j)
    shift = 1
    while shift < SUBLANES:
        w = [pltpu.roll(a, shift, axis=0) for a in v]
        v = [jnp.maximum(v[k], w[n - 1 - k]) for k in range(n)]
        d = n // 2
        while d >= 1:
            for k in range(n):
                if not k & d:
                    _compare_exchange(v, k, k + d)
            d //= 2
        shift *= 2
    return jnp.concatenate([a[0:1] for a in v], axis=0)


def _peer_query_kernel(w_ref, x_ref, sk_ref, thr_ref, e0_ref, s1_ref, e1_ref):
    half = PEER_QDIM // 2
    nh = thr_ref.shape[0]
    qt = jnp.dot(w_ref[...], x_ref[...], preferred_element_type=F32)

    def split(a):
        hi = a.astype(BF16)
        return hi, (a - hi.astype(F32)).astype(BF16)

    def scores(keys, q):
        (k_hi, k_lo), (q_hi, q_lo) = keys, split(q)
        return (jnp.dot(k_hi, q_hi, preferred_element_type=F32)
                + jnp.dot(k_hi, q_lo, preferred_element_type=F32)
                + jnp.dot(k_lo, q_hi, preferred_element_type=F32))

    keys0, keys1 = split(sk_ref[0]), split(sk_ref[1])
    for h in range(nh):
        s0 = scores(keys0, qt[h * PEER_QDIM:h * PEER_QDIM + half])
        s1 = scores(keys1, qt[h * PEER_QDIM + half:(h + 1) * PEER_QDIM])
        sv0 = _top_sorted(s0)
        sv1 = _top_sorted(s1)
        cand = [sv0[0:1] + sv1]
        for a in range(1, 8):
            cand.append(sv0[a:a + 1] + sv1[0:8])
        cand.append(sv0[8:16] + sv1[0:1])
        top = _top_values(jnp.concatenate(cand, axis=0), PEER_TOPK)
        z = jnp.sum(jnp.exp(top - top[0:1]), axis=0, keepdims=True)
        tau = top[PEER_TOPK - 1:PEER_TOPK]
        thr = jnp.full(s0.shape, jnp.inf, F32)
        for b in range(PEER_TOPK):
            thr = jnp.where(s0 + sv1[b:b + 1] >= tau, sv1[b:b + 1], thr)
        thr_ref[h] = thr.reshape(thr_ref.shape[1:])
        s1_ref[h] = s1
        e0_ref[h] = (jnp.exp(s0 - sv0[0:1]) / z).reshape(e0_ref.shape[1:])
        e1_ref[h] = jnp.exp(s1 - sv1[0:1])


def _peer_query(w_pq_t, xn_t, sub_keys):
    d, t = xn_t.shape
    tb = _largest_divisor(t, (256, 128))
    nh = PEER_QUERY_HEADS_PER_STEP
    keyed = jax.ShapeDtypeStruct((PEER_HEADS, PEER_KEYS, t), F32)
    key_spec = pl.BlockSpec((nh, PEER_KEYS, tb), lambda i, h: (h, 0, i))
    nb = PEER_EXPERT_KEY_ROWS
    grouped = jax.ShapeDtypeStruct((PEER_HEADS, PEER_KEYS // nb, nb, t), F32)
    group_spec = pl.BlockSpec((nh, PEER_KEYS // nb, nb, tb), lambda i, h: (h, 0, 0, i))
    return pl.pallas_call(
        _peer_query_kernel,
        grid=(t // tb, PEER_HEADS // nh),
        in_specs=[pl.BlockSpec((nh * PEER_QDIM, d), lambda i, h: (h, 0)),
                  pl.BlockSpec((d, tb), lambda i, h: (0, i)),
                  pl.BlockSpec((2, PEER_KEYS, PEER_QDIM // 2), lambda i, h: (0, 0, 0))],
        out_specs=[group_spec, group_spec, key_spec, key_spec],
        out_shape=[grouped, grouped, keyed, keyed],
        compiler_params=_params("parallel", "arbitrary"),
        name="peer_query",
    )(w_pq_t, xn_t, sub_keys)


def _gelu_tanh(x):
    return 0.5 * x * (1.0 + jnp.tanh(0.7978845608028654 * (x + 0.044715 * (x * x * x))))


def _peer_expert_kernel(x_ref, u_ref, v_ref, thr_ref, e0_ref, s1_ref, e1_ref, o_ref, w_ref, *, nb, tb):
    e = pl.program_id(1)

    @pl.when(e == 0)
    def _():
        o_ref[...] = jnp.zeros_like(o_ref)

    act = _gelu_tanh(jnp.dot(u_ref[...], x_ref[...], preferred_element_type=F32))
    for i in range(nb):
        for tl in range(tb // LANES):
            lanes = pl.ds(tl * LANES, LANES)
            acc = jnp.zeros((PEER_KEYS, LANES), F32)
            for h in range(PEER_HEADS):
                thr = thr_ref[h, 0, i:i + 1, lanes]
                c0 = e0_ref[h, 0, i:i + 1, lanes]
                acc = acc + jnp.where(s1_ref[h, :, lanes] >= thr, e1_ref[h, :, lanes] * c0, 0.0)
            w_ref[i * PEER_KEYS:(i + 1) * PEER_KEYS, lanes] = acc
    g = (w_ref[...] * act).astype(BF16)
    o_ref[...] += lax.dot_general(g, v_ref[...], (((0,), (0,)), ((), ())),
                                  preferred_element_type=F32)


def _peer_experts(xn_t, u, v, thr, e0, s1, e1):
    d, t = xn_t.shape
    n_exp = u.shape[0]
    tb = _largest_divisor(t, (512, 256, 128))
    nb = PEER_EXPERT_KEY_ROWS
    eb = nb * PEER_KEYS
    once = pl.Buffered(1)
    keyed = pl.BlockSpec((PEER_HEADS, PEER_KEYS, tb), lambda i, e: (0, 0, i), pipeline_mode=once)
    rows = pl.BlockSpec((PEER_HEADS, 1, nb, tb), lambda i, e: (0, e, 0, i))
    return pl.pallas_call(
        functools.partial(_peer_expert_kernel, nb=nb, tb=tb),
        grid=(t // tb, n_exp // eb),
        in_specs=[pl.BlockSpec((d, tb), lambda i, e: (0, i), pipeline_mode=once),
                  pl.BlockSpec((eb, d), lambda i, e: (e, 0)),
                  pl.BlockSpec((eb, d), lambda i, e: (e, 0)),
                  rows, rows, keyed, keyed],
        out_specs=pl.BlockSpec((tb, d), lambda i, e: (i, 0)),
        out_shape=jax.ShapeDtypeStruct((t, d), F32),
        scratch_shapes=[pltpu.VMEM((eb, tb), F32)],
        compiler_params=_params("parallel", "arbitrary"),
        name="peer_experts",
    )(xn_t, u, v, thr, e0, s1, e1)


def _branches(proj, dtx, dt, row_off, bt, seqlen, conv_buf, s_ssd, s_hg, conv_w, conv_b, a_log, d_skip,
              ssd_norm_w, lb_table, hg_norm_w):
    xconv, new_conv = _conv_silu(proj, row_off, bt, seqlen, conv_buf, conv_w, conv_b)
    y_ssd, s_ssd_new = _ssd_scan(xconv, proj, dtx, dt, row_off, bt, seqlen,
                                 a_log, d_skip, ssd_norm_w, s_ssd)
    o_hg, s_hg_new = _hgrn_scan(proj, row_off, bt, seqlen, lb_table, hg_norm_w, s_hg)
    return y_ssd, o_hg, s_ssd_new, new_conv, s_hg_new


def kernel(x_prompt, x_sample, state_ssd, cache_ssd_conv, state_hgrn, lb_table, norm1_w, w_in, conv_w, conv_b, dt_bias, a_log, d_skip, ssd_norm_w, w_ssd_out, hg_norm_w, w_hg_out, w_o, norm2_w, w_pq, sub_keys, u_experts, v_experts, final_norm_w):
    assert w_in.shape[0] == 1 and lb_table.shape[0] == 2, "single-layer stack"
    bp, lp, d = x_prompt.shape
    bs, ls, _ = x_sample.shape
    tp, ts = bp * lp, bs * ls
    x_parts = (x_prompt.reshape(tp, d), x_sample.reshape(ts, d))

    w_in_t = w_in[0].T
    w_dt = jnp.pad(w_in_t[DT_COL:DT_COL + SSD_HEADS], ((0, LANES - SSD_HEADS), (0, 0))).astype(BF16)
    b_dt = jnp.pad(dt_bias[0], (0, LANES - SSD_HEADS)).reshape(1, LANES)

    xn = _rmsnorm_stacked(*x_parts, norm1_w[0], BF16)
    proj, u_bf16, v_bf16 = _in_proj(xn, w_in_t, u_experts[0], v_experts[0])
    dt, dtx = _dt_proj(xn, w_dt, b_dt)

    zeros = functools.partial(jnp.zeros, dtype=F32)
    common = (conv_w[0], conv_b[0], a_log[0], d_skip[0], ssd_norm_w[0], lb_table, hg_norm_w[0])
    ya_p, yb_p, ssd_p, conv_p, hg_p = _branches(
        proj, dtx, dt, 0, bp, lp,
        zeros((bp, SSD_CONV - 1, SSD_CONV_DIM)), zeros((bp, SSD_HEADS, SSD_HEADDIM, SSD_STATE)),
        zeros((bp, HG_HEADS, HG_K, HG_V)), *common)
    ya_s, yb_s, ssd_s, conv_s, hg_s = _branches(
        proj, dtx, dt, tp, bs, ls, cache_ssd_conv[0], state_ssd[0], state_hgrn[0], *common)

    merged = _gated_merge((ya_p, ya_s), (yb_p, yb_s), w_ssd_out[0].astype(BF16),
                          w_hg_out[0].astype(BF16), proj)
    h = _residual_matmul(merged, w_o[0].astype(BF16), x_parts)

    hn_t = _rmsnorm_t(h, norm2_w[0], BF16)
    thr, e0, s1, e1 = _peer_query(w_pq[0].T.astype(BF16), hn_t, sub_keys[0])
    peer = _peer_experts(hn_t, u_bf16, v_bf16, thr, e0, s1, e1)
    y_prompt = _add_rmsnorm(h, peer, final_norm_w, 0, tp).reshape(bp, lp, d)
    y_sample = _add_rmsnorm(h, peer, final_norm_w, tp, ts).reshape(bs, ls, d)
    return (y_prompt, y_sample, ssd_p[None], conv_p[None], hg_p[None],
            ssd_s[None], conv_s[None], hg_s[None])
```

```python
import functools

import jax
import jax.numpy as jnp
from jax import lax
from jax.experimental import pallas as pl
from jax.experimental.pallas import tpu as pltpu

F32 = jnp.float32
BF16 = jnp.bfloat16

NORM_EPS = 1e-6
LANES = 128
VMEM_LIMIT_BYTES = 56 * 1024 * 1024
IN_PROJ_VMEM_LIMIT_BYTES = 62 * 1024 * 1024

D_MODEL = 4096
SSD_GROUPS = 8
SSD_HEADS_PER_GROUP = 8
SSD_HEADDIM = 64
SSD_STATE = 128
SSD_HEADS = SSD_GROUPS * SSD_HEADS_PER_GROUP
SSD_INNER = SSD_HEADS * SSD_HEADDIM
SSD_GROUP_WIDTH = SSD_HEADS_PER_GROUP * SSD_HEADDIM
SSD_CONV = 4
SSD_CONV_DIM = SSD_INNER + 2 * SSD_GROUPS * SSD_STATE
HG_HEADS = 32
HG_K = 128
HG_V = 128
CHUNK = 64
PEER_HEADS = 8
PEER_KEYS = 128
PEER_TOPK = 16
PEER_QDIM = 256
PEER_EXPERT_KEY_ROWS = 4
PEER_QUERY_HEADS_PER_STEP = 4
SCAN_UNROLL = 2
SSD_GROUPS_PER_STEP = 4
HGRN_HEADS_PER_STEP = 8

COL_Z = 0
COL_XBC = SSD_INNER
COL_HQ = COL_XBC + SSD_CONV_DIM
COL_HF = COL_HQ + HG_HEADS * HG_K
COL_HI = COL_HF + HG_HEADS * HG_K
COL_HG = COL_HI + HG_HEADS * HG_V
COL_GA = COL_HG + HG_HEADS * HG_V
COL_GB = COL_GA + D_MODEL
PROJ_WIDTH = COL_GB + D_MODEL
DT_COL = SSD_INNER + SSD_CONV_DIM


def _largest_divisor(n, candidates):
    for c in candidates:
        if n % c == 0:
            return c
    raise ValueError(f"no block size among {candidates} divides {n}")


def _params(*semantics):
    return pltpu.CompilerParams(dimension_semantics=semantics,
                                vmem_limit_bytes=VMEM_LIMIT_BYTES)


def _sigmoid(x):
    return 1.0 / (1.0 + jnp.exp(-x))


def _silu(x):
    return x * _sigmoid(x)


def _stacked_specs(block, n_first, row_pos, col_of=None):
    def first(*idx):
        return (jnp.minimum(idx[row_pos], n_first - 1), 0 if col_of is None else col_of(*idx))

    def second(*idx):
        return (jnp.maximum(idx[row_pos] - n_first, 0), 0 if col_of is None else col_of(*idx))

    return pl.BlockSpec(block, first), pl.BlockSpec(block, second)


def _stacked_block(i, n_first, first_ref, second_ref):
    return jnp.where(i < n_first, first_ref[...], second_ref[...])


def _row_block(n_first_rows, n_second_rows, candidates):
    for c in candidates:
        if n_first_rows % c == 0 and n_second_rows % c == 0:
            return c
    raise ValueError(f"no block size among {candidates} divides {n_first_rows} and {n_second_rows}")


def _rmsnorm_stacked_kernel(xa_ref, xb_ref, w_ref, o_ref, *, n_first):
    x = _stacked_block(pl.program_id(0), n_first, xa_ref, xb_ref)
    ms = jnp.mean(x * x, axis=-1, keepdims=True)
    o_ref[...] = (x * lax.rsqrt(ms + NORM_EPS) * w_ref[...]).astype(o_ref.dtype)


def _rmsnorm_stacked(xa, xb, w, out_dtype):
    (ta, d), tb_rows = xa.shape, xb.shape[0]
    bm = _row_block(ta, tb_rows, (256, 128, 64, 32, 16, 8))
    n_first = ta // bm
    spec_a, spec_b = _stacked_specs((bm, d), n_first, 0)
    return pl.pallas_call(
        functools.partial(_rmsnorm_stacked_kernel, n_first=n_first),
        grid=((ta + tb_rows) // bm,),
        in_specs=[spec_a, spec_b, pl.BlockSpec((1, d), lambda i: (0, 0))],
        out_specs=pl.BlockSpec((bm, d), lambda i: (i, 0)),
        out_shape=jax.ShapeDtypeStruct((ta + tb_rows, d), out_dtype),
        compiler_params=_params("parallel"),
        name="rmsnorm_in",
    )(xa, xb, w.reshape(1, d))


def _rmsnorm_t_kernel(x_ref, w_ref, o_ref):
    x = x_ref[...]
    ms = jnp.mean(x * x, axis=-1, keepdims=True)
    o_ref[...] = jnp.transpose(x * lax.rsqrt(ms + NORM_EPS) * w_ref[...]).astype(o_ref.dtype)


def _rmsnorm_t(x, w, out_dtype):
    t, d = x.shape
    bm = _largest_divisor(t, (256, 128))
    return pl.pallas_call(
        _rmsnorm_t_kernel,
        grid=(t // bm,),
        in_specs=[pl.BlockSpec((bm, d), lambda i: (i, 0)),
                  pl.BlockSpec((1, d), lambda i: (0, 0))],
        out_specs=pl.BlockSpec((d, bm), lambda i: (0, i)),
        out_shape=jax.ShapeDtypeStruct((d, t), out_dtype),
        compiler_params=_params("parallel"),
        name="rmsnorm_t",
    )(x, w.reshape(1, d))


def _add_rmsnorm_kernel(a_ref, b_ref, w_ref, o_ref):
    x = a_ref[...] + b_ref[...]
    ms = jnp.mean(x * x, axis=-1, keepdims=True)
    o_ref[...] = x * lax.rsqrt(ms + NORM_EPS) * w_ref[...]


def _add_rmsnorm(a, b, w, row_off, rows):
    d = a.shape[1]
    bm = _row_block(rows, row_off, (256, 128, 64, 32, 16, 8))
    off = row_off // bm
    return pl.pallas_call(
        _add_rmsnorm_kernel,
        grid=(rows // bm,),
        in_specs=[pl.BlockSpec((bm, d), lambda i: (off + i, 0)),
                  pl.BlockSpec((bm, d), lambda i: (off + i, 0)),
                  pl.BlockSpec((1, d), lambda i: (0, 0))],
        out_specs=pl.BlockSpec((bm, d), lambda i: (i, 0)),
        out_shape=jax.ShapeDtypeStruct((rows, d), F32),
        compiler_params=_params("parallel"),
        name="add_rmsnorm",
    )(a, b, w.reshape(1, d))


def _in_proj_kernel(x_ref, wa_ref, wb_ref, ta_ref, tb_ref, o_ref, ta_out_ref, tb_out_ref, wbuf_ref,
                    *, n_plain, shift, chunk):
    ta_out_ref[...] = ta_ref[...].astype(ta_out_ref.dtype)
    tb_out_ref[...] = tb_ref[...].astype(tb_out_ref.dtype)
    j = pl.program_id(0)
    i = pl.program_id(1)
    bn = wa_ref.shape[0]

    @pl.when(jnp.logical_and(i == 0, j < n_plain))
    def _():
        def body(c, carry):
            rows = pl.ds(pl.multiple_of(c * chunk, chunk), chunk)
            wbuf_ref[rows, :] = wa_ref[rows, :].astype(BF16)
            return carry
        lax.fori_loop(0, bn // chunk, body, 0)

    @pl.when(jnp.logical_and(i == 0, j >= n_plain))
    def _():
        def body(c, carry):
            dst = pl.ds(pl.multiple_of(c * chunk, chunk), chunk)
            src = pl.ds(pl.multiple_of(c * chunk + shift, shift), chunk)
            wbuf_ref[dst, :] = wa_ref[src, :].astype(BF16)
            return carry
        lax.fori_loop(0, (bn - shift) // chunk, body, 0)
        wbuf_ref[bn - shift:bn, :] = wb_ref[...].astype(BF16)

    o_ref[...] = lax.dot_general(x_ref[...], wbuf_ref[...], (((1,), (1,)), ((), ())),
                                 preferred_element_type=F32)


def _slab_spec(rows, width, steps, step_of):
    slab = next(r for r in range(16, rows + 1, 16) if rows % r == 0 and rows // r <= steps)
    last = rows // slab - 1
    return pl.BlockSpec((slab, width), lambda *idx: (jnp.minimum(step_of(*idx), last), 0))


def _in_proj(x, w_t, table_a, table_b):
    m, k = x.shape
    bm = _largest_divisor(m, (512, 256, 128, 64, 32, 16, 8))
    bn = 1024
    shift = SSD_HEADS
    assert DT_COL % bn == 0 and PROJ_WIDTH % bn == 0 and bn % shift == 0
    n_i = m // bm
    assert table_b.shape == table_a.shape
    slab_spec = _slab_spec(*table_a.shape, (PROJ_WIDTH // bn) * n_i, lambda j, i: j * n_i + i)
    table_bf16 = jax.ShapeDtypeStruct(table_a.shape, BF16)
    return pl.pallas_call(
        functools.partial(_in_proj_kernel, n_plain=DT_COL // bn, shift=shift, chunk=shift),
        grid=(PROJ_WIDTH // bn, n_i),
        in_specs=[pl.BlockSpec((bm, k), lambda j, i: (i, 0)),
                  pl.BlockSpec((bn, k), lambda j, i: (j, 0)),
                  pl.BlockSpec((shift, k), lambda j, i: ((bn // shift) * (j + 1), 0)),
                  slab_spec, slab_spec],
        out_specs=[pl.BlockSpec((bm, bn), lambda j, i: (i, j)), slab_spec, slab_spec],
        out_shape=[jax.ShapeDtypeStruct((m, PROJ_WIDTH), F32), table_bf16, table_bf16],
        scratch_shapes=[pltpu.VMEM((bn, k), BF16)],
        compiler_params=pltpu.CompilerParams(dimension_semantics=("arbitrary", "arbitrary"),
                                             vmem_limit_bytes=IN_PROJ_VMEM_LIMIT_BYTES),
        name="in_proj",
    )(x, w_t, w_t, table_a, table_b)


def _split3_dot(x, rhs_bf16):
    hi = x.astype(BF16)
    rest = x - hi.astype(F32)
    mid = rest.astype(BF16)
    lo = (rest - mid.astype(F32)).astype(BF16)
    return (jnp.dot(hi, rhs_bf16, preferred_element_type=F32)
            + jnp.dot(mid, rhs_bf16, preferred_element_type=F32)
            + jnp.dot(lo, rhs_bf16, preferred_element_type=F32))


def _dt_kernel(xa_ref, xb_ref, nw_ref, w_ref, b_ref, ex_ref, xn_ref, dt_ref, dtx_ref, *, n_first):
    x = _stacked_block(pl.program_id(0), n_first, xa_ref, xb_ref)
    ms = jnp.mean(x * x, axis=-1, keepdims=True)
    xn = (x * lax.rsqrt(ms + NORM_EPS) * nw_ref[...]).astype(BF16)
    xn_ref[...] = xn
    a = lax.dot_general(xn, w_ref[...], (((1,), (1,)), ((), ())),
                        preferred_element_type=F32) + b_ref[...]
    dt = jnp.maximum(a, 0.0) + jnp.log1p(jnp.exp(-jnp.abs(a)))
    dt_ref[...] = dt
    n = dt.shape[1]
    hi = dt.astype(BF16)
    rest = dt - hi.astype(F32)
    mid = rest.astype(BF16)
    lo = (rest - mid.astype(F32)).astype(BF16)
    dtx_ref[...] = (jnp.dot(jnp.concatenate([hi, mid], axis=1), ex_ref[...], preferred_element_type=F32)
                    + jnp.dot(lo, ex_ref[0:n, :], preferred_element_type=F32))


def _norm_dt_proj(xa, xb, norm_w, w_t, b):
    (ma, k), mb = xa.shape, xb.shape[0]
    m = ma + mb
    n = w_t.shape[0]
    bm = _row_block(ma, mb, (256, 128, 64, 32, 16, 8))
    n_first = ma // bm
    spec_a, spec_b = _stacked_specs((bm, k), n_first, 0)
    expand = (jnp.arange(SSD_INNER)[None, :] // SSD_HEADDIM == jnp.arange(n)[:, None]).astype(BF16)
    expand = jnp.concatenate([expand, expand], axis=0)
    return pl.pallas_call(
        functools.partial(_dt_kernel, n_first=n_first),
        grid=(m // bm,),
        in_specs=[spec_a, spec_b,
                  pl.BlockSpec((1, k), lambda i: (0, 0)),
                  pl.BlockSpec((n, k), lambda i: (0, 0)),
                  pl.BlockSpec((1, n), lambda i: (0, 0)),
                  pl.BlockSpec((2 * n, SSD_INNER), lambda i: (0, 0))],
        out_specs=[pl.BlockSpec((bm, k), lambda i: (i, 0)),
                   pl.BlockSpec((bm, n), lambda i: (i, 0)),
                   pl.BlockSpec((bm, SSD_INNER), lambda i: (i, 0))],
        out_shape=[jax.ShapeDtypeStruct((m, k), BF16),
                   jax.ShapeDtypeStruct((m, n), F32),
                   jax.ShapeDtypeStruct((m, SSD_INNER), F32)],
        compiler_params=_params("parallel"),
        name="norm_dt_proj",
    )(xa, xb, norm_w.reshape(1, k), w_t, b, expand)


def _merge_kernel(ya1_ref, ya2_ref, yb1_ref, yb2_ref, wa_ref, wb_ref, ga_ref, gb_ref, o_ref, *, n_first):
    i = pl.program_id(1)
    ya = _stacked_block(i, n_first, ya1_ref, ya2_ref)
    yb = _stacked_block(i, n_first, yb1_ref, yb2_ref)
    pa = jnp.dot(ya, wa_ref[...], preferred_element_type=F32)
    pb = jnp.dot(yb, wb_ref[...], preferred_element_type=F32)
    o_ref[...] = (_sigmoid(ga_ref[...]) * pa + _sigmoid(gb_ref[...]) * pb).astype(o_ref.dtype)


def _gated_merge(ya_parts, yb_parts, wa, wb, proj):
    (m1, k), m2 = ya_parts[0].shape, ya_parts[1].shape[0]
    n = wa.shape[1]
    bm = _row_block(m1, m2, (512, 256, 128, 64, 32, 16, 8))
    bn = 512
    n_first = m1 // bm
    y_first, y_second = _stacked_specs((bm, k), n_first, 1)
    return pl.pallas_call(
        functools.partial(_merge_kernel, n_first=n_first),
        grid=(n // bn, (m1 + m2) // bm),
        in_specs=[y_first, y_second, y_first, y_second,
                  pl.BlockSpec((k, bn), lambda j, i: (0, j)),
                  pl.BlockSpec((k, bn), lambda j, i: (0, j)),
                  pl.BlockSpec((bm, bn), lambda j, i: (i, COL_GA // bn + j)),
                  pl.BlockSpec((bm, bn), lambda j, i: (i, COL_GB // bn + j))],
        out_specs=pl.BlockSpec((bm, bn), lambda j, i: (i, j)),
        out_shape=jax.ShapeDtypeStruct((m1 + m2, n), BF16),
        compiler_params=_params("parallel", "parallel"),
        name="gated_merge",
    )(ya_parts[0], ya_parts[1], yb_parts[0], yb_parts[1], wa, wb, proj, proj)


def _residual_matmul_kernel(x_ref, w_ref, r1_ref, r2_ref, o_ref, *, n_first):
    r = _stacked_block(pl.program_id(1), n_first, r1_ref, r2_ref)
    o_ref[...] = r + jnp.dot(x_ref[...], w_ref[...], preferred_element_type=F32)


def _residual_matmul(x, w, r_parts):
    m, k = x.shape
    n = w.shape[1]
    m1, m2 = r_parts[0].shape[0], r_parts[1].shape[0]
    bm = _row_block(m1, m2, (512, 256, 128, 64, 32, 16, 8))
    bn = 1024
    n_first = m1 // bm
    r_first, r_second = _stacked_specs((bm, bn), n_first, 1, col_of=lambda j, i: j)
    return pl.pallas_call(
        functools.partial(_residual_matmul_kernel, n_first=n_first),
        grid=(n // bn, m // bm),
        in_specs=[pl.BlockSpec((bm, k), lambda j, i: (i, 0)),
                  pl.BlockSpec((k, bn), lambda j, i: (0, j)),
                  r_first, r_second],
        out_specs=pl.BlockSpec((bm, bn), lambda j, i: (i, j)),
        out_shape=jax.ShapeDtypeStruct((m, n), F32),
        compiler_params=_params("parallel", "parallel"),
        name="residual_matmul",
    )(x, w, r_parts[0], r_parts[1])


CONV_PAD = 8


def _conv_kernel(u_ref, buf_ref, w_ref, b_ref, o_ref, cache_ref, prev_ref, *, tb):
    t = pl.program_id(2)
    lo = CONV_PAD - (SSD_CONV - 1)

    @pl.when(t == 0)
    def _():
        prev_ref[lo:CONV_PAD, :] = buf_ref[0]

    u = u_ref[...]
    first_row = lax.broadcasted_iota(jnp.int32, (CONV_PAD, u.shape[1]), 0) == 0
    acc = b_ref[...] + u * w_ref[SSD_CONV - 1:SSD_CONV, :]
    shifted = u
    for k in range(1, SSD_CONV):
        shifted = pltpu.roll(shifted, 1, axis=0)
        head = jnp.where(first_row, prev_ref[CONV_PAD - k:CONV_PAD - k + 1, :], shifted[0:CONV_PAD])
        shifted = jnp.concatenate([head, shifted[CONV_PAD:]], axis=0)
        acc = acc + shifted * w_ref[SSD_CONV - 1 - k:SSD_CONV - k, :]
    o_ref[...] = _silu(acc)
    prev_ref[...] = u[tb - CONV_PAD:tb]

    @pl.when(t == pl.num_programs(2) - 1)
    def _():
        cache_ref[0] = prev_ref[lo:CONV_PAD, :]


def _conv_silu(proj, row_off, bt, seqlen, buf, conv_w, conv_b):
    tb = _largest_divisor(seqlen, (512, 256, 128, 64, 32))
    cb = 2048
    nt = seqlen // tb
    roff = row_off // tb
    coff = COL_XBC // cb
    return pl.pallas_call(
        functools.partial(_conv_kernel, tb=tb),
        grid=(SSD_CONV_DIM // cb, bt, nt),
        in_specs=[pl.BlockSpec((tb, cb), lambda c, b, t: (roff + b * nt + t, coff + c)),
                  pl.BlockSpec((1, SSD_CONV - 1, cb), lambda c, b, t: (b, 0, c)),
                  pl.BlockSpec((SSD_CONV, cb), lambda c, b, t: (0, c)),
                  pl.BlockSpec((1, cb), lambda c, b, t: (0, c))],
        out_specs=[pl.BlockSpec((tb, cb), lambda c, b, t: (b * nt + t, c)),
                   pl.BlockSpec((1, SSD_CONV - 1, cb), lambda c, b, t: (b, 0, c))],
        out_shape=[jax.ShapeDtypeStruct((bt * seqlen, SSD_CONV_DIM), F32),
                   jax.ShapeDtypeStruct((bt, SSD_CONV - 1, SSD_CONV_DIM), F32)],
        scratch_shapes=[pltpu.VMEM((CONV_PAD, cb), F32)],
        compiler_params=_params("parallel", "parallel", "arbitrary"),
        name="conv_silu",
    )(proj, buf, conv_w, conv_b.reshape(1, SSD_CONV_DIM))


def _ssd_kernel(x_ref, b_ref, c_ref, z_ref, dtx_ref, dtt_ref, alx_ref, alt_ref, dsk_ref, nw_ref,
                s0_ref, y_ref, sout_ref, st_ref, *, q, nchunk, ng):
    t = pl.program_id(2)
    p, r, gw, n = SSD_HEADDIM, SSD_HEADS_PER_GROUP, SSD_GROUP_WIDTH, SSD_STATE

    @pl.when(t == 0)
    def _():
        st_ref[...] = s0_ref[0]

    row = lax.broadcasted_iota(jnp.int32, (q, q), 0)
    col = lax.broadcasted_iota(jnp.int32, (q, q), 1)
    lower = row >= col
    tril = _tril3(q)
    triu = (row <= col).astype(BF16)
    a_x = -jnp.exp(alx_ref[...])
    a_t = -jnp.exp(alt_ref[...]).reshape(ng * r, 1)
    d_skip = dsk_ref[...]
    norm_w = nw_ref[...]
    nt_dims = (((1,), (1,)), ((), ()))
    tn_dims = (((0,), (0,)), ((), ()))

    def chunk(ci, carry):
        rows = pl.ds(pl.multiple_of(ci * q, q), q)
        x = x_ref[rows, :]
        dtx = dtx_ref[rows, :]
        cum = _cumsum_rows(tril, dtx * a_x)
        cum_t = _split3_dot(dtt_ref[:, ci].reshape(ng * r, q) * a_t, triu)
        cum_last = cum[q - 1:q, :]
        decay_in = jnp.exp(cum)
        xdt = (x * dtx).astype(BF16)
        xt = (x * (jnp.exp(cum_last - cum) * dtx)).astype(BF16)
        decay_st = jnp.exp(cum_last)
        gate = _silu(z_ref[rows, :])
        skip = d_skip * x
        for gi in range(ng):
            cols = slice(gi * gw, (gi + 1) * gw)
            bm = b_ref[rows, gi * n:(gi + 1) * n].astype(BF16)
            cm = c_ref[rows, gi * n:(gi + 1) * n].astype(BF16)
            cb = lax.dot_general(cm, bm, nt_dims, preferred_element_type=F32)
            st = st_ref[gi]
            y = jnp.dot(cm, st.astype(BF16), preferred_element_type=F32) * decay_in[:, cols]
            parts = []
            for h in range(r):
                c0 = gi * gw + h * p
                seg = cum[:, c0:c0 + q] - cum_t[gi * r + h:gi * r + h + 1, :]
                mix = (cb * jnp.where(lower, jnp.exp(seg), 0.0)).astype(BF16)
                parts.append(jnp.dot(mix, xdt[:, c0:c0 + p], preferred_element_type=F32))
            y = (y + jnp.concatenate(parts, axis=1) + skip[:, cols]) * gate[:, cols]
            ms = jnp.mean(y * y, axis=-1, keepdims=True)
            y_ref[rows, cols] = (y * lax.rsqrt(ms + NORM_EPS) * norm_w[:, cols]).astype(y_ref.dtype)
            st_ref[gi] = st * decay_st[:, cols] + lax.dot_general(
                bm, xt[:, cols], tn_dims, preferred_element_type=F32)
        return carry

    lax.fori_loop(0, nchunk, chunk, 0, unroll=SCAN_UNROLL if nchunk % SCAN_UNROLL == 0 else 1)

    @pl.when(t == pl.num_programs(2) - 1)
    def _():
        sout_ref[0] = st_ref[...]


def _ssd_scan(xconv, proj, dtx, dt, row_off, bt, seqlen, a_log, d_skip, ssd_norm_w, state0):
    g, r, p, n, gw = SSD_GROUPS, SSD_HEADS_PER_GROUP, SSD_HEADDIM, SSD_STATE, SSD_GROUP_WIDTH
    q = min(CHUNK, seqlen)
    ng = SSD_GROUPS_PER_STEP if q == CHUNK else 2 * SSD_GROUPS_PER_STEP
    tb = _largest_divisor(seqlen, (512, 256, 128, 64, 32))
    nchunk = tb // q
    nt = seqlen // tb
    rows = bt * seqlen
    roff = row_off // tb
    dtt = dt[row_off:row_off + rows, :SSD_HEADS].reshape(rows // q, q, g, r).transpose(2, 0, 3, 1)
    alx = jnp.repeat(a_log, p).reshape(1, SSD_INNER)
    alt = a_log.reshape(g, r, 1)
    dsk = jnp.repeat(d_skip, p).reshape(1, SSD_INNER)
    s0 = state0.reshape(bt, g, r, p, n).transpose(0, 1, 4, 2, 3).reshape(bt, g, n, gw)
    bcol = SSD_INNER // (ng * n)
    ccol = bcol + g // ng
    wide = ng * gw
    y, sout = pl.pallas_call(
        functools.partial(_ssd_kernel, q=q, nchunk=nchunk, ng=ng),
        grid=(bt, g // ng, nt),
        in_specs=[pl.BlockSpec((tb, wide), lambda b, gi, t: (b * nt + t, gi)),
                  pl.BlockSpec((tb, ng * n), lambda b, gi, t: (b * nt + t, bcol + gi)),
                  pl.BlockSpec((tb, ng * n), lambda b, gi, t: (b * nt + t, ccol + gi)),
                  pl.BlockSpec((tb, wide), lambda b, gi, t: (roff + b * nt + t, COL_Z // wide + gi)),
                  pl.BlockSpec((tb, wide), lambda b, gi, t: (roff + b * nt + t, gi)),
                  pl.BlockSpec((ng, nchunk, r, q), lambda b, gi, t: (gi, b * nt + t, 0, 0)),
                  pl.BlockSpec((1, wide), lambda b, gi, t: (0, gi)),
                  pl.BlockSpec((ng, r, 1), lambda b, gi, t: (gi, 0, 0)),
                  pl.BlockSpec((1, wide), lambda b, gi, t: (0, gi)),
                  pl.BlockSpec((1, wide), lambda b, gi, t: (0, gi)),
                  pl.BlockSpec((1, ng, n, gw), lambda b, gi, t: (b, gi, 0, 0))],
        out_specs=[pl.BlockSpec((tb, wide), lambda b, gi, t: (b * nt + t, gi)),
                   pl.BlockSpec((1, ng, n, gw), lambda b, gi, t: (b, gi, 0, 0))],
        out_shape=[jax.ShapeDtypeStruct((rows, SSD_INNER), BF16),
                   jax.ShapeDtypeStruct((bt, g, n, gw), F32)],
        scratch_shapes=[pltpu.VMEM((ng, n, gw), F32)],
        compiler_params=_params("parallel", "parallel", "arbitrary"),
        name="ssd_scan",
    )(xconv, xconv, xconv, proj, dtx, dtt, alx, alt, dsk, ssd_norm_w.reshape(1, SSD_INNER), s0)
    state = sout.reshape(bt, g, n, r, p).transpose(0, 1, 3, 4, 2).reshape(bt, SSD_HEADS, p, n)
    return y, state


def _tril3(q):
    row = lax.broadcasted_iota(jnp.int32, (q, 3 * q), 0)
    col = lax.broadcasted_iota(jnp.int32, (q, 3 * q), 1)
    col = jnp.where(col >= 2 * q, col - 2 * q, jnp.where(col >= q, col - q, col))
    return (row >= col).astype(BF16)


def _cumsum_rows(tril3_bf16, x):
    hi = x.astype(BF16)
    rest = x - hi.astype(F32)
    mid = rest.astype(BF16)
    lo = (rest - mid.astype(F32)).astype(BF16)
    return jnp.dot(tril3_bf16, jnp.concatenate([hi, mid, lo], axis=0), preferred_element_type=F32)


def _hgrn_kernel(q_ref, f_ref, i_ref, g_ref, lbt_ref, nw_ref, s0_ref, o_ref, sout_ref, st_ref,
                 *, q, nchunk, nh):
    t = pl.program_id(2)

    @pl.when(t == 0)
    def _():
        st_ref[...] = s0_ref[0]

    hq = q // 2
    row = lax.broadcasted_iota(jnp.int32, (q, q), 0)
    col = lax.broadcasted_iota(jnp.int32, (q, q), 1)
    lower = row >= col
    tril = _tril3(q)
    left_half = lax.broadcasted_iota(jnp.int32, (hq, LANES), 1) < hq
    k_pad = jnp.zeros((LANES - 3 * hq, nh * HG_K), BF16)
    tab = lbt_ref[...]
    e = jnp.exp(tab - jnp.max(tab, axis=0, keepdims=True))
    lb = e[0:1, :] / jnp.sum(e, axis=0, keepdims=True)
    norm_w = nw_ref[...]
    nt_dims = (((1,), (1,)), ((), ()))
    tn_dims = (((0,), (0,)), ((), ()))

    def decayed(x, cum_rows, ref_row, sign):
        return (x * jnp.exp(sign * (cum_rows - ref_row))).astype(BF16)

    def chunk(ci, carry):
        rows = pl.ds(pl.multiple_of(ci * q, q), q)
        qq = _silu(q_ref[rows, :]) * (HG_K ** -0.5)
        f = lb + (1.0 - lb) * _sigmoid(f_ref[rows, :])
        kk = 1.0 - f
        v = i_ref[rows, :].astype(BF16)
        gate = _silu(g_ref[rows, :])
        cum = _cumsum_rows(tril, jnp.log(f))
        top, bot = slice(0, hq), slice(hq, q)
        mid_top = cum[hq // 2 - 1:hq // 2, :]
        mid_bot = cum[hq + hq // 2 - 1:hq + hq // 2, :]
        edge = cum[hq - 1:hq, :]
        cum_last = cum[q - 1:q, :]
        q_tt = decayed(qq[top], cum[top], mid_top, 1.0)
        k_tt = decayed(kk[top], cum[top], mid_top, -1.0)
        q_bb = decayed(qq[bot], cum[bot], mid_bot, 1.0)
        k_bb = decayed(kk[bot], cum[bot], mid_bot, -1.0)
        q_bt = decayed(qq[bot], cum[bot], edge, 1.0)
        k_bt = decayed(kk[top], cum[top], edge, -1.0)
        qe = (qq * jnp.exp(cum)).astype(BF16)
        kt = (kk * jnp.exp(cum_last - cum)).astype(BF16)
        dec = jnp.exp(cum_last)
        q_stack = jnp.concatenate([q_tt, q_bb, q_bt], axis=0)
        k_stack = jnp.concatenate([k_tt, k_bb, k_bt, k_pad], axis=0)
        for h in range(nh):
            sl = slice(h * HG_K, (h + 1) * HG_K)
            s_all = lax.dot_general(q_stack[:, sl], k_stack[:, sl], nt_dims,
                                    preferred_element_type=F32)
            bottom_left = pltpu.roll(s_all[2 * hq:3 * hq, :], LANES - 2 * hq, axis=1)
            s_bot = jnp.where(left_half, bottom_left, s_all[hq:2 * hq, :])
            att = jnp.concatenate([s_all[0:hq, 0:q], s_bot[:, 0:q]], axis=0)
            att = jnp.where(lower, att, 0.0).astype(BF16)
            st = st_ref[h]
            o = jnp.dot(att, v[:, sl], preferred_element_type=F32) + lax.dot_general(
                qe[:, sl], st.astype(BF16), nt_dims, preferred_element_type=F32)
            ms = jnp.mean(o * o, axis=-1, keepdims=True)
            o_ref[rows, sl] = (o * lax.rsqrt(ms + NORM_EPS) * norm_w * gate[:, sl]).astype(o_ref.dtype)
            st_ref[h] = st * dec[:, sl] + lax.dot_general(v[:, sl], kt[:, sl], tn_dims,
                                                          preferred_element_type=F32)
        return carry

    lax.fori_loop(0, nchunk, chunk, 0, unroll=SCAN_UNROLL if nchunk % SCAN_UNROLL == 0 else 1)

    @pl.when(t == pl.num_programs(2) - 1)
    def _():
        sout_ref[0] = st_ref[...]


def _hgrn_scan(proj, row_off, bt, seqlen, lb_table, hg_norm_w, state0):
    q = min(CHUNK, seqlen)
    tb = _largest_divisor(seqlen, (512, 256, 128, 64, 32))
    nchunk = tb // q
    nt = seqlen // tb
    rows = bt * seqlen
    roff = row_off // tb
    nh = HGRN_HEADS_PER_STEP if q == CHUNK else 2 * HGRN_HEADS_PER_STEP
    hw = nh * HG_K
    s0 = state0.transpose(0, 1, 3, 2)
    nrow = lb_table.shape[0]

    def col_spec(col0):
        return pl.BlockSpec((tb, hw), lambda b, h, t: (roff + b * nt + t, col0 // hw + h))

    o, sout = pl.pallas_call(
        functools.partial(_hgrn_kernel, q=q, nchunk=nchunk, nh=nh),
        grid=(bt, HG_HEADS // nh, nt),
        in_specs=[col_spec(COL_HQ), col_spec(COL_HF), col_spec(COL_HI), col_spec(COL_HG),
                  pl.BlockSpec((nrow, hw), lambda b, h, t: (0, h)),
                  pl.BlockSpec((1, HG_V), lambda b, h, t: (0, 0)),
                  pl.BlockSpec((1, nh, HG_V, HG_K), lambda b, h, t: (b, h, 0, 0))],
        out_specs=[pl.BlockSpec((tb, hw), lambda b, h, t: (b * nt + t, h)),
                   pl.BlockSpec((1, nh, HG_V, HG_K), lambda b, h, t: (b, h, 0, 0))],
        out_shape=[jax.ShapeDtypeStruct((rows, HG_HEADS * HG_V), BF16),
                   jax.ShapeDtypeStruct((bt, HG_HEADS, HG_V, HG_K), F32)],
        scratch_shapes=[pltpu.VMEM((nh, HG_V, HG_K), F32)],
        compiler_params=_params("parallel", "parallel", "arbitrary"),
        name="hgrn_scan",
    )(proj, proj, proj, proj, lb_table, hg_norm_w.reshape(1, HG_V), s0)
    return o, sout.transpose(0, 1, 3, 2)


def _top_values(s, count):
    vals = []
    work = s
    for _ in range(count):
        m = jnp.max(work, axis=0, keepdims=True)
        vals.append(m)
        work = jnp.where(work == m, -jnp.inf, work)
    return jnp.concatenate(vals, axis=0)


def _oddeven_merge(lo, hi, r):
    step = r * 2
    if step < hi - lo:
        yield from _oddeven_merge(lo, hi, step)
        yield from _oddeven_merge(lo + r, hi, step)
        yield from [(i, i + r) for i in range(lo + r, hi - r, step)]
    else:
        yield (lo, lo + r)


def _oddeven_merge_sort(lo, hi):
    if hi - lo >= 1:
        mid = lo + (hi - lo) // 2
        yield from _oddeven_merge_sort(lo, mid)
        yield from _oddeven_merge_sort(mid + 1, hi)
        yield from _oddeven_merge(lo, hi, 1)


SUBLANES = 8
_SORT_NETWORK = tuple(_oddeven_merge_sort(0, PEER_KEYS // SUBLANES - 1))


def _compare_exchange(v, i, j):
    v[i], v[j] = jnp.maximum(v[i], v[j]), jnp.minimum(v[i], v[j])


def _top_sorted(s):
    n = PEER_KEYS // SUBLANES
    assert n == PEER_TOPK
    x = s.reshape(n, SUBLANES, s.shape[-1])
    v = [x[k] for k in range(n)]
    for i, j in _SORT_NETWORK:
        _compare_exchange(v, i, j)
    shift = 1
    while shift < SUBLANES:
        w = [pltpu.roll(a, shift, axis=0) for a in v]
        v = [jnp.maximum(v[k], w[n - 1 - k]) for k in range(n)]
        d = n // 2
        while d >= 1:
            for k in range(n):
                if not k & d:
                    _compare_exchange(v, k, k + d)
            d //= 2
        shift *= 2
    return jnp.concatenate([a[0:1] for a in v], axis=0)


def _peer_query_kernel(w_ref, x_ref, sk_ref, thr_ref, e0_ref, s1_ref, e1_ref):
    half = PEER_QDIM // 2
    nh = thr_ref.shape[0]
    qt = jnp.dot(w_ref[...], x_ref[...], preferred_element_type=F32)

    def split(a):
        hi = a.astype(BF16)
        return hi, (a - hi.astype(F32)).astype(BF16)

    def scores(keys, q):
        (k_hi, k_lo), (q_hi, q_lo) = keys, split(q)
        return (jnp.dot(k_hi, q_hi, preferred_element_type=F32)
                + jnp.dot(k_hi, q_lo, preferred_element_type=F32)
                + jnp.dot(k_lo, q_hi, preferred_element_type=F32))

    keys0, keys1 = split(sk_ref[0]), split(sk_ref[1])
    for h in range(nh):
        s0 = scores(keys0, qt[h * PEER_QDIM:h * PEER_QDIM + half])
        s1 = scores(keys1, qt[h * PEER_QDIM + half:(h + 1) * PEER_QDIM])
        sv0 = _top_sorted(s0)
        sv1 = _top_sorted(s1)
        cand = [sv0[0:1] + sv1]
        for a in range(1, 8):
            cand.append(sv0[a:a + 1] + sv1[0:8])
        cand.append(sv0[8:16] + sv1[0:1])
        top = _top_values(jnp.concatenate(cand, axis=0), PEER_TOPK)
        z = jnp.sum(jnp.exp(top - top[0:1]), axis=0, keepdims=True)
        tau = top[PEER_TOPK - 1:PEER_TOPK]
        thr = jnp.full(s0.shape, jnp.inf, F32)
        for b in range(PEER_TOPK):
            thr = jnp.where(s0 + sv1[b:b + 1] >= tau, sv1[b:b + 1], thr)
        thr_ref[h] = thr.reshape(thr_ref.shape[1:])
        s1_ref[h] = s1
        e0_ref[h] = (jnp.exp(s0 - sv0[0:1]) / z).reshape(e0_ref.shape[1:])
        e1_ref[h] = jnp.exp(s1 - sv1[0:1])


def _peer_query(w_pq_t, xn_t, sub_keys):
    d, t = xn_t.shape
    tb = _largest_divisor(t, (256, 128))
    nh = PEER_QUERY_HEADS_PER_STEP
    keyed = jax.ShapeDtypeStruct((PEER_HEADS, PEER_KEYS, t), F32)
    key_spec = pl.BlockSpec((nh, PEER_KEYS, tb), lambda i, h: (h, 0, i))
    nb = PEER_EXPERT_KEY_ROWS
    grouped = jax.ShapeDtypeStruct((PEER_HEADS, PEER_KEYS // nb, nb, t), F32)
    group_spec = pl.BlockSpec((nh, PEER_KEYS // nb, nb, tb), lambda i, h: (h, 0, 0, i))
    return pl.pallas_call(
        _peer_query_kernel,
        grid=(t // tb, PEER_HEADS // nh),
        in_specs=[pl.BlockSpec((nh * PEER_QDIM, d), lambda i, h: (h, 0)),
                  pl.BlockSpec((d, tb), lambda i, h: (0, i)),
                  pl.BlockSpec((2, PEER_KEYS, PEER_QDIM // 2), lambda i, h: (0, 0, 0))],
        out_specs=[group_spec, group_spec, key_spec, key_spec],
        out_shape=[grouped, grouped, keyed, keyed],
        compiler_params=_params("parallel", "arbitrary"),
        name="peer_query",
    )(w_pq_t, xn_t, sub_keys)


def _gelu_tanh(x):
    return 0.5 * x * (1.0 + jnp.tanh(0.7978845608028654 * (x + 0.044715 * (x * x * x))))


def _peer_expert_kernel(x_ref, u_ref, v_ref, thr_ref, e0_ref, s1_ref, e1_ref, o_ref, w_ref, *, nb, tb):
    e = pl.program_id(1)

    @pl.when(e == 0)
    def _():
        o_ref[...] = jnp.zeros_like(o_ref)

    act = _gelu_tanh(jnp.dot(u_ref[...], x_ref[...], preferred_element_type=F32))
    for i in range(nb):
        for tl in range(tb // LANES):
            lanes = pl.ds(tl * LANES, LANES)
            acc = jnp.zeros((PEER_KEYS, LANES), F32)
            for h in range(PEER_HEADS):
                thr = thr_ref[h, 0, i:i + 1, lanes]
                c0 = e0_ref[h, 0, i:i + 1, lanes]
                acc = acc + jnp.where(s1_ref[h, :, lanes] >= thr, e1_ref[h, :, lanes] * c0, 0.0)
            w_ref[i * PEER_KEYS:(i + 1) * PEER_KEYS, lanes] = acc
    g = (w_ref[...] * act).astype(BF16)
    o_ref[...] += lax.dot_general(g, v_ref[...], (((0,), (0,)), ((), ())),
                                  preferred_element_type=F32)


def _peer_experts(xn_t, u, v, thr, e0, s1, e1):
    d, t = xn_t.shape
    n_exp = u.shape[0]
    tb = _largest_divisor(t, (512, 256, 128))
    nb = PEER_EXPERT_KEY_ROWS
    eb = nb * PEER_KEYS
    once = pl.Buffered(1)
    keyed = pl.BlockSpec((PEER_HEADS, PEER_KEYS, tb), lambda i, e: (0, 0, i), pipeline_mode=once)
    rows = pl.BlockSpec((PEER_HEADS, 1, nb, tb), lambda i, e: (0, e, 0, i))
    return pl.pallas_call(
        functools.partial(_peer_expert_kernel, nb=nb, tb=tb),
        grid=(t // tb, n_exp // eb),
        in_specs=[pl.BlockSpec((d, tb), lambda i, e: (0, i), pipeline_mode=once),
                  pl.BlockSpec((eb, d), lambda i, e: (e, 0)),
                  pl.BlockSpec((eb, d), lambda i, e: (e, 0)),
                  rows, rows, keyed, keyed],
        out_specs=pl.BlockSpec((tb, d), lambda i, e: (i, 0)),
        out_shape=jax.ShapeDtypeStruct((t, d), F32),
        scratch_shapes=[pltpu.VMEM((eb, tb), F32)],
        compiler_params=_params("parallel", "arbitrary"),
        name="peer_experts",
    )(xn_t, u, v, thr, e0, s1, e1)


def _branches(proj, dtx, dt, row_off, bt, seqlen, conv_buf, s_ssd, s_hg, conv_w, conv_b, a_log, d_skip,
              ssd_norm_w, lb_table, hg_norm_w):
    xconv, new_conv = _conv_silu(proj, row_off, bt, seqlen, conv_buf, conv_w, conv_b)
    y_ssd, s_ssd_new = _ssd_scan(xconv, proj, dtx, dt, row_off, bt, seqlen,
                                 a_log, d_skip, ssd_norm_w, s_ssd)
    o_hg, s_hg_new = _hgrn_scan(proj, row_off, bt, seqlen, lb_table, hg_norm_w, s_hg)
    return y_ssd, o_hg, s_ssd_new, new_conv, s_hg_new


def kernel(x_prompt, x_sample, state_ssd, cache_ssd_conv, state_hgrn, lb_table, norm1_w, w_in, conv_w, conv_b, dt_bias, a_log, d_skip, ssd_norm_w, w_ssd_out, hg_norm_w, w_hg_out, w_o, norm2_w, w_pq, sub_keys, u_experts, v_experts, final_norm_w):
    assert w_in.shape[0] == 1 and lb_table.shape[0] == 2, "single-layer stack"
    bp, lp, d = x_prompt.shape
    bs, ls, _ = x_sample.shape
    tp, ts = bp * lp, bs * ls
    x_parts = (x_prompt.reshape(tp, d), x_sample.reshape(ts, d))

    w_in_t = w_in[0].T
    w_dt = jnp.pad(w_in_t[DT_COL:DT_COL + SSD_HEADS], ((0, LANES - SSD_HEADS), (0, 0))).astype(BF16)
    b_dt = jnp.pad(dt_bias[0], (0, LANES - SSD_HEADS)).reshape(1, LANES)

    xn, dt, dtx = _norm_dt_proj(*x_parts, norm1_w[0], w_dt, b_dt)
    proj, u_bf16, v_bf16 = _in_proj(xn, w_in_t, u_experts[0], v_experts[0])

    zeros = functools.partial(jnp.zeros, dtype=F32)
    common = (conv_w[0], conv_b[0], a_log[0], d_skip[0], ssd_norm_w[0], lb_table, hg_norm_w[0])
    ya_p, yb_p, ssd_p, conv_p, hg_p = _branches(
        proj, dtx, dt, 0, bp, lp,
        zeros((bp, SSD_CONV - 1, SSD_CONV_DIM)), zeros((bp, SSD_HEADS, SSD_HEADDIM, SSD_STATE)),
        zeros((bp, HG_HEADS, HG_K, HG_V)), *common)
    ya_s, yb_s, ssd_s, conv_s, hg_s = _branches(
        proj, dtx, dt, tp, bs, ls, cache_ssd_conv[0], state_ssd[0], state_hgrn[0], *common)

    merged = _gated_merge((ya_p, ya_s), (yb_p, yb_s), w_ssd_out[0].astype(BF16),
                          w_hg_out[0].astype(BF16), proj)
    h = _residual_matmul(merged, w_o[0].astype(BF16), x_parts)

    hn_t = _rmsnorm_t(h, norm2_w[0], BF16)
    thr, e0, s1, e1 = _peer_query(w_pq[0].T.astype(BF16), hn_t, sub_keys[0])
    peer = _peer_experts(hn_t, u_bf16, v_bf16, thr, e0, s1, e1)
    y_prompt = _add_rmsnorm(h, peer, final_norm_w, 0, tp).reshape(bp, lp, d)
    y_sample = _add_rmsnorm(h, peer, final_norm_w, tp, ts).reshape(bs, ls, d)
    return (y_prompt, y_sample, ssd_p[None], conv_p[None], hg_p[None],
            ssd_s[None], conv_s[None], hg_s[None])
```
